```python
import math
import jax, jax.numpy as jnp
from jax import lax
import numpy as np

D_MODEL = 1024
BATCH = 8
SEQ = 2048
DEPTH = 4
DEC_BATCH = 128
DEC_SEQ = 4
PAST_LEN = 16384
PAGE_SIZE = 128

MIX = D_MODEL
N_GROUPS = 4
W_GROUP = MIX // N_GROUPS
H_A = 4
DV_A = W_GROUP // H_A
DK_A = DV_A // 2
R_GLA = 16
GLA_TAU = 16.0
CHUNK = 64
POOL_WINDOWS = (2, 4, 8, 16)
POOL_GROUP = W_GROUP // len(POOL_WINDOWS)
POOL_BUF = max(POOL_WINDOWS) - 1
H_C = 4
N_C = W_GROUP // H_C
R_W = W_GROUP // 16
R_AA = W_GROUP // 16
R_G = W_GROUP // 8
RWKV_LN_EPS = 64e-5
H_D = 4
D_D = W_GROUP // H_D
CONV_D = 4
D_FF = 256 * ((8 * D_MODEL // 3 + 255) // 256)
FFN_CONV = 3
ALPHA = (2 * DEPTH) ** 0.25
BETA = (8 * DEPTH) ** -0.25
GLA_SIZES = (H_A * DK_A, H_A * DK_A, W_GROUP, W_GROUP, R_GLA)
RWKV_SIZES = (W_GROUP, W_GROUP, W_GROUP, R_W, R_AA, R_G)
RWKV_COLS = sum(RWKV_SIZES)
MLSTM_SIZES = (2 * W_GROUP, W_GROUP, W_GROUP, H_D, H_D)
GROUP_SIZES = (sum(GLA_SIZES), W_GROUP, RWKV_COLS, sum(MLSTM_SIZES))
IN_COLS = sum(GROUP_SIZES)

kernel_name = 'hybrid_gla_pool_rwkv7_mlstm_step'


def _split(z, sizes):
    cuts = [int(c) for c in np.cumsum(sizes)[:-1]]
    return jnp.split(z, cuts, axis=-1)


def _f32(t):
    return t.astype(jnp.float32)


def layer_norm(x, g, b, eps=1e-5):
    xf = _f32(x)
    mu = jnp.mean(xf, -1, keepdims=True)
    var = jnp.mean(jnp.square(xf - mu), -1, keepdims=True)
    return ((xf - mu) * lax.rsqrt(var + eps) * g + b).astype(x.dtype)


def head_rms_norm(o, g, eps=1e-6):
    return o * lax.rsqrt(jnp.mean(o * o, -1, keepdims=True) + eps) * g.reshape(o.shape[-2:])


def head_layer_norm(o, g, b, eps):
    mu = jnp.mean(o, -1, keepdims=True)
    var = jnp.mean(jnp.square(o - mu), -1, keepdims=True)
    return (o - mu) * lax.rsqrt(var + eps) * g.reshape(o.shape[-2:]) + b.reshape(o.shape[-2:])


def causal_dwconv(z, buf, w, b):
    K, L = w.shape[0], z.shape[1]
    ext = jnp.concatenate([buf.astype(z.dtype), z], axis=1)
    out = b + sum(ext[:, j:j + L] * w[j] for j in range(K))
    return out, ext[:, L:]


def _to_chunks(t, c):
    B, L, H = t.shape[:3]
    t = t.reshape((B, L // c, c, H) + t.shape[3:])
    return t.transpose((1, 0, 3, 2) + tuple(range(4, t.ndim)))


def _from_chunks(t):
    n, B, H, c, d = t.shape
    return t.transpose(1, 0, 3, 2, 4).reshape(B, n * c, H, d)


def gla_mixer(q, k, v, g, lr, S0, w_a2, b_a, norm_g):
    B, L, _ = q.shape
    q = _f32(q).reshape(B, L, H_A, DK_A) * DK_A ** -0.5
    k = _f32(k).reshape(B, L, H_A, DK_A)
    v = _f32(v).reshape(B, L, H_A, DV_A)
    lg = (jax.nn.log_sigmoid(_f32(lr) @ w_a2 + b_a) / GLA_TAU).reshape(B, L, H_A, DK_A)
    c = math.gcd(L, CHUNK)
    causal = jnp.tril(jnp.ones((c, c), dtype=bool))[:, :, None]

    def step(S, inp):
        qc, kc, vc, lc = inp
        b = jnp.cumsum(lc, axis=2)
        o = jnp.einsum('bhtk,bhkv->bhtv', qc * jnp.exp(b), S)
        dec = jnp.exp(jnp.where(causal, b[:, :, :, None, :] - b[:, :, None, :, :], -jnp.inf))
        att = jnp.einsum('bhtk,bhtsk,bhsk->bhts', qc, dec, kc)
        o = o + jnp.einsum('bhts,bhsv->bhtv', att, vc)
        b_last = b[:, :, -1:, :]
        S = jnp.exp(b_last[:, :, 0, :, None]) * S + jnp.einsum('bhsk,bhsv->bhkv', kc * jnp.exp(b_last - b), vc)
        return S, o

    S, o = lax.scan(step, _f32(S0), tuple(_to_chunks(t, c) for t in (q, k, v, lg)))
    o = head_rms_norm(_from_chunks(o), norm_g).reshape(B, L, W_GROUP) * jax.nn.silu(_f32(g))
    return o, S


def pool_mixer(z, buf, w_pool, scale, start_pos):
    B, L, _ = z.shape
    zf = _f32(z)
    ext = jnp.concatenate([_f32(buf), zf], axis=1)
    cs = jnp.concatenate([jnp.zeros((B, 1, W_GROUP), jnp.float32), jnp.cumsum(ext, axis=1)], axis=1)
    pos = start_pos + jnp.arange(L)
    outs = []
    for gi, w in enumerate(POOL_WINDOWS):
        ch = slice(gi * POOL_GROUP, (gi + 1) * POOL_GROUP)
        win = cs[:, POOL_BUF + 1:POOL_BUF + 1 + L, ch] - cs[:, POOL_BUF + 1 - w:POOL_BUF + 1 - w + L, ch]
        cnt = jnp.minimum(pos + 1, w).astype(jnp.float32)[None, :, None]
        outs.append((win / cnt - zf[:, :, ch]) @ w_pool[gi])
    return jnp.concatenate(outs, axis=-1) * scale, ext[:, L:]


def rwkv_mixer(feat, shift_buf, S0, mu, w0, w2, a0, a2, g2, k_k, k_a, r_k, ln_g, ln_b):
    B, L, _ = feat.shape
    ff = _f32(feat)
    ext = jnp.concatenate([_f32(shift_buf), ff], axis=1)
    mixed = ff + (ext[:, :L] - ff) * mu
    r, k, v, wd, ad, gd = _split(mixed, RWKV_SIZES)
    logw = -jnp.exp(-jax.nn.softplus(-(w0 + jnp.tanh(wd) @ w2)) - 0.5)
    a = jax.nn.sigmoid(a0 + ad @ a2)
    g = jax.nn.sigmoid(gd) @ g2
    hs = lambda t: t.reshape(B, L, H_C, N_C)
    r, k, v, a, decay = hs(r), hs(k), hs(v), hs(a), jnp.exp(hs(logw))
    kk = k * k_k.reshape(H_C, N_C)
    kk = kk * lax.rsqrt(jnp.maximum(jnp.sum(kk * kk, -1, keepdims=True), 1e-24))
    k = k * (1.0 + (a - 1.0) * k_a.reshape(H_C, N_C))

    def step(S, inp):
        r_t, k_t, v_t, w_t, kk_t, a_t = inp
        sa = jnp.einsum('bhvk,bhk->bhv', S, -kk_t)
        S = S * w_t[:, :, None, :] + sa[..., None] * (kk_t * a_t)[:, :, None, :] + v_t[..., None] * k_t[:, :, None, :]
        return S, jnp.einsum('bhvk,bhk->bhv', S, r_t)

    tm = lambda t: jnp.moveaxis(t, 1, 0)
    S, o = lax.scan(step, _f32(S0), (tm(r), tm(k), tm(v), tm(decay), tm(kk), tm(a)))
    o = head_layer_norm(jnp.moveaxis(o, 0, 1), ln_g, ln_b, RWKV_LN_EPS)
    bonus = jnp.sum(r * k * r_k, -1, keepdims=True) * v
    return (o + bonus).reshape(B, L, W_GROUP) * g, ext[:, L:], S


def mlstm_mixer(qk, v, o_gate, i_pre, f_pre, conv_buf, C0, n0, m0, conv_w, conv_b, b_i, b_f, norm_g):
    B, L, _ = v.shape
    qk, new_buf = causal_dwconv(_f32(qk), conv_buf, conv_w, conv_b)
    q, k = _split(jax.nn.silu(qk), (W_GROUP, W_GROUP))
    hs = lambda t: t.reshape(B, L, H_D, D_D)
    q, k, v = hs(q) * D_D ** -0.5, hs(k), hs(_f32(v))
    ig = _f32(i_pre) + b_i
    lf = jax.nn.log_sigmoid(_f32(f_pre) + b_f)
    c = math.gcd(L, CHUNK)
    causal = jnp.tril(jnp.ones((c, c), dtype=bool))

    def step(carry, inp):
        Cm, nv, mv = carry
        qc, kc, vc, ic, fc = inp
        F = jnp.cumsum(fc, axis=-1)
        dlog = jnp.where(causal, F[..., :, None] - F[..., None, :] + ic[..., None, :], -jnp.inf)
        inter = F + mv[..., None]
        m_t = jnp.maximum(inter, jnp.max(dlog, -1))
        wgt = jnp.exp(dlog - m_t[..., None])
        w0 = jnp.exp(inter - m_t)
        s = wgt * jnp.einsum('bhtd,bhsd->bhts', qc, kc)
        num = w0[..., None] * jnp.einsum('bhtd,bhde->bhte', qc, Cm) + jnp.einsum('bhts,bhse->bhte', s, vc)
        den = w0 * jnp.einsum('bhtd,bhd->bht', qc, nv) + jnp.sum(s, -1)
        h = num / jnp.maximum(jnp.abs(den), jnp.exp(-m_t))[..., None]
        m_new = m_t[..., -1]
        wk = jnp.exp(F[..., -1:] - F + ic - m_new[..., None])
        w0f = jnp.exp(F[..., -1] + mv - m_new)
        C_new = w0f[..., None, None] * Cm + jnp.einsum('bhs,bhsd,bhse->bhde', wk, kc, vc)
        n_new = w0f[..., None] * nv + jnp.einsum('bhs,bhsd->bhd', wk, kc)
        return (C_new, n_new, m_new), h

    (C, n, m), h = lax.scan(step, (_f32(C0), _f32(n0), _f32(m0)),
                            tuple(_to_chunks(t, c) for t in (q, k, v, ig, lf)))
    h = head_rms_norm(_from_chunks(h), norm_g).reshape(B, L, W_GROUP)
    return jax.nn.sigmoid(_f32(o_gate)) * h, new_buf, C, n, m


def conv_ffn(x, buf, w_up, conv_w, conv_b, w_down):
    u, new_buf = causal_dwconv(x @ w_up, buf, conv_w, conv_b)
    gate, up = jnp.split(u, 2, axis=-1)
    return (jax.nn.silu(gate) * up) @ w_down, new_buf


def decoder_layer(x, states, p, start_pos):
    s_gla, s_pool, s_rwkv, s_shift, s_c, s_n, s_m, s_conv, s_ffn = states
    gla_in, pool_in, rwkv_in, mlstm_in = _split(x @ p['w_in'], GROUP_SIZES)
    q_a, k_a, v_a, g_a, lr_a = _split(gla_in, GLA_SIZES)
    y_a, s_gla = gla_mixer(q_a, k_a, v_a, g_a, lr_a, s_gla, p['gla_w_a2'], p['gla_b_a'], p['gla_norm_g'])
    y_b, s_pool = pool_mixer(pool_in, s_pool, p['pool_w'], p['pool_scale'], start_pos)
    y_c, s_shift, s_rwkv = rwkv_mixer(rwkv_in, s_shift, s_rwkv, p['rwkv_mu'], p['rwkv_w0'], p['rwkv_w2'],
                                      p['rwkv_a0'], p['rwkv_a2'], p['rwkv_g2'], p['rwkv_k_k'], p['rwkv_k_a'],
                                      p['rwkv_r_k'], p['rwkv_ln_g'], p['rwkv_ln_b'])
    qk_d, v_d, o_d, i_d, f_d = _split(mlstm_in, MLSTM_SIZES)
    y_d, s_conv, s_c, s_n, s_m = mlstm_mixer(qk_d, v_d, o_d, i_d, f_d, s_conv, s_c, s_n, s_m,
                                             p['mlstm_conv_w'], p['mlstm_conv_b'], p['mlstm_b_i'],
                                             p['mlstm_b_f'], p['mlstm_norm_g'])
    mixed = jnp.concatenate([y_a, y_b, y_c, y_d], axis=-1).astype(x.dtype) @ p['w_out']
    x = layer_norm(ALPHA * x + mixed, p['ln1_g'], p['ln1_b'])
    f, s_ffn = conv_ffn(x, s_ffn, p['ffn_w_up'], p['ffn_conv_w'], p['ffn_conv_b'], p['ffn_w_down'])
    x = layer_norm(ALPHA * x + f, p['ln2_g'], p['ln2_b'])
    new_states = tuple(s.astype(x.dtype) for s in (s_gla, s_pool, s_rwkv, s_shift, s_c, s_n, s_m, s_conv, s_ffn))
    return x, new_states


def init_prompt_states(batch, dtype):
    z = lambda *s: jnp.zeros((batch,) + s, dtype)
    return (z(H_A, DK_A, DV_A), z(POOL_BUF, W_GROUP), z(H_C, N_C, N_C), z(1, RWKV_COLS),
            z(H_D, D_D, D_D), z(H_D, D_D), z(H_D), z(CONV_D - 1, 2 * W_GROUP), z(FFN_CONV - 1, 2 * D_FF))


def setup_inputs(seed: int = 0) -> dict:
    key = jax.random.key(seed)
    ks = iter(jax.random.split(key, 48))

    def nrm(shape, scale):
        return jax.random.normal(next(ks), shape, jnp.float32) * scale

    d = {}
    d['x_prompt'] = nrm((BATCH, SEQ, D_MODEL), 1.0)
    d['x_sample'] = nrm((DEC_BATCH, DEC_SEQ, D_MODEL), 1.0)
    d['state_gla'] = nrm((DEPTH, DEC_BATCH, H_A, DK_A, DV_A), 0.1)
    d['state_pool'] = nrm((DEPTH, DEC_BATCH, POOL_BUF, W_GROUP), 1.0)
    d['state_rwkv'] = nrm((DEPTH, DEC_BATCH, H_C, N_C, N_C), 0.1)
    d['state_rwkv_shift'] = nrm((DEPTH, DEC_BATCH, 1, RWKV_COLS), 1.0)
    d['state_mlstm_c'] = nrm((DEPTH, DEC_BATCH, H_D, D_D, D_D), 0.1)
    d['state_mlstm_n'] = nrm((DEPTH, DEC_BATCH, H_D, D_D), 0.1)
    d['state_mlstm_m'] = nrm((DEPTH, DEC_BATCH, H_D), 1.0)
    d['state_mlstm_conv'] = nrm((DEPTH, DEC_BATCH, CONV_D - 1, 2 * W_GROUP), 1.0)
    d['state_ffn_conv'] = nrm((DEPTH, DEC_BATCH, FFN_CONV - 1, 2 * D_FF), 1.0)
    d['w_in'] = nrm((DEPTH, D_MODEL, IN_COLS), D_MODEL ** -0.5)
    d['gla_w_a2'] = nrm((DEPTH, R_GLA, H_A * DK_A), R_GLA ** -0.5)
    d['gla_b_a'] = nrm((DEPTH, H_A * DK_A), 0.1)
    d['gla_norm_g'] = 1.0 + nrm((DEPTH, W_GROUP), 0.02)
    d['pool_w'] = nrm((DEPTH, len(POOL_WINDOWS), POOL_GROUP, POOL_GROUP), POOL_GROUP ** -0.5)
    d['pool_scale'] = 1.0 + nrm((DEPTH, W_GROUP), 0.02)
    d['rwkv_mu'] = jax.random.uniform(next(ks), (DEPTH, RWKV_COLS), jnp.float32)
    d['rwkv_w0'] = nrm((DEPTH, W_GROUP), 0.5)
    d['rwkv_w2'] = nrm((DEPTH, R_W, W_GROUP), 0.5 * R_W ** -0.5)
    d['rwkv_a0'] = nrm((DEPTH, W_GROUP), 0.1)
    d['rwkv_a2'] = nrm((DEPTH, R_AA, W_GROUP), 0.5 * R_AA ** -0.5)
    d['rwkv_g2'] = nrm((DEPTH, R_G, W_GROUP), R_G ** -0.5)
    d['rwkv_k_k'] = 0.85 + nrm((DEPTH, W_GROUP), 0.02)
    d['rwkv_k_a'] = 1.0 + nrm((DEPTH, W_GROUP), 0.02)
    d['rwkv_r_k'] = nrm((DEPTH, H_C, N_C), 0.1)
    d['rwkv_ln_g'] = 1.0 + nrm((DEPTH, W_GROUP), 0.02)
    d['rwkv_ln_b'] = nrm((DEPTH, W_GROUP), 0.01)
    d['mlstm_conv_w'] = nrm((DEPTH, CONV_D, 2 * W_GROUP), CONV_D ** -0.5)
    d['mlstm_conv_b'] = nrm((DEPTH, 2 * W_GROUP), 0.01)
    d['mlstm_b_i'] = nrm((DEPTH, H_D), 0.1)
    d['mlstm_b_f'] = jnp.linspace(3.0, 6.0, H_D, dtype=jnp.float32)[None, :] + nrm((DEPTH, H_D), 0.1)
    d['mlstm_norm_g'] = 1.0 + nrm((DEPTH, W_GROUP), 0.02)
    d['w_out'] = nrm((DEPTH, MIX, D_MODEL), BETA * MIX ** -0.5)
    d['ln1_g'] = 1.0 + nrm((DEPTH, D_MODEL), 0.02)
    d['ln1_b'] = nrm((DEPTH, D_MODEL), 0.01)
    d['ffn_w_up'] = nrm((DEPTH, D_MODEL, 2 * D_FF), D_MODEL ** -0.5)
    d['ffn_conv_w'] = nrm((DEPTH, FFN_CONV, 2 * D_FF), FFN_CONV ** -0.5)
    d['ffn_conv_b'] = nrm((DEPTH, 2 * D_FF), 0.01)
    d['ffn_w_down'] = nrm((DEPTH, D_FF, D_MODEL), BETA * D_FF ** -0.5)
    d['ln2_g'] = 1.0 + nrm((DEPTH, D_MODEL), 0.02)
    d['ln2_b'] = nrm((DEPTH, D_MODEL), 0.01)
    return d


def reference(x_prompt, x_sample, state_gla, state_pool, state_rwkv, state_rwkv_shift, state_mlstm_c,
              state_mlstm_n, state_mlstm_m, state_mlstm_conv, state_ffn_conv, w_in, gla_w_a2, gla_b_a,
              gla_norm_g, pool_w, pool_scale, rwkv_mu, rwkv_w0, rwkv_w2, rwkv_a0, rwkv_a2, rwkv_g2, rwkv_k_k,
              rwkv_k_a, rwkv_r_k, rwkv_ln_g, rwkv_ln_b, mlstm_conv_w, mlstm_conv_b, mlstm_b_i, mlstm_b_f,
              mlstm_norm_g, w_out, ln1_g, ln1_b, ffn_w_up, ffn_conv_w, ffn_conv_b, ffn_w_down, ln2_g, ln2_b):
    sample_states = (state_gla, state_pool, state_rwkv, state_rwkv_shift, state_mlstm_c, state_mlstm_n,
                     state_mlstm_m, state_mlstm_conv, state_ffn_conv)
    n_st = len(sample_states)
    yp, ys = x_prompt, x_sample
    acc_p = [[] for _ in range(n_st)]
    acc_s = [[] for _ in range(n_st)]
    for l in range(DEPTH):
        p = dict(w_in=w_in[l], gla_w_a2=gla_w_a2[l], gla_b_a=gla_b_a[l], gla_norm_g=gla_norm_g[l],
                 pool_w=pool_w[l], pool_scale=pool_scale[l], rwkv_mu=rwkv_mu[l], rwkv_w0=rwkv_w0[l],
                 rwkv_w2=rwkv_w2[l], rwkv_a0=rwkv_a0[l], rwkv_a2=rwkv_a2[l], rwkv_g2=rwkv_g2[l],
                 rwkv_k_k=rwkv_k_k[l], rwkv_k_a=rwkv_k_a[l], rwkv_r_k=rwkv_r_k[l], rwkv_ln_g=rwkv_ln_g[l],
                 rwkv_ln_b=rwkv_ln_b[l], mlstm_conv_w=mlstm_conv_w[l], mlstm_conv_b=mlstm_conv_b[l],
                 mlstm_b_i=mlstm_b_i[l], mlstm_b_f=mlstm_b_f[l], mlstm_norm_g=mlstm_norm_g[l],
                 w_out=w_out[l], ln1_g=ln1_g[l], ln1_b=ln1_b[l], ffn_w_up=ffn_w_up[l],
                 ffn_conv_w=ffn_conv_w[l], ffn_conv_b=ffn_conv_b[l], ffn_w_down=ffn_w_down[l],
                 ln2_g=ln2_g[l], ln2_b=ln2_b[l])
        yp, st_p = decoder_layer(yp, init_prompt_states(yp.shape[0], yp.dtype), p, 0)
        ys, st_s = decoder_layer(ys, tuple(s[l] for s in sample_states), p, PAST_LEN)
        for i in range(n_st):
            acc_p[i].append(st_p[i])
            acc_s[i].append(st_s[i])
    gla_p, pool_p, rwkv_p, shift_p, mc_p, mn_p, mm_p, mconv_p, ffn_p = [jnp.stack(t) for t in acc_p]
    gla_s, pool_s, rwkv_s, shift_s, mc_s, mn_s, mm_s, mconv_s, ffn_s = [jnp.stack(t) for t in acc_s]
    return (yp, ys, gla_p, gla_s, pool_p, pool_s, rwkv_p, rwkv_s, shift_p, shift_s, mc_p, mc_s,
            mn_p, mn_s, mm_p, mm_s, mconv_p, mconv_s, ffn_p, ffn_s)
```

```python
import functools
import math

import jax
import jax.numpy as jnp
from jax import lax
from jax.experimental import pallas as pl
from jax.experimental.pallas import tpu as pltpu

F32 = jnp.float32
BF16 = jnp.bfloat16
HI = lax.Precision.HIGHEST

D_MODEL = 1024
DEPTH = 4
PAST_LEN = 16384
W_GROUP = 256
N_HEADS = 4
DK_A = 32
DV_A = 64
R_GLA = 16
GLA_TAU = 16.0
POOL_WINDOWS = (2, 4, 8, 16)
POOL_BUF = 15
N_C = 64
R_W, R_AA, R_G = 16, 16, 32
RWKV_COLS = 832
RWKV_LN_EPS = 64e-5
D_D = 64
CONV_D = 4
D_FF = 2816
FFN_CONV = 3
ALPHA = (2 * DEPTH) ** 0.25
CHUNK = 64

GLA_W = 896
RWKV_W = 896
GLA_OFF, RWKV_OFF, POOL_OFF, ML_OFF, MLIF_OFF = 0, 896, 1792, 2048, 3072
ZC = 3200
ML_W = 1024
MLIF_W = 128

VMEM_LIMIT = 56 * 1024 * 1024


def _cp(*sem):
    return pltpu.CompilerParams(dimension_semantics=sem, vmem_limit_bytes=VMEM_LIMIT)


def _mm(a, b):
    return jnp.dot(a.astype(BF16), b.astype(BF16), preferred_element_type=F32)


def _mm_nt(a, b):
    return lax.dot_general(a.astype(BF16), b.astype(BF16), (((1,), (1,)), ((), ())), preferred_element_type=F32)


def _mm_tn(a, b):
    return lax.dot_general(a.astype(BF16), b.astype(BF16), (((0,), (0,)), ((), ())), preferred_element_type=F32)


def _mm_hi(a, b):
    return jnp.dot(a, b, preferred_element_type=F32, precision=HI)


def _iota(shape, dim):
    return lax.broadcasted_iota(jnp.int32, shape, dim)


def _sigmoid(x):
    return 1.0 / (1.0 + jnp.exp(-x))


def _silu(x):
    return x * _sigmoid(x)


def _log_sigmoid(x):
    return jnp.minimum(x, 0.0) - jnp.log(1.0 + jnp.exp(-jnp.abs(x)))


def _tile4(x):
    return jnp.concatenate([x, x, x, x], axis=0)


def _block_diag_mask(rows, cols, rshift, cshift):
    return (_iota((rows, cols), 0) >> rshift) == (_iota((rows, cols), 1) >> cshift)


def _layer_norm(h, g, b, eps):
    mu = jnp.mean(h, axis=-1, keepdims=True)
    xc = h - mu
    var = jnp.mean(xc * xc, axis=-1, keepdims=True)
    return xc * lax.rsqrt(var + eps) * g + b


def _linear_body(x_ref, w_ref, o_ref):
    o_ref[...] = jnp.dot(x_ref[...].astype(BF16), w_ref[...], preferred_element_type=F32)


def _linear(x, w, tm):
    t, k = x.shape
    n = w.shape[1]
    return pl.pallas_call(
        _linear_body,
        grid=(t // tm,),
        in_specs=[pl.BlockSpec((tm, k), lambda i: (i, 0)), pl.BlockSpec((k, n), lambda i: (0, 0))],
        out_specs=pl.BlockSpec((tm, n), lambda i: (i, 0)),
        out_shape=jax.ShapeDtypeStruct((t, n), F32),
        compiler_params=_cp("arbitrary"),
        name="in_proj",
    )(x, w)


def _outproj_body(ya_ref, yb_ref, yc_ref, yd_ref, x_ref, w_ref, g_ref, b_ref, o_ref):
    y = jnp.concatenate([ya_ref[...], yb_ref[...], yc_ref[...], yd_ref[...]], axis=1).astype(BF16)
    mixed = jnp.dot(y, w_ref[...], preferred_element_type=F32)
    o_ref[...] = _layer_norm(ALPHA * x_ref[...] + mixed, g_ref[...], b_ref[...], 1e-5)


def _outproj_ln(ys, x, w, g, b, tm):
    t, d = x.shape
    yspec = pl.BlockSpec((tm, W_GROUP), lambda i: (i, 0))
    vec = pl.BlockSpec((1, d), lambda i: (0, 0))
    return pl.pallas_call(
        _outproj_body,
        grid=(t // tm,),
        in_specs=[yspec, yspec, yspec, yspec, pl.BlockSpec((tm, d), lambda i: (i, 0)),
                  pl.BlockSpec((d, d), lambda i: (0, 0)), vec, vec],
        out_specs=pl.BlockSpec((tm, d), lambda i: (i, 0)),
        out_shape=jax.ShapeDtypeStruct((t, d), F32),
        compiler_params=_cp("arbitrary"),
        name="out_proj_ln",
    )(*ys, x, w, g, b)


FF_CHUNK = 256
N_FF_CHUNKS = D_FF // FF_CHUNK


def _ffn_tail(x, h_ref, wdn_ref, g_ref, b_ref):
    f = jnp.dot(h_ref[...], wdn_ref[...], preferred_element_type=F32)
    return _layer_norm(ALPHA * x + f, g_ref[...], b_ref[...], 1e-5)


def _ffn_prompt_body(x_ref, buf_ref, wup_ref, cw_ref, cb_ref, wdn_ref, g_ref, b_ref,
                     o_ref, st_ref, carry_ref, h_ref, *, tm, n_tiles):
    j = pl.program_id(1)

    @pl.when(j == 0)
    def _():
        carry_ref[...] = buf_ref[0]

    x = x_ref[0]
    xb = x.astype(BF16)
    for c in range(N_FF_CHUNKS):
        acts = []
        for half in range(2):
            sl = slice(half * D_FF + c * FF_CHUNK, half * D_FF + (c + 1) * FF_CHUNK)
            u = jnp.dot(xb, wup_ref[:, sl], preferred_element_type=F32)
            ext = jnp.concatenate([carry_ref[:, sl], u], axis=0)
            conv = (cb_ref[:, sl] + cw_ref[0:1, sl] * ext[0:tm] + cw_ref[1:2, sl] * ext[1:tm + 1]
                    + cw_ref[2:3, sl] * ext[2:tm + 2])
            carry_ref[:, sl] = ext[tm:tm + 2]
            acts.append(conv)
        h_ref[:, c * FF_CHUNK:(c + 1) * FF_CHUNK] = (_silu(acts[0]) * acts[1]).astype(BF16)
    o_ref[0] = _ffn_tail(x, h_ref, wdn_ref, g_ref, b_ref)

    @pl.when(j == n_tiles - 1)
    def _():
        st_ref[0] = carry_ref[...]


def _ffn_prompt(x3, buf, wup, cw, cb, wdn, g, b, tm):
    bsz, seq, d = x3.shape
    n_tiles = seq // tm
    const = lambda shape: pl.BlockSpec(shape, lambda i, j: (0,) * len(shape))
    return pl.pallas_call(
        functools.partial(_ffn_prompt_body, tm=tm, n_tiles=n_tiles),
        grid=(bsz, n_tiles),
        in_specs=[pl.BlockSpec((1, tm, d), lambda i, j: (i, j, 0)),
                  pl.BlockSpec((1, 2, 2 * D_FF), lambda i, j: (i, 0, 0)),
                  const((d, 2 * D_FF)), const((FFN_CONV, 2 * D_FF)), const((1, 2 * D_FF)),
                  const((D_FF, d)), const((1, d)), const((1, d))],
        out_specs=[pl.BlockSpec((1, tm, d), lambda i, j: (i, j, 0)),
                   pl.BlockSpec((1, 2, 2 * D_FF), lambda i, j: (i, 0, 0))],
        out_shape=[jax.ShapeDtypeStruct((bsz, seq, d), F32), jax.ShapeDtypeStruct((bsz, 2, 2 * D_FF), F32)],
        scratch_shapes=[pltpu.VMEM((2, 2 * D_FF), F32), pltpu.VMEM((tm, D_FF), BF16)],
        compiler_params=_cp("arbitrary", "arbitrary"),
        name="ffn_prompt",
    )(x3, buf, wup, cw, cb, wdn, g, b)


def _ffn_sample_body(x_ref, buf_ref, wup_ref, cw_ref, cb_ref, wdn_ref, g_ref, b_ref,
                     o_ref, st_ref, h_ref, *, nb, steps):
    t = nb * steps
    x = x_ref[...]
    xb = x.astype(BF16)
    for c in range(N_FF_CHUNKS):
        acts = []
        for half in range(2):
            lo = half * D_FF + c * FF_CHUNK
            sl = slice(lo, lo + FF_CHUNK)
            u = jnp.dot(xb, wup_ref[:, sl], preferred_element_type=F32)
            b0 = buf_ref[:, lo:lo + FF_CHUNK]
            b1 = buf_ref[:, 2 * D_FF + lo:2 * D_FF + lo + FF_CHUNK]
            back1 = jnp.concatenate([b1, u[0:t - nb]], axis=0)
            back2 = jnp.concatenate([b0, b1, u[0:t - 2 * nb]], axis=0)
            conv = cb_ref[:, sl] + cw_ref[0:1, sl] * back2 + cw_ref[1:2, sl] * back1 + cw_ref[2:3, sl] * u
            st_ref[:, lo:lo + FF_CHUNK] = u[t - 2 * nb:t - nb]
            st_ref[:, 2 * D_FF + lo:2 * D_FF + lo + FF_CHUNK] = u[t - nb:t]
            acts.append(conv)
        h_ref[:, c * FF_CHUNK:(c + 1) * FF_CHUNK] = (_silu(acts[0]) * acts[1]).astype(BF16)
    o_ref[...] = _ffn_tail(x, h_ref, wdn_ref, g_ref, b_ref)


def _ffn_sample(x, buf, wup, cw, cb, wdn, g, b, nb, steps):
    t, d = x.shape
    return pl.pallas_call(
        functools.partial(_ffn_sample_body, nb=nb, steps=steps),
        out_shape=[jax.ShapeDtypeStruct((t, d), F32), jax.ShapeDtypeStruct((nb, 4 * D_FF), F32)],
        scratch_shapes=[pltpu.VMEM((t, D_FF), BF16)],
        compiler_params=pltpu.CompilerParams(vmem_limit_bytes=VMEM_LIMIT),
        name="ffn_sample",
    )(x, buf, wup, cw, cb, wdn, g, b)


def _tril64():
    return (_iota((CHUNK, CHUNK), 0) >= _iota((CHUNK, CHUNK), 1)).astype(F32)


def _exp_masks():
    t = _iota((CHUNK, 256), 0)
    s = _iota((CHUNK, 256), 1) & (CHUNK - 1)
    return t >= s, t > s, t == s


def _bd256():
    return _block_diag_mask(256, 256, 6, 6)


def _put_block_diag(dst_ref, blocks, rows, cols):
    dst_ref[...] = jnp.zeros(dst_ref.shape, F32)
    for h in range(N_HEADS):
        dst_ref[h * rows:(h + 1) * rows, h * cols:(h + 1) * cols] = blocks[h]


def _get_block_diag(src_ref, out_ref, rows, cols):
    for h in range(N_HEADS):
        out_ref[0, h] = src_ref[h * rows:(h + 1) * rows, h * cols:(h + 1) * cols]


def _gla_prompt_body(z_ref, s0_ref, wa_ref, ba_ref, ng_ref, y_ref, so_ref, sbd_ref, *, n_chunks, n_blocks):
    j = pl.program_id(1)

    @pl.when(j == 0)
    def _():
        _put_block_diag(sbd_ref, s0_ref[0], DK_A, DV_A)

    tril = _tril64()
    causal, _, _ = _exp_masks()
    bd_state = _block_diag_mask(128, 256, 5, 6)
    bd_k = _block_diag_mask(256, 128, 6, 5)
    bd_v = _bd256()
    pavg = bd_v.astype(F32) * (1.0 / DV_A)

    def chunk(c, carry):
        r0 = pl.multiple_of(c * CHUNK, CHUNK)
        rows = z_ref[0, pl.ds(r0, CHUNK), :]
        q = rows[:, 0:128] * DK_A ** -0.5
        k = rows[:, 128:256]
        v = rows[:, 256:512]
        g = rows[:, 512:768]
        lr = rows[:, 768:896]
        lg = _log_sigmoid(_mm_hi(lr, wa_ref[...]) + ba_ref[...]) * (1.0 / GLA_TAU)
        bc = _mm_hi(tril, lg)
        bl = bc[CHUNK - 1:CHUNK]
        rho = bc[CHUNK // 2 - 1:CHUNK // 2]
        s = sbd_ref[...]
        o = _mm(q * jnp.exp(bc), s)
        qt = q * jnp.exp(bc - rho)
        kt = k * jnp.exp(rho - bc)
        att = _mm_nt(qt, jnp.where(bd_k, _tile4(kt), 0.0))
        att = jnp.where(causal, att, 0.0)
        o = o + _mm(att, jnp.where(bd_v, _tile4(v), 0.0))
        ms = _mm_hi(o * o, pavg)
        y_ref[0, pl.ds(r0, CHUNK), :] = o * lax.rsqrt(ms + 1e-6) * ng_ref[...] * _silu(g)
        kd = k * jnp.exp(bl - bc)
        decay_col = jnp.broadcast_to(jnp.exp(bl), (128, 128)).T
        sbd_ref[...] = jnp.where(bd_state, jnp.concatenate([decay_col, decay_col], axis=1) * s + _mm_tn(kd, v), 0.0)
        return carry

    lax.fori_loop(0, n_chunks, chunk, 0)

    @pl.when(j == n_blocks - 1)
    def _():
        _get_block_diag(sbd_ref, so_ref, DK_A, DV_A)


def _gla_prompt(z3, s0, wa, ba, ng, lb):
    bsz, seq, _ = z3.shape
    n_blocks = seq // lb
    const = lambda shape: pl.BlockSpec(shape, lambda i, j: (0,) * len(shape))
    st = pl.BlockSpec((1, N_HEADS, DK_A, DV_A), lambda i, j: (i, 0, 0, 0))
    return pl.pallas_call(
        functools.partial(_gla_prompt_body, n_chunks=lb // CHUNK, n_blocks=n_blocks),
        grid=(bsz, n_blocks),
        in_specs=[pl.BlockSpec((1, lb, GLA_W), lambda i, j: (i, j, GLA_OFF // GLA_W)), st,
                  const((128, 128)), const((1, 128)), const((1, 256))],
        out_specs=[pl.BlockSpec((1, lb, W_GROUP), lambda i, j: (i, j, 0)), st],
        out_shape=[jax.ShapeDtypeStruct((bsz, seq, W_GROUP), F32),
                   jax.ShapeDtypeStruct((bsz, N_HEADS, DK_A, DV_A), F32)],
        scratch_shapes=[pltpu.VMEM((128, 256), F32)],
        compiler_params=_cp("arbitrary", "arbitrary"),
        name="gla_prompt",
    )(z3, s0, wa, ba, ng)


def _pool_prompt_body(z_ref, buf_ref, w_ref, sc_ref, y_ref, bo_ref, hist_ref, *, lb, n_blocks, start_pos):
    j = pl.program_id(1)

    @pl.when(j == 0)
    def _():
        hist_ref[0:1, :] = jnp.zeros((1, W_GROUP), F32)
        hist_ref[1:16, :] = buf_ref[0]

    zp = z_ref[0]
    e = jnp.concatenate([hist_ref[...], zp], axis=0)
    s2 = e[1:] + e[:-1]
    s4 = s2[2:] + s2[:-2]
    s8 = s4[4:] + s4[:-4]
    s16 = s8[8:] + s8[:-8]
    lane = _iota((lb, W_GROUP), 1)
    win = jnp.where(lane < 64, s2[15:], jnp.where(lane < 128, s4[13:], jnp.where(lane < 192, s8[9:], s16[1:])))
    width = jnp.where(lane < 64, 2, jnp.where(lane < 128, 4, jnp.where(lane < 192, 8, 16)))
    pos = start_pos + j * lb + _iota((lb, W_GROUP), 0)
    cnt = jnp.minimum(pos + 1, width).astype(F32)
    y_ref[0] = _mm(win / cnt - zp, w_ref[...]) * sc_ref[...]
    hist_ref[1:16, :] = e[lb + 1:lb + 16]

    @pl.when(j == n_blocks - 1)
    def _():
        bo_ref[0] = hist_ref[1:16, :]


def _pool_prompt(z3, buf, wbd, scale, lb, start_pos):
    bsz, seq, _ = z3.shape
    n_blocks = seq // lb
    const = lambda shape: pl.BlockSpec(shape, lambda i, j: (0,) * len(shape))
    st = pl.BlockSpec((1, POOL_BUF, W_GROUP), lambda i, j: (i, 0, 0))
    return pl.pallas_call(
        functools.partial(_pool_prompt_body, lb=lb, n_blocks=n_blocks, start_pos=start_pos),
        grid=(bsz, n_blocks),
        in_specs=[pl.BlockSpec((1, lb, W_GROUP), lambda i, j: (i, j, POOL_OFF // W_GROUP)), st,
                  const((256, 256)), const((1, 256))],
        out_specs=[pl.BlockSpec((1, lb, W_GROUP), lambda i, j: (i, j, 0)), st],
        out_shape=[jax.ShapeDtypeStruct((bsz, seq, W_GROUP), F32),
                   jax.ShapeDtypeStruct((bsz, POOL_BUF, W_GROUP), F32)],
        scratch_shapes=[pltpu.VMEM((16, W_GROUP), F32)],
        compiler_params=_cp("arbitrary", "arbitrary"),
        name="pool_prompt",
    )(z3, buf, wbd, scale)


def _rwkv_prompt_body(z_ref, sh0_ref, s0_ref, mu_ref, w0_ref, w2_ref, a0_ref, a2_ref, g2_ref, kk_ref, ka_ref,
                      rk_ref, lg_ref, lb_ref, y_ref, so_ref, sho_ref,
                      nbd_ref, carry_ref, r_s, k_s, v_s, kk_s, b_s, lw_s, g_s, *, n_chunks, n_blocks, lb):
    j = pl.program_id(1)

    @pl.when(j == 0)
    def _():
        _put_block_diag(nbd_ref, s0_ref[0], N_C, N_C)
        carry_ref[...] = sh0_ref[0]

    bd = _bd256()
    ones_bd = bd.astype(F32)
    pavg = ones_bd * (1.0 / N_C)

    f = z_ref[0]
    prev = jnp.concatenate([carry_ref[...], f[:-1]], axis=0)
    mixed = f + (prev - f) * mu_ref[...]
    carry_ref[...] = f[lb - 1:lb]
    r = mixed[:, 0:256]
    k = mixed[:, 256:512]
    v = mixed[:, 512:768]
    low = mixed[:, 768:896]
    lw_s[...] = -_sigmoid(w0_ref[...] + _mm_hi(jnp.tanh(low), w2_ref[...])) * math.exp(-0.5)
    a = _sigmoid(a0_ref[...] + _mm_hi(low, a2_ref[...]))
    g_s[...] = _mm_hi(_sigmoid(low), g2_ref[...])
    kk = k * kk_ref[...]
    kk = kk * lax.rsqrt(jnp.maximum(_mm_hi(kk * kk, ones_bd), 1e-24))
    r_s[...] = r
    v_s[...] = v
    kk_s[...] = kk
    k_s[...] = k * (1.0 + (a - 1.0) * ka_ref[...])
    b_s[...] = kk * a

    tril = _tril64()
    lower, strict, _ = _exp_masks()
    eye = (_iota((256, 256), 0) == _iota((256, 256), 1)).astype(F32)

    def chunk(c, carry):
        r0 = pl.multiple_of(c * CHUNK, CHUNK)
        rows = pl.ds(r0, CHUNK)
        lw = lw_s[rows, :]
        rr, kp, vv, kq, be = r_s[rows, :], k_s[rows, :], v_s[rows, :], kk_s[rows, :], b_s[rows, :]
        gc = _mm_hi(tril, lw)
        gl = gc[CHUNK - 1:CHUNK]
        einv = jnp.exp(-gc)
        lhs = jnp.concatenate([kq * jnp.exp(gc - lw), rr * jnp.exp(gc)], axis=0)
        rhs = jnp.concatenate([jnp.where(bd, _tile4(kp * einv), 0.0),
                               jnp.where(bd, _tile4(be * einv), 0.0)], axis=0)
        aa = _mm_nt(lhs, rhs)
        a_kk = jnp.where(strict, aa[0:CHUNK, 0:256], 0.0)
        a_kb = jnp.where(strict, aa[0:CHUNK, 256:512], 0.0)
        a_rk = jnp.where(lower, aa[CHUNK:, 0:256], 0.0)
        a_rb = jnp.where(lower, aa[CHUNK:, 256:512], 0.0)
        n = nbd_ref[...]
        sn = _mm_nt(lhs, n)
        vbd = jnp.where(bd, _tile4(vv), 0.0)
        rhs_u = sn[0:CHUNK] + _mm(a_kk, vbd)
        lmat = jnp.where(bd, _tile4(a_kb), 0.0)
        tinv = eye - lmat
        p = lmat
        for _ in range(5):
            p = _mm(p, p)
            tinv = _mm(tinv, eye + p)
        ubd = jnp.where(bd, _mm(tinv, _tile4(rhs_u)), 0.0)
        u = ubd[0:64] + ubd[64:128] + ubd[128:192] + ubd[192:256]
        o = sn[CHUNK:] + _mm(jnp.concatenate([a_rk, -a_rb], axis=1), jnp.concatenate([vbd, ubd], axis=0))
        dk = jnp.exp(gl - gc)
        upd = _mm_tn(jnp.concatenate([vv, -u], axis=0), jnp.concatenate([kp * dk, be * dk], axis=0))
        nbd_ref[...] = jnp.where(bd, n * jnp.exp(gl) + upd, 0.0)
        mu = _mm_hi(o, pavg)
        xc = o - mu
        var = _mm_hi(xc * xc, pavg)
        on = xc * lax.rsqrt(var + RWKV_LN_EPS) * lg_ref[...] + lb_ref[...]
        bonus = _mm_hi(rr * kp * rk_ref[...], ones_bd) * vv
        y_ref[0, rows, :] = (on + bonus) * g_s[rows, :]
        return carry

    lax.fori_loop(0, n_chunks, chunk, 0)

    @pl.when(j == n_blocks - 1)
    def _():
        _get_block_diag(nbd_ref, so_ref, N_C, N_C)
        sho_ref[0] = carry_ref[...]


def _rwkv_prompt(z3, sh0, s0, p, lb):
    bsz, seq, _ = z3.shape
    n_blocks = seq // lb
    const = lambda shape: pl.BlockSpec(shape, lambda i, j: (0,) * len(shape))
    st = pl.BlockSpec((1, N_HEADS, N_C, N_C), lambda i, j: (i, 0, 0, 0))
    sh = pl.BlockSpec((1, 1, RWKV_W), lambda i, j: (i, 0, 0))
    vec = const((1, 256))
    lowrank = const((128, 256))
    blk = pltpu.VMEM((lb, 256), F32)
    return pl.pallas_call(
        functools.partial(_rwkv_prompt_body, n_chunks=lb // CHUNK, n_blocks=n_blocks, lb=lb),
        grid=(bsz, n_blocks),
        in_specs=[pl.BlockSpec((1, lb, RWKV_W), lambda i, j: (i, j, RWKV_OFF // RWKV_W)), sh, st,
                  const((1, RWKV_W)), vec, lowrank, vec, lowrank, lowrank, vec, vec, vec, vec, vec],
        out_specs=[pl.BlockSpec((1, lb, W_GROUP), lambda i, j: (i, j, 0)), st, sh],
        out_shape=[jax.ShapeDtypeStruct((bsz, seq, W_GROUP), F32),
                   jax.ShapeDtypeStruct((bsz, N_HEADS, N_C, N_C), F32),
                   jax.ShapeDtypeStruct((bsz, 1, RWKV_W), F32)],
        scratch_shapes=[pltpu.VMEM((256, 256), F32), pltpu.VMEM((1, RWKV_W), F32),
                        blk, blk, blk, blk, blk, blk, blk],
        compiler_params=_cp("arbitrary", "arbitrary"),
        name="rwkv_prompt",
    )(z3, sh0, s0, p["mu"], p["w0"], p["w2"], p["a0"], p["a2"], p["g2"], p["k_k"], p["k_a"], p["r_k"],
      p["ln_g"], p["ln_b"])


def _mlstm_prompt_body(z_ref, zif_ref, cb0_ref, c0_ref, n0_ref, m0_ref, cw_ref, cbias_ref, bif_ref, ng_ref,
                       y_ref, co_ref, no_ref, mo_ref, cbo_ref,
                       cbd_ref, n_ref, m_ref, carry_ref, qk_s, *, n_chunks, n_blocks, lb):
    j = pl.program_id(1)

    @pl.when(j == 0)
    def _():
        _put_block_diag(cbd_ref, c0_ref[0], D_D, D_D)
        n_ref[...] = n0_ref[0]
        m_ref[...] = m0_ref[0]
        carry_ref[...] = cb0_ref[0]

    qk = z_ref[0, :, 0:512]
    ext = jnp.concatenate([carry_ref[...], qk], axis=0)
    conv = cbias_ref[...]
    for jj in range(CONV_D):
        conv = conv + cw_ref[jj:jj + 1, :] * ext[jj:jj + lb]
    qk_s[...] = _silu(conv)
    carry_ref[...] = ext[lb:lb + 3]

    bd = _bd256()
    ones_bd = bd.astype(F32)
    pavg = ones_bd * (1.0 / D_D)
    tril = _tril64()
    causal, _, eye_exp = _exp_masks()
    er = _iota((128, 512), 0)
    ec = _iota((128, 512), 1)
    expand = (((ec < 256) & (er == (ec >> 6))) | ((ec >= 256) & (er == 4 + ((ec - 256) >> 6)))).astype(F32)
    is_f = (_iota((CHUNK, 128), 1) >= 4) & (_iota((CHUNK, 128), 1) < 8)
    lane = _iota((CHUNK, 256), 1)

    def chunk(c, carry):
        r0 = pl.multiple_of(c * CHUNK, CHUNK)
        rows = pl.ds(r0, CHUNK)
        q = qk_s[rows, 0:256] * D_D ** -0.5
        k = qk_s[rows, 256:512]
        v = z_ref[0, rows, 512:768]
        og = z_ref[0, rows, 768:1024]
        gates = zif_ref[0, rows, :] + bif_ref[...]
        gates = jnp.where(is_f, _log_sigmoid(gates), gates)
        gates = jnp.where(is_f, _mm_hi(tril, gates), gates)
        ge = _mm_hi(gates, expand)
        iexp = ge[:, 0:256]
        fexp = ge[:, 256:512]
        d_row = jnp.sum(jnp.where(eye_exp, iexp - fexp, 0.0), axis=0, keepdims=True)
        dlog = jnp.where(causal, fexp + d_row, -jnp.inf)
        m_prev = m_ref[...]
        inter = fexp + m_prev
        mx = [jnp.max(dlog[:, h * 64:(h + 1) * 64], axis=1, keepdims=True) for h in range(N_HEADS)]
        mx = jnp.where(lane < 64, mx[0], jnp.where(lane < 128, mx[1], jnp.where(lane < 192, mx[2], mx[3])))
        m_t = jnp.maximum(inter, mx)
        wgt = jnp.exp(dlog - m_t)
        w0 = jnp.exp(inter - m_t)
        s = wgt * _mm_nt(q, jnp.where(bd, _tile4(k), 0.0))
        cbd = cbd_ref[...]
        n_row = n_ref[...]
        num = w0 * _mm(q, cbd) + _mm(s, jnp.where(bd, _tile4(v), 0.0))
        den = w0 * _mm_hi(q * n_row, ones_bd) + _mm_hi(s, ones_bd)
        hh = num / jnp.maximum(jnp.abs(den), jnp.exp(-m_t))
        ms = _mm_hi(hh * hh, pavg)
        y_ref[0, rows, :] = _sigmoid(og) * (hh * lax.rsqrt(ms + 1e-6) * ng_ref[...])
        m_new = m_t[CHUNK - 1:CHUNK]
        f_last = fexp[CHUNK - 1:CHUNK]
        kw = k * jnp.exp(f_last - fexp + iexp - m_new)
        w0f = jnp.exp(f_last + m_prev - m_new)
        cbd_ref[...] = jnp.where(bd, w0f * cbd + _mm_tn(kw, v), 0.0)
        n_ref[...] = w0f * n_row + jnp.sum(kw, axis=0, keepdims=True)
        m_ref[...] = m_new
        return carry

    lax.fori_loop(0, n_chunks, chunk, 0)

    @pl.when(j == n_blocks - 1)
    def _():
        _get_block_diag(cbd_ref, co_ref, D_D, D_D)
        no_ref[0] = n_ref[...]
        mo_ref[0] = m_ref[...]
        cbo_ref[0] = carry_ref[...]


def _mlstm_prompt(z3, cb0, c0, n0, m0, cw, cbias, bif, ng, lb):
    bsz, seq, _ = z3.shape
    n_blocks = seq // lb
    const = lambda shape: pl.BlockSpec(shape, lambda i, j: (0,) * len(shape))
    cst = pl.BlockSpec((1, N_HEADS, D_D, D_D), lambda i, j: (i, 0, 0, 0))
    row = pl.BlockSpec((1, 1, 256), lambda i, j: (i, 0, 0))
    cbs = pl.BlockSpec((1, CONV_D - 1, 512), lambda i, j: (i, 0, 0))
    return pl.pallas_call(
        functools.partial(_mlstm_prompt_body, n_chunks=lb // CHUNK, n_blocks=n_blocks, lb=lb),
        grid=(bsz, n_blocks),
        in_specs=[pl.BlockSpec((1, lb, ML_W), lambda i, j: (i, j, ML_OFF // ML_W)),
                  pl.BlockSpec((1, lb, MLIF_W), lambda i, j: (i, j, MLIF_OFF // MLIF_W)),
                  cbs, cst, row, row, const((CONV_D, 512)), const((1, 512)), const((1, 128)), const((1, 256))],
        out_specs=[pl.BlockSpec((1, lb, W_GROUP), lambda i, j: (i, j, 0)), cst, row, row, cbs],
        out_shape=[jax.ShapeDtypeStruct((bsz, seq, W_GROUP), F32),
                   jax.ShapeDtypeStruct((bsz, N_HEADS, D_D, D_D), F32),
                   jax.ShapeDtypeStruct((bsz, 1, 256), F32),
                   jax.ShapeDtypeStruct((bsz, 1, 256), F32),
                   jax.ShapeDtypeStruct((bsz, CONV_D - 1, 512), F32)],
        scratch_shapes=[pltpu.VMEM((256, 256), F32), pltpu.VMEM((1, 256), F32), pltpu.VMEM((1, 256), F32),
                        pltpu.VMEM((CONV_D - 1, 512), F32), pltpu.VMEM((lb, 512), F32)],
        compiler_params=_cp("arbitrary", "arbitrary"),
        name="mlstm_prompt",
    )(z3, z3, cb0, c0, n0, m0, cw, cbias, bif, ng)


T_COLS = 1024


def _to_rows(src_ref, dst_ref, n):
    for i in range(n // T_COLS):
        dst_ref[i * T_COLS:(i + 1) * T_COLS, :] = src_ref[:, i * T_COLS:(i + 1) * T_COLS].T


def _to_cols(src_ref, dst_ref, n):
    for i in range(n // T_COLS):
        dst_ref[:, i * T_COLS:(i + 1) * T_COLS] = src_ref[i * T_COLS:(i + 1) * T_COLS, :].T


def _per_head(x, fn):
    return jnp.concatenate([fn(x[h * 64:(h + 1) * 64], h) for h in range(N_HEADS)], axis=0)


def _expand_heads(x, nb):
    return jnp.concatenate([jnp.broadcast_to(x[h:h + 1], (64, nb)) for h in range(N_HEADS)], axis=0)


def _gla_sample_body(z_ref, s_ref, wa_ref, ba_ref, ng_ref, y_ref, so_ref,
                     st_s, dec_s, k_s, q_s, v_s, o_s, *, steps, nb):
    n_state = N_HEADS * DK_A * DV_A
    _to_rows(s_ref, st_s, n_state)
    for t in range(steps):
        zt = z_ref[t].T
        q_s[t] = zt[0:128] * DK_A ** -0.5
        k_s[t] = zt[128:256]
        v_s[t] = zt[256:512]
        lg = _log_sigmoid(_mm_hi(wa_ref[...], zt[768:896]) + ba_ref[...]) * (1.0 / GLA_TAU)
        dec_s[t] = jnp.exp(lg)
    o_s[...] = jnp.zeros(o_s.shape, F32)

    def per_key(hk, carry):
        r0 = pl.multiple_of(hk * DV_A, DV_A)
        v0 = pl.multiple_of((hk >> 5) * DV_A, DV_A)
        s = st_s[pl.ds(r0, DV_A), :]
        for t in range(steps):
            s = dec_s[t, pl.ds(hk, 1), :] * s + k_s[t, pl.ds(hk, 1), :] * v_s[t, pl.ds(v0, DV_A), :]
            o_s[t, pl.ds(v0, DV_A), :] += q_s[t, pl.ds(hk, 1), :] * s
        st_s[pl.ds(r0, DV_A), :] = s
        return carry

    lax.fori_loop(0, N_HEADS * DK_A, per_key, 0)
    for t in range(steps):
        g = z_ref[t, :, 512:768].T
        on = _per_head(o_s[t], lambda oh, h: oh * lax.rsqrt(jnp.mean(oh * oh, axis=0, keepdims=True) + 1e-6))
        y_ref[t] = (on * ng_ref[...] * _silu(g)).T
    _to_cols(st_s, so_ref, n_state)


def _gla_sample(zs, s, wa_t, ba_col, ng_col):
    steps, nb, _ = zs.shape
    n_state = N_HEADS * DK_A * DV_A
    full = lambda shape: pl.BlockSpec(shape, lambda i: (0,) * len(shape))
    return pl.pallas_call(
        functools.partial(_gla_sample_body, steps=steps, nb=nb),
        grid=(1,),
        in_specs=[pl.BlockSpec((steps, nb, GLA_W), lambda i: (0, 0, GLA_OFF // GLA_W)), full((nb, n_state)),
                  full((128, 128)), full((128, 1)), full((256, 1))],
        out_specs=[full((steps, nb, W_GROUP)), full((nb, n_state))],
        out_shape=[jax.ShapeDtypeStruct((steps, nb, W_GROUP), F32), jax.ShapeDtypeStruct((nb, n_state), F32)],
        scratch_shapes=[pltpu.VMEM((n_state, nb), F32), pltpu.VMEM((steps, 128, nb), F32),
                        pltpu.VMEM((steps, 128, nb), F32), pltpu.VMEM((steps, 128, nb), F32),
                        pltpu.VMEM((steps, 256, nb), F32), pltpu.VMEM((steps, 256, nb), F32)],
        compiler_params=_cp("arbitrary"),
        name="gla_sample",
    )(zs, s, wa_t, ba_col, ng_col)


def _pool_sample_body(z_ref, buf_ref, w_ref, sc_ref, y_ref, bo_ref, *, steps, nb, start_pos):
    ext = [buf_ref[:, j * W_GROUP:(j + 1) * W_GROUP] for j in range(POOL_BUF)] + [z_ref[t] for t in range(steps)]
    lane = _iota((nb, W_GROUP), 1)

    def by_group(vals):
        return jnp.where(lane < 64, vals[0], jnp.where(lane < 128, vals[1], jnp.where(lane < 192, vals[2], vals[3])))

    for t in range(steps):
        r = POOL_BUF + t
        acc = ext[r]
        sums = {}
        for back in range(1, max(POOL_WINDOWS)):
            acc = acc + ext[r - back]
            sums[back + 1] = acc
        win = by_group([sums[w] for w in POOL_WINDOWS])
        cnt = by_group([float(min(start_pos + t + 1, w)) for w in POOL_WINDOWS])
        y_ref[t] = _mm(win / cnt - ext[r], w_ref[...]) * sc_ref[...]
    for j in range(POOL_BUF):
        bo_ref[:, j * W_GROUP:(j + 1) * W_GROUP] = ext[steps + j]


def _pool_sample(zs, buf, wbd, scale, start_pos):
    steps, nb, _ = zs.shape
    full = lambda shape: pl.BlockSpec(shape, lambda i: (0,) * len(shape))
    return pl.pallas_call(
        functools.partial(_pool_sample_body, steps=steps, nb=nb, start_pos=start_pos),
        grid=(1,),
        in_specs=[pl.BlockSpec((steps, nb, W_GROUP), lambda i: (0, 0, POOL_OFF // W_GROUP)),
                  full((nb, POOL_BUF * W_GROUP)), full((256, 256)), full((1, 256))],
        out_specs=[full((steps, nb, W_GROUP)), full((nb, POOL_BUF * W_GROUP))],
        out_shape=[jax.ShapeDtypeStruct((steps, nb, W_GROUP), F32),
                   jax.ShapeDtypeStruct((nb, POOL_BUF * W_GROUP), F32)],
        compiler_params=_cp("arbitrary"),
        name="pool_sample",
    )(zs, buf, wbd, scale)


def _rwkv_sample_body(z_ref, sh_ref, s_ref, mu_ref, w0_ref, w2_ref, a0_ref, a2_ref, g2_ref, kk_ref, ka_ref,
                      rk_ref, lg_ref, lb_ref, y_ref, so_ref, sho_ref,
                      st_s, r_s, k_s, v_s, kk_s, b_s, w_s, g_s, o_s, *, steps, nb):
    n_state = N_HEADS * N_C * N_C
    _to_rows(s_ref, st_s, n_state)
    prev = sh_ref[...].T
    for t in range(steps):
        f = z_ref[t].T
        mixed = f + (prev - f) * mu_ref[...]
        prev = f
        k = mixed[256:512]
        low = mixed[768:896]
        w_s[t] = jnp.exp(-_sigmoid(w0_ref[...] + _mm_hi(w2_ref[...], jnp.tanh(low))) * math.exp(-0.5))
        a = _sigmoid(a0_ref[...] + _mm_hi(a2_ref[...], low))
        g_s[t] = _mm_hi(g2_ref[...], _sigmoid(low))
        kk = _per_head(k * kk_ref[...], lambda kh, h: kh * lax.rsqrt(
            jnp.maximum(jnp.sum(kh * kh, axis=0, keepdims=True), 1e-24)))
        r_s[t] = mixed[0:256]
        v_s[t] = mixed[512:768]
        kk_s[t] = kk
        k_s[t] = k * (1.0 + (a - 1.0) * ka_ref[...])
        b_s[t] = kk * a

    def per_value(hv, carry):
        r0 = pl.multiple_of(hv * N_C, N_C)
        hs = pl.ds(pl.multiple_of((hv >> 6) * N_C, N_C), N_C)
        s = st_s[pl.ds(r0, N_C), :]
        for t in range(steps):
            sa = -jnp.sum(s * kk_s[t, hs, :], axis=0, keepdims=True)
            s = s * w_s[t, hs, :] + sa * b_s[t, hs, :] + v_s[t, pl.ds(hv, 1), :] * k_s[t, hs, :]
            o_s[t, pl.ds(hv, 1), :] = jnp.sum(s * r_s[t, hs, :], axis=0, keepdims=True)
        st_s[pl.ds(r0, N_C), :] = s
        return carry

    lax.fori_loop(0, N_HEADS * N_C, per_value, 0)
    for t in range(steps):
        def norm(oh, h):
            mu = jnp.mean(oh, axis=0, keepdims=True)
            xc = oh - mu
            return xc * lax.rsqrt(jnp.mean(xc * xc, axis=0, keepdims=True) + RWKV_LN_EPS)
        on = _per_head(o_s[t], norm) * lg_ref[...] + lb_ref[...]
        rk = r_s[t] * k_s[t] * rk_ref[...]
        v = v_s[t]
        bonus = _per_head(rk, lambda x, h: jnp.sum(x, axis=0, keepdims=True) * v[h * 64:(h + 1) * 64])
        y_ref[t] = ((on + bonus) * g_s[t]).T
    _to_cols(st_s, so_ref, n_state)
    sho_ref[...] = z_ref[steps - 1]


def _rwkv_sample(zs, sh, s, p):
    steps, nb, _ = zs.shape
    n_state = N_HEADS * N_C * N_C
    full = lambda shape: pl.BlockSpec(shape, lambda i: (0,) * len(shape))
    col = full((256, 1))
    lowrank = full((256, 128))
    blk = pltpu.VMEM((steps, 256, nb), F32)
    return pl.pallas_call(
        functools.partial(_rwkv_sample_body, steps=steps, nb=nb),
        grid=(1,),
        in_specs=[pl.BlockSpec((steps, nb, RWKV_W), lambda i: (0, 0, RWKV_OFF // RWKV_W)), full((nb, RWKV_W)),
                  full((nb, n_state)), full((RWKV_W, 1)), col, lowrank, col, lowrank, lowrank, col, col, col, col, col],
        out_specs=[full((steps, nb, W_GROUP)), full((nb, n_state)), full((nb, RWKV_W))],
        out_shape=[jax.ShapeDtypeStruct((steps, nb, W_GROUP), F32), jax.ShapeDtypeStruct((nb, n_state), F32),
                   jax.ShapeDtypeStruct((nb, RWKV_W), F32)],
        scratch_shapes=[pltpu.VMEM((n_state, nb), F32), blk, blk, blk, blk, blk, blk, blk, blk],
        compiler_params=_cp("arbitrary"),
        name="rwkv_sample",
    )(zs, sh, s, p["mu_c"], p["w0_c"], p["w2_t"], p["a0_c"], p["a2_t"], p["g2_t"], p["k_k_c"], p["k_a_c"],
      p["r_k_c"], p["ln_g_c"], p["ln_b_c"])


def _mlstm_sample_body(z_ref, zif_ref, cb_ref, c_ref, n_ref, m_ref, cw_ref, cbias_ref, bif_ref, ng_ref,
                       y_ref, co_ref, no_ref, mo_ref, cbo_ref,
                       ct_s, q_s, k_s, v_s, wf_s, den_s, em_s, o_s, *, steps, nb):
    n_state = N_HEADS * D_D * D_D
    _to_rows(c_ref, ct_s, n_state)
    n = n_ref[...].T
    m = m_ref[...].T[0:N_HEADS]
    hist = [cb_ref[:, j * 512:(j + 1) * 512].T for j in range(CONV_D - 1)]
    for t in range(steps):
        hist = hist + [z_ref[t, :, 0:512].T]
        conv = cbias_ref[...]
        for jj in range(CONV_D):
            conv = conv + cw_ref[:, jj:jj + 1] * hist[jj]
        hist = hist[1:]
        act = _silu(conv)
        q = act[0:256] * D_D ** -0.5
        k = act[256:512]
        gates = zif_ref[t].T + bif_ref[...]
        ig = gates[0:N_HEADS]
        lf = _log_sigmoid(gates[N_HEADS:2 * N_HEADS])
        m_new = jnp.maximum(lf + m, ig)
        wf = _expand_heads(jnp.exp(lf + m - m_new), nb)
        kw = _expand_heads(jnp.exp(ig - m_new), nb) * k
        m = m_new
        n = wf * n + kw
        den = _per_head(q * n, lambda x, h: jnp.broadcast_to(jnp.sum(x, axis=0, keepdims=True), (64, nb)))
        q_s[t] = q
        k_s[t] = kw
        v_s[t] = z_ref[t, :, 512:768].T
        wf_s[t] = wf
        den_s[t] = den
        em_s[t] = _expand_heads(jnp.exp(-m_new), nb)
    o_s[...] = jnp.zeros(o_s.shape, F32)

    def per_key(hd, carry):
        r0 = pl.multiple_of(hd * D_D, D_D)
        e0 = pl.multiple_of((hd >> 6) * D_D, D_D)
        c = ct_s[pl.ds(r0, D_D), :]
        for t in range(steps):
            c = wf_s[t, pl.ds(hd, 1), :] * c + k_s[t, pl.ds(hd, 1), :] * v_s[t, pl.ds(e0, D_D), :]
            o_s[t, pl.ds(e0, D_D), :] += q_s[t, pl.ds(hd, 1), :] * c
        ct_s[pl.ds(r0, D_D), :] = c
        return carry

    lax.fori_loop(0, N_HEADS * D_D, per_key, 0)
    for t in range(steps):
        hh = o_s[t] / jnp.maximum(jnp.abs(den_s[t]), em_s[t])
        hn = _per_head(hh, lambda x, h: x * lax.rsqrt(jnp.mean(x * x, axis=0, keepdims=True) + 1e-6))
        y_ref[t] = (_sigmoid(z_ref[t, :, 768:1024].T) * (hn * ng_ref[...])).T
    _to_cols(ct_s, co_ref, n_state)
    no_ref[...] = n.T
    mo_ref[...] = jnp.concatenate([m, jnp.zeros((8 - N_HEADS, nb), F32)], axis=0)
    for j in range(CONV_D - 1):
        cbo_ref[:, j * 512:(j + 1) * 512] = hist[j].T


def _mlstm_sample(zs, cb, c, n, m_pad, cw_t, cbias_col, bif_col, ng_col):
    steps, nb, _ = zs.shape
    n_state = N_HEADS * D_D * D_D
    full = lambda shape: pl.BlockSpec(shape, lambda i: (0,) * len(shape))
    blk = pltpu.VMEM((steps, 256, nb), F32)
    return pl.pallas_call(
        functools.partial(_mlstm_sample_body, steps=steps, nb=nb),
        grid=(1,),
        in_specs=[pl.BlockSpec((steps, nb, ML_W), lambda i: (0, 0, ML_OFF // ML_W)),
                  pl.BlockSpec((steps, nb, MLIF_W), lambda i: (0, 0, MLIF_OFF // MLIF_W)),
                  full((nb, (CONV_D - 1) * 512)), full((nb, n_state)), full((nb, 256)), full((nb, 128)),
                  full((512, CONV_D)), full((512, 1)), full((128, 1)), full((256, 1))],
        out_specs=[full((steps, nb, W_GROUP)), full((nb, n_state)), full((nb, 256)), full((8, nb)),
                   full((nb, (CONV_D - 1) * 512))],
        out_shape=[jax.ShapeDtypeStruct((steps, nb, W_GROUP), F32), jax.ShapeDtypeStruct((nb, n_state), F32),
                   jax.ShapeDtypeStruct((nb, 256), F32), jax.ShapeDtypeStruct((8, nb), F32),
                   jax.ShapeDtypeStruct((nb, (CONV_D - 1) * 512), F32)],
        scratch_shapes=[pltpu.VMEM((n_state, nb), F32), blk, blk, blk, blk, blk, blk, blk],
        compiler_params=_cp("arbitrary"),
        name="mlstm_sample",
    )(zs, zs, cb, c, n, m_pad, cw_t, cbias_col, bif_col, ng_col)


def _pad_cols(x, n):
    return jnp.pad(x, [(0, 0)] * (x.ndim - 1) + [(0, n - x.shape[-1])])


def _rows_at(x, row0, n_rows):
    return jnp.pad(x, ((row0, n_rows - row0 - x.shape[0]), (0, 0)))


def _layer_params(l, w):
    w_in = w["w_in"][l]
    gla, pool, rwkv, ml = jnp.split(w_in, [784, 784 + 256, 784 + 256 + 832], axis=1)
    w_in_p = jnp.concatenate([_pad_cols(gla, GLA_W), _pad_cols(rwkv, RWKV_W), pool, ml[:, :1024],
                              _pad_cols(ml[:, 1024:], MLIF_W)], axis=1).astype(BF16)
    col = lambda v: v.reshape(-1, 1)
    row = lambda v: v.reshape(1, -1)
    wa = _rows_at(w["gla_w_a2"][l], 0, 128)
    w2 = _rows_at(w["rwkv_w2"][l], 0, 128)
    a2 = _rows_at(w["rwkv_a2"][l], R_W, 128)
    g2 = _rows_at(w["rwkv_g2"][l], R_W + R_AA, 128)
    mu = _pad_cols(w["rwkv_mu"][l], RWKV_W)
    bif = _pad_cols(jnp.concatenate([w["mlstm_b_i"][l], w["mlstm_b_f"][l]]), 128)
    pool_w = w["pool_w"][l]
    wbd = jnp.zeros((256, 256), F32)
    for gi in range(4):
        wbd = wbd.at[gi * 64:(gi + 1) * 64, gi * 64:(gi + 1) * 64].set(pool_w[gi])
    rw = dict(mu=row(mu), w0=row(w["rwkv_w0"][l]), w2=w2, a0=row(w["rwkv_a0"][l]), a2=a2, g2=g2,
              k_k=row(w["rwkv_k_k"][l]), k_a=row(w["rwkv_k_a"][l]), r_k=row(w["rwkv_r_k"][l]),
              ln_g=row(w["rwkv_ln_g"][l]), ln_b=row(w["rwkv_ln_b"][l]),
              mu_c=col(mu), w0_c=col(w["rwkv_w0"][l]), w2_t=w2.T, a0_c=col(w["rwkv_a0"][l]), a2_t=a2.T, g2_t=g2.T,
              k_k_c=col(w["rwkv_k_k"][l]), k_a_c=col(w["rwkv_k_a"][l]), r_k_c=col(w["rwkv_r_k"][l]),
              ln_g_c=col(w["rwkv_ln_g"][l]), ln_b_c=col(w["rwkv_ln_b"][l]))
    return dict(
        w_in=w_in_p, w_out=w["w_out"][l].astype(BF16), ln1_g=row(w["ln1_g"][l]), ln1_b=row(w["ln1_b"][l]),
        w_up=w["ffn_w_up"][l].astype(BF16), ffn_cw=w["ffn_conv_w"][l], ffn_cb=row(w["ffn_conv_b"][l]),
        w_down=w["ffn_w_down"][l].astype(BF16), ln2_g=row(w["ln2_g"][l]), ln2_b=row(w["ln2_b"][l]),
        gla_wa=wa, gla_wa_t=wa.T, gla_ba=row(w["gla_b_a"][l]), gla_ba_c=col(w["gla_b_a"][l]),
        gla_ng=row(w["gla_norm_g"][l]), gla_ng_c=col(w["gla_norm_g"][l]),
        pool_w=wbd, pool_scale=row(w["pool_scale"][l]), rwkv=rw,
        ml_cw=w["mlstm_conv_w"][l], ml_cw_t=w["mlstm_conv_w"][l].T, ml_cb=row(w["mlstm_conv_b"][l]),
        ml_cb_c=col(w["mlstm_conv_b"][l]), ml_bif=row(bif), ml_bif_c=col(bif),
        ml_ng=row(w["mlstm_norm_g"][l]), ml_ng_c=col(w["mlstm_norm_g"][l]))


PROMPT_BLOCK = 256
DENSE_TILE = 512
FFN_TILE = 256


def _prompt_layer(x, p):
    bsz, seq, d = x.shape
    z = _linear(x.reshape(bsz * seq, d), p["w_in"], DENSE_TILE).reshape(bsz, seq, ZC)
    zeros = lambda *s: jnp.zeros((bsz,) + s, F32)
    y_a, s_gla = _gla_prompt(z, zeros(N_HEADS, DK_A, DV_A), p["gla_wa"], p["gla_ba"], p["gla_ng"], PROMPT_BLOCK)
    y_b, s_pool = _pool_prompt(z, zeros(POOL_BUF, W_GROUP), p["pool_w"], p["pool_scale"], PROMPT_BLOCK, 0)
    y_c, s_rwkv, s_shift = _rwkv_prompt(z, zeros(1, RWKV_W), zeros(N_HEADS, N_C, N_C), p["rwkv"], PROMPT_BLOCK)
    y_d, s_c, s_n, s_m, s_conv = _mlstm_prompt(z, zeros(CONV_D - 1, 512), zeros(N_HEADS, D_D, D_D), zeros(1, 256),
                                               zeros(1, 256), p["ml_cw"], p["ml_cb"], p["ml_bif"], p["ml_ng"],
                                               PROMPT_BLOCK)
    flat = lambda y: y.reshape(bsz * seq, W_GROUP)
    x1 = _outproj_ln([flat(y_a), flat(y_b), flat(y_c), flat(y_d)], x.reshape(bsz * seq, d), p["w_out"],
                     p["ln1_g"], p["ln1_b"], DENSE_TILE)
    x2, s_ffn = _ffn_prompt(x1.reshape(bsz, seq, d), zeros(FFN_CONV - 1, 2 * D_FF), p["w_up"], p["ffn_cw"],
                            p["ffn_cb"], p["w_down"], p["ln2_g"], p["ln2_b"], FFN_TILE)
    states = (s_gla, s_pool, s_rwkv, s_shift[:, :, :RWKV_COLS], s_c, s_n.reshape(bsz, N_HEADS, D_D),
              s_m[:, 0, ::D_D], s_conv, s_ffn)
    return x2, states


def _sample_layer(x, st, p, steps, nb):
    s_gla, s_pool, s_rwkv, s_shift, s_c, s_n, s_m, s_conv, s_ffn = st
    d = x.shape[1]
    z = _linear(x, p["w_in"], steps * nb).reshape(steps, nb, ZC)
    y_a, n_gla = _gla_sample(z, s_gla.reshape(nb, -1), p["gla_wa_t"], p["gla_ba_c"], p["gla_ng_c"])
    y_b, n_pool = _pool_sample(z, s_pool.reshape(nb, -1), p["pool_w"], p["pool_scale"], PAST_LEN)
    y_c, n_rwkv, n_shift = _rwkv_sample(z, _pad_cols(s_shift.reshape(nb, -1), RWKV_W), s_rwkv.reshape(nb, -1),
                                        p["rwkv"])
    y_d, n_c, n_n, n_m, n_conv = _mlstm_sample(z, s_conv.reshape(nb, -1), s_c.reshape(nb, -1), s_n.reshape(nb, -1),
                                               _pad_cols(s_m, 128), p["ml_cw_t"], p["ml_cb_c"], p["ml_bif_c"],
                                               p["ml_ng_c"])
    flat = lambda y: y.reshape(steps * nb, W_GROUP)
    x1 = _outproj_ln([flat(y_a), flat(y_b), flat(y_c), flat(y_d)], x, p["w_out"], p["ln1_g"], p["ln1_b"],
                     steps * nb)
    x2, n_ffn = _ffn_sample(x1, s_ffn.reshape(nb, -1), p["w_up"], p["ffn_cw"], p["ffn_cb"], p["w_down"],
                            p["ln2_g"], p["ln2_b"], nb, steps)
    states = (n_gla.reshape(s_gla.shape), n_pool.reshape(s_pool.shape), n_rwkv.reshape(s_rwkv.shape),
              n_shift[:, :RWKV_COLS].reshape(s_shift.shape), n_c.reshape(s_c.shape), n_n.reshape(s_n.shape),
              n_m[0:N_HEADS].T, n_conv.reshape(s_conv.shape), n_ffn.reshape(s_ffn.shape))
    return x2, states


def kernel(x_prompt, x_sample, state_gla, state_pool, state_rwkv, state_rwkv_shift, state_mlstm_c, state_mlstm_n, state_mlstm_m, state_mlstm_conv, state_ffn_conv, w_in, gla_w_a2, gla_b_a, gla_norm_g, pool_w, pool_scale, rwkv_mu, rwkv_w0, rwkv_w2, rwkv_a0, rwkv_a2, rwkv_g2, rwkv_k_k, rwkv_k_a, rwkv_r_k, rwkv_ln_g, rwkv_ln_b, mlstm_conv_w, mlstm_conv_b, mlstm_b_i, mlstm_b_f, mlstm_norm_g, w_out, ln1_g, ln1_b, ffn_w_up, ffn_conv_w, ffn_conv_b, ffn_w_down, ln2_g, ln2_b):
    w = dict(w_in=w_in, gla_w_a2=gla_w_a2, gla_b_a=gla_b_a, gla_norm_g=gla_norm_g, pool_w=pool_w,
             pool_scale=pool_scale, rwkv_mu=rwkv_mu, rwkv_w0=rwkv_w0, rwkv_w2=rwkv_w2, rwkv_a0=rwkv_a0,
             rwkv_a2=rwkv_a2, rwkv_g2=rwkv_g2, rwkv_k_k=rwkv_k_k, rwkv_k_a=rwkv_k_a, rwkv_r_k=rwkv_r_k,
             rwkv_ln_g=rwkv_ln_g, rwkv_ln_b=rwkv_ln_b, mlstm_conv_w=mlstm_conv_w, mlstm_conv_b=mlstm_conv_b,
             mlstm_b_i=mlstm_b_i, mlstm_b_f=mlstm_b_f, mlstm_norm_g=mlstm_norm_g, w_out=w_out, ln1_g=ln1_g,
             ln1_b=ln1_b, ffn_w_up=ffn_w_up, ffn_conv_w=ffn_conv_w, ffn_conv_b=ffn_conv_b, ffn_w_down=ffn_w_down,
             ln2_g=ln2_g, ln2_b=ln2_b)
    sample_states = (state_gla, state_pool, state_rwkv, state_rwkv_shift, state_mlstm_c, state_mlstm_n,
                     state_mlstm_m, state_mlstm_conv, state_ffn_conv)
    nb, steps, d = x_sample.shape
    yp = x_prompt
    ys = x_sample.transpose(1, 0, 2).reshape(steps * nb, d)
    acc_p = [[] for _ in sample_states]
    acc_s = [[] for _ in sample_states]
    for l in range(DEPTH):
        p = _layer_params(l, w)
        yp, st_p = _prompt_layer(yp, p)
        ys, st_s = _sample_layer(ys, tuple(s[l] for s in sample_states), p, steps, nb)
        for i in range(len(sample_states)):
            acc_p[i].append(st_p[i])
            acc_s[i].append(st_s[i])
    ys = ys.reshape(steps, nb, d).transpose(1, 0, 2)
    out = [yp, ys]
    for sp, ss in zip(acc_p, acc_s):
        out.append(jnp.stack(sp))
        out.append(jnp.stack(ss))
    return tuple(out)
```

```python
import functools
import math

import jax
import jax.numpy as jnp
from jax import lax
from jax.experimental import pallas as pl
from jax.experimental.pallas import tpu as pltpu

F32 = jnp.float32
BF16 = jnp.bfloat16
HI = lax.Precision.HIGHEST

D_MODEL = 1024
DEPTH = 4
PAST_LEN = 16384
W_GROUP = 256
N_HEADS = 4
DK_A = 32
DV_A = 64
R_GLA = 16
GLA_TAU = 16.0
POOL_WINDOWS = (2, 4, 8, 16)
POOL_BUF = 15
N_C = 64
R_W, R_AA, R_G = 16, 16, 32
RWKV_COLS = 832
RWKV_LN_EPS = 64e-5
D_D = 64
CONV_D = 4
D_FF = 2816
FFN_CONV = 3
ALPHA = (2 * DEPTH) ** 0.25
CHUNK = 64

GLA_W = 896
RWKV_W = 896
GLA_OFF, RWKV_OFF, POOL_OFF, ML_OFF, MLIF_OFF = 0, 896, 1792, 2048, 3072
ZC = 3200
ML_W = 1024
MLIF_W = 128

VMEM_LIMIT = 56 * 1024 * 1024


def _cp(*sem):
    return pltpu.CompilerParams(dimension_semantics=sem, vmem_limit_bytes=VMEM_LIMIT)


def _mm(a, b):
    return jnp.dot(a.astype(BF16), b.astype(BF16), preferred_element_type=F32)


def _mm_nt(a, b):
    return lax.dot_general(a.astype(BF16), b.astype(BF16), (((1,), (1,)), ((), ())), preferred_element_type=F32)


def _mm_tn(a, b):
    return lax.dot_general(a.astype(BF16), b.astype(BF16), (((0,), (0,)), ((), ())), preferred_element_type=F32)


def _mm_hi(a, b):
    return jnp.dot(a, b, preferred_element_type=F32, precision=HI)


def _split2(x):
    hi = x.astype(BF16)
    return hi, (x - hi.astype(F32)).astype(BF16)


def _mm_xc(x, c2):
    hi, mid = _split2(x)
    return jnp.dot(jnp.concatenate([hi, mid], axis=1), c2, preferred_element_type=F32)


def _mm_cx(c2, x):
    hi, mid = _split2(x)
    return jnp.dot(c2, jnp.concatenate([hi, mid], axis=0), preferred_element_type=F32)


def _twice(c, axis):
    c = c.astype(F32).astype(BF16)
    return jnp.concatenate([c, c], axis=axis)


def _iota(shape, dim):
    return lax.broadcasted_iota(jnp.int32, shape, dim)


def _sigmoid(x):
    return 1.0 / (1.0 + jnp.exp(-x))


def _silu(x):
    return x * _sigmoid(x)


def _log_sigmoid(x):
    return jnp.minimum(x, 0.0) - jnp.log(1.0 + jnp.exp(-jnp.abs(x)))


def _tile4(x):
    return jnp.concatenate([x, x, x, x], axis=0)


def _block_diag_mask(rows, cols, rshift, cshift):
    return (_iota((rows, cols), 0) >> rshift) == (_iota((rows, cols), 1) >> cshift)


def _layer_norm(h, g, b, eps):
    mu = jnp.mean(h, axis=-1, keepdims=True)
    xc = h - mu
    var = jnp.mean(xc * xc, axis=-1, keepdims=True)
    return xc * lax.rsqrt(var + eps) * g + b


def _linear_body(x_ref, w_ref, o_ref):
    o_ref[...] = jnp.dot(x_ref[...].astype(BF16), w_ref[...], preferred_element_type=F32)


def _linear(x, w, tm):
    t, k = x.shape
    n = w.shape[1]
    return pl.pallas_call(
        _linear_body,
        grid=(t // tm,),
        in_specs=[pl.BlockSpec((tm, k), lambda i: (i, 0)), pl.BlockSpec((k, n), lambda i: (0, 0))],
        out_specs=pl.BlockSpec((tm, n), lambda i: (i, 0)),
        out_shape=jax.ShapeDtypeStruct((t, n), F32),
        compiler_params=_cp("arbitrary"),
        name="in_proj",
    )(x, w)


def _outproj_body(ya_ref, yb_ref, yc_ref, yd_ref, x_ref, w_ref, g_ref, b_ref, o_ref):
    y = jnp.concatenate([ya_ref[...], yb_ref[...], yc_ref[...], yd_ref[...]], axis=1).astype(BF16)
    mixed = jnp.dot(y, w_ref[...], preferred_element_type=F32)
    o_ref[...] = _layer_norm(ALPHA * x_ref[...] + mixed, g_ref[...], b_ref[...], 1e-5)


def _outproj_ln(ys, x, w, g, b, tm):
    t, d = x.shape
    yspec = pl.BlockSpec((tm, W_GROUP), lambda i: (i, 0))
    vec = pl.BlockSpec((1, d), lambda i: (0, 0))
    return pl.pallas_call(
        _outproj_body,
        grid=(t // tm,),
        in_specs=[yspec, yspec, yspec, yspec, pl.BlockSpec((tm, d), lambda i: (i, 0)),
                  pl.BlockSpec((d, d), lambda i: (0, 0)), vec, vec],
        out_specs=pl.BlockSpec((tm, d), lambda i: (i, 0)),
        out_shape=jax.ShapeDtypeStruct((t, d), F32),
        compiler_params=_cp("arbitrary"),
        name="out_proj_ln",
    )(*ys, x, w, g, b)


FF_CHUNK = 256
N_FF_CHUNKS = D_FF // FF_CHUNK


def _ffn_tail(x, h_ref, wdn_ref, g_ref, b_ref):
    f = jnp.dot(h_ref[...], wdn_ref[...], preferred_element_type=F32)
    return _layer_norm(ALPHA * x + f, g_ref[...], b_ref[...], 1e-5)


def _ffn_prompt_body(x_ref, buf_ref, wup_ref, cw_ref, cb_ref, wdn_ref, g_ref, b_ref,
                     o_ref, st_ref, carry_ref, h_ref, *, tm, n_tiles):
    j = pl.program_id(1)

    @pl.when(j == 0)
    def _():
        carry_ref[...] = buf_ref[0]

    x = x_ref[0]
    xb = x.astype(BF16)
    for c in range(N_FF_CHUNKS):
        acts = []
        for half in range(2):
            sl = slice(half * D_FF + c * FF_CHUNK, half * D_FF + (c + 1) * FF_CHUNK)
            u = jnp.dot(xb, wup_ref[:, sl], preferred_element_type=F32)
            ext = jnp.concatenate([carry_ref[:, sl], u], axis=0)
            conv = (cb_ref[:, sl] + cw_ref[0:1, sl] * ext[0:tm] + cw_ref[1:2, sl] * ext[1:tm + 1]
                    + cw_ref[2:3, sl] * ext[2:tm + 2])
            carry_ref[:, sl] = ext[tm:tm + 2]
            acts.append(conv)
        h_ref[:, c * FF_CHUNK:(c + 1) * FF_CHUNK] = (_silu(acts[0]) * acts[1]).astype(BF16)
    o_ref[0] = _ffn_tail(x, h_ref, wdn_ref, g_ref, b_ref)

    @pl.when(j == n_tiles - 1)
    def _():
        st_ref[0] = carry_ref[...]


def _ffn_prompt(x3, buf, wup, cw, cb, wdn, g, b, tm):
    bsz, seq, d = x3.shape
    n_tiles = seq // tm
    const = lambda shape: pl.BlockSpec(shape, lambda i, j: (0,) * len(shape))
    return pl.pallas_call(
        functools.partial(_ffn_prompt_body, tm=tm, n_tiles=n_tiles),
        grid=(bsz, n_tiles),
        in_specs=[pl.BlockSpec((1, tm, d), lambda i, j: (i, j, 0)),
                  pl.BlockSpec((1, 2, 2 * D_FF), lambda i, j: (i, 0, 0)),
                  const((d, 2 * D_FF)), const((FFN_CONV, 2 * D_FF)), const((1, 2 * D_FF)),
                  const((D_FF, d)), const((1, d)), const((1, d))],
        out_specs=[pl.BlockSpec((1, tm, d), lambda i, j: (i, j, 0)),
                   pl.BlockSpec((1, 2, 2 * D_FF), lambda i, j: (i, 0, 0))],
        out_shape=[jax.ShapeDtypeStruct((bsz, seq, d), F32), jax.ShapeDtypeStruct((bsz, 2, 2 * D_FF), F32)],
        scratch_shapes=[pltpu.VMEM((2, 2 * D_FF), F32), pltpu.VMEM((tm, D_FF), BF16)],
        compiler_params=_cp("arbitrary", "arbitrary"),
        name="ffn_prompt",
    )(x3, buf, wup, cw, cb, wdn, g, b)


def _ffn_sample_body(x_ref, buf_ref, wup_ref, cw_ref, cb_ref, wdn_ref, g_ref, b_ref,
                     o_ref, st_ref, h_ref, *, nb, steps):
    t = nb * steps
    x = x_ref[...]
    xb = x.astype(BF16)
    for c in range(N_FF_CHUNKS):
        acts = []
        for half in range(2):
            lo = half * D_FF + c * FF_CHUNK
            sl = slice(lo, lo + FF_CHUNK)
            u = jnp.dot(xb, wup_ref[:, sl], preferred_element_type=F32)
            b0 = buf_ref[:, lo:lo + FF_CHUNK]
            b1 = buf_ref[:, 2 * D_FF + lo:2 * D_FF + lo + FF_CHUNK]
            back1 = jnp.concatenate([b1, u[0:t - nb]], axis=0)
            back2 = jnp.concatenate([b0, b1, u[0:t - 2 * nb]], axis=0)
            conv = cb_ref[:, sl] + cw_ref[0:1, sl] * back2 + cw_ref[1:2, sl] * back1 + cw_ref[2:3, sl] * u
            st_ref[:, lo:lo + FF_CHUNK] = u[t - 2 * nb:t - nb]
            st_ref[:, 2 * D_FF + lo:2 * D_FF + lo + FF_CHUNK] = u[t - nb:t]
            acts.append(conv)
        h_ref[:, c * FF_CHUNK:(c + 1) * FF_CHUNK] = (_silu(acts[0]) * acts[1]).astype(BF16)
    o_ref[...] = _ffn_tail(x, h_ref, wdn_ref, g_ref, b_ref)


def _ffn_sample(x, buf, wup, cw, cb, wdn, g, b, nb, steps):
    t, d = x.shape
    return pl.pallas_call(
        functools.partial(_ffn_sample_body, nb=nb, steps=steps),
        out_shape=[jax.ShapeDtypeStruct((t, d), F32), jax.ShapeDtypeStruct((nb, 4 * D_FF), F32)],
        scratch_shapes=[pltpu.VMEM((t, D_FF), BF16)],
        compiler_params=pltpu.CompilerParams(vmem_limit_bytes=VMEM_LIMIT),
        name="ffn_sample",
    )(x, buf, wup, cw, cb, wdn, g, b)


def _tril64():
    return (_iota((CHUNK, CHUNK), 0) >= _iota((CHUNK, CHUNK), 1)).astype(F32)


def _exp_masks():
    t = _iota((CHUNK, 256), 0)
    s = _iota((CHUNK, 256), 1) & (CHUNK - 1)
    return t >= s, t > s, t == s


def _bd256():
    return _block_diag_mask(256, 256, 6, 6)


def _tril_blocks(n):
    r = _iota((n, n), 0)
    c = _iota((n, n), 1)
    return ((r >= c) & ((r >> 6) == (c >> 6))).astype(F32)


def _bdx(x, mask):
    return jnp.where(mask, _tile4(x.astype(BF16)), jnp.zeros((), BF16))


def _put_block_diag(dst_ref, blocks, rows, cols):
    dst_ref[...] = jnp.zeros(dst_ref.shape, F32)
    for h in range(N_HEADS):
        dst_ref[h * rows:(h + 1) * rows, h * cols:(h + 1) * cols] = blocks[h]


def _get_block_diag(src_ref, out_ref, rows, cols):
    for h in range(N_HEADS):
        out_ref[0, h] = src_ref[h * rows:(h + 1) * rows, h * cols:(h + 1) * cols]


def _gla_prompt_body(z_ref, s0_ref, wa_ref, ba_ref, ng_ref, y_ref, so_ref, sbd_ref, *, n_chunks, n_blocks):
    j = pl.program_id(1)

    @pl.when(j == 0)
    def _():
        _put_block_diag(sbd_ref, s0_ref[0], DK_A, DV_A)

    lb = n_chunks * CHUNK
    causal, _, _ = _exp_masks()
    bd_state = _block_diag_mask(128, 256, 5, 6)
    bd_k = _block_diag_mask(256, 128, 6, 5)
    bd_v = _bd256()
    pavg2 = _twice(bd_v.astype(F32) * (1.0 / DV_A), 0)

    lg = _log_sigmoid(_mm(z_ref[0, :, 768:896], wa_ref[...]) + ba_ref[...]) * (1.0 / GLA_TAU)
    bc_all = _mm_cx(_twice(_tril_blocks(lb), 1), lg)

    o_intra, q_dec, incr, decay = [], [], [], []
    for c in range(n_chunks):
        rows = slice(c * CHUNK, (c + 1) * CHUNK)
        q = z_ref[0, rows, 0:128] * DK_A ** -0.5
        k = z_ref[0, rows, 128:256]
        v = z_ref[0, rows, 256:512]
        bc = bc_all[rows]
        bl = bc[CHUNK - 1:CHUNK]
        rho = bc[CHUNK // 2 - 1:CHUNK // 2]
        att = _mm_nt(q * jnp.exp(bc - rho), _bdx(k * jnp.exp(rho - bc), bd_k))
        o_intra.append(_mm(jnp.where(causal, att, 0.0), _bdx(v, bd_v)))
        q_dec.append(q * jnp.exp(bc))
        incr.append(jnp.where(bd_state, _mm_tn(k * jnp.exp(bl - bc), v), 0.0))
        decay_col = jnp.broadcast_to(jnp.exp(bl), (128, 128)).T
        decay.append(jnp.concatenate([decay_col, decay_col], axis=1))
    s = sbd_ref[...]
    o = []
    for c in range(n_chunks):
        o.append(_mm(q_dec[c], s) + o_intra[c])
        s = decay[c] * s + incr[c]
    sbd_ref[...] = s
    o = jnp.concatenate(o, axis=0)
    ms = _mm_xc(o * o, pavg2)
    y_ref[0] = o * lax.rsqrt(ms + 1e-6) * ng_ref[...] * _silu(z_ref[0, :, 512:768])

    @pl.when(j == n_blocks - 1)
    def _():
        _get_block_diag(sbd_ref, so_ref, DK_A, DV_A)


def _gla_prompt(z3, s0, wa, ba, ng, lb):
    bsz, seq, _ = z3.shape
    n_blocks = seq // lb
    const = lambda shape: pl.BlockSpec(shape, lambda i, j: (0,) * len(shape))
    st = pl.BlockSpec((1, N_HEADS, DK_A, DV_A), lambda i, j: (i, 0, 0, 0))
    return pl.pallas_call(
        functools.partial(_gla_prompt_body, n_chunks=lb // CHUNK, n_blocks=n_blocks),
        grid=(bsz, n_blocks),
        in_specs=[pl.BlockSpec((1, lb, GLA_W), lambda i, j: (i, j, GLA_OFF // GLA_W)), st,
                  const((128, 128)), const((1, 128)), const((1, 256))],
        out_specs=[pl.BlockSpec((1, lb, W_GROUP), lambda i, j: (i, j, 0)), st],
        out_shape=[jax.ShapeDtypeStruct((bsz, seq, W_GROUP), F32),
                   jax.ShapeDtypeStruct((bsz, N_HEADS, DK_A, DV_A), F32)],
        scratch_shapes=[pltpu.VMEM((128, 256), F32)],
        compiler_params=_cp("arbitrary", "arbitrary"),
        name="gla_prompt",
    )(z3, s0, wa, ba, ng)


def _pool_prompt_body(z_ref, buf_ref, w_ref, sc_ref, y_ref, bo_ref, hist_ref, *, lb, n_blocks, start_pos):
    j = pl.program_id(1)

    @pl.when(j == 0)
    def _():
        hist_ref[0:1, :] = jnp.zeros((1, W_GROUP), F32)
        hist_ref[1:16, :] = buf_ref[0]

    zp = z_ref[0]
    e = jnp.concatenate([hist_ref[...], zp], axis=0)
    s2 = e[1:] + e[:-1]
    s4 = s2[2:] + s2[:-2]
    s8 = s4[4:] + s4[:-4]
    s16 = s8[8:] + s8[:-8]
    lane = _iota((lb, W_GROUP), 1)
    win = jnp.where(lane < 64, s2[15:], jnp.where(lane < 128, s4[13:], jnp.where(lane < 192, s8[9:], s16[1:])))
    width = jnp.where(lane < 64, 2, jnp.where(lane < 128, 4, jnp.where(lane < 192, 8, 16)))
    pos = start_pos + j * lb + _iota((lb, W_GROUP), 0)
    cnt = jnp.minimum(pos + 1, width).astype(F32)
    y_ref[0] = _mm(win / cnt - zp, w_ref[...]) * sc_ref[...]
    hist_ref[1:16, :] = e[lb + 1:lb + 16]

    @pl.when(j == n_blocks - 1)
    def _():
        bo_ref[0] = hist_ref[1:16, :]


def _pool_prompt(z3, buf, wbd, scale, lb, start_pos):
    bsz, seq, _ = z3.shape
    n_blocks = seq // lb
    const = lambda shape: pl.BlockSpec(shape, lambda i, j: (0,) * len(shape))
    st = pl.BlockSpec((1, POOL_BUF, W_GROUP), lambda i, j: (i, 0, 0))
    return pl.pallas_call(
        functools.partial(_pool_prompt_body, lb=lb, n_blocks=n_blocks, start_pos=start_pos),
        grid=(bsz, n_blocks),
        in_specs=[pl.BlockSpec((1, lb, W_GROUP), lambda i, j: (i, j, POOL_OFF // W_GROUP)), st,
                  const((256, 256)), const((1, 256))],
        out_specs=[pl.BlockSpec((1, lb, W_GROUP), lambda i, j: (i, j, 0)), st],
        out_shape=[jax.ShapeDtypeStruct((bsz, seq, W_GROUP), F32),
                   jax.ShapeDtypeStruct((bsz, POOL_BUF, W_GROUP), F32)],
        scratch_shapes=[pltpu.VMEM((16, W_GROUP), F32)],
        compiler_params=_cp("arbitrary", "arbitrary"),
        name="pool_prompt",
    )(z3, buf, wbd, scale)


def _rwkv_prompt_body(z_ref, sh0_ref, s0_ref, mu_ref, w0_ref, w2_ref, a0_ref, a2_ref, g2_ref, kk_ref, ka_ref,
                      rk_ref, lg_ref, lb_ref, y_ref, so_ref, sho_ref,
                      nbd_ref, carry_ref, r_s, k_s, v_s, kk_s, b_s, lw_s, g_s, *, n_chunks, n_blocks, lb):
    j = pl.program_id(1)

    @pl.when(j == 0)
    def _():
        _put_block_diag(nbd_ref, s0_ref[0], N_C, N_C)
        carry_ref[...] = sh0_ref[0]

    bd = _bd256()
    ones2 = _twice(bd, 0)
    pavg2 = _twice(bd.astype(F32) * (1.0 / N_C), 0)

    f = z_ref[0]
    prev = jnp.concatenate([carry_ref[...], f[:-1]], axis=0)
    mixed = f + (prev - f) * mu_ref[...]
    carry_ref[...] = f[lb - 1:lb]
    r = mixed[:, 0:256]
    k = mixed[:, 256:512]
    v = mixed[:, 512:768]
    low = mixed[:, 768:896]
    lw_all = -_sigmoid(w0_ref[...] + _mm(jnp.tanh(low), w2_ref[...])) * math.exp(-0.5)
    lw_s[...] = lw_all
    a = _sigmoid(a0_ref[...] + _mm(low, a2_ref[...]))
    g_s[...] = _mm(_sigmoid(low), g2_ref[...])
    kk = k * kk_ref[...]
    kk = kk * lax.rsqrt(jnp.maximum(_mm_xc(kk * kk, ones2), 1e-24))
    r_s[...] = r
    v_s[...] = v
    kk_s[...] = kk
    k_s[...] = k * (1.0 + (a - 1.0) * ka_ref[...])
    b_s[...] = kk * a
    gc_all = _mm_cx(_twice(_tril_blocks(lb), 1), lw_all)

    lower, strict, eye_exp = _exp_masks()
    eye_f = eye_exp.astype(F32)

    cs = range(n_chunks)
    sl = [slice(c * CHUNK, (c + 1) * CHUNK) for c in cs]
    gc = [gc_all[s] for s in sl]
    kt = [kk_s[sl[c], :] * jnp.exp(gc[c] - lw_s[sl[c], :]) for c in cs]
    rt = [r_s[sl[c], :] * jnp.exp(gc[c]) for c in cs]
    a_kk, a_kb, a_rk, a_rb = [], [], [], []
    for c in cs:
        einv = jnp.exp(-gc[c])
        lhs = jnp.concatenate([kt[c], rt[c]], axis=0)
        rhs = jnp.concatenate([_bdx(k_s[sl[c], :] * einv, bd), _bdx(b_s[sl[c], :] * einv, bd)], axis=0)
        aa = _mm_nt(lhs, rhs)
        a_kk.append(jnp.where(strict, aa[0:CHUNK, 0:256], 0.0))
        a_kb.append(jnp.where(strict, aa[0:CHUNK, 256:512], 0.0))
        a_rk.append(jnp.where(lower, aa[CHUNK:, 0:256], 0.0))
        a_rb.append(jnp.where(lower, aa[CHUNK:, 256:512], 0.0))
    t = [eye_f - a_kb[c] for c in cs]
    p = [_mm(a_kb[c], _bdx(a_kb[c], bd)) for c in cs]
    for _ in range(4):
        m = [_mm(jnp.concatenate([p[c], t[c]], axis=0), _bdx(p[c], bd)) for c in cs]
        p = [m[c][0:CHUNK] for c in cs]
        t = [t[c] + m[c][CHUNK:] for c in cs]
    t = [t[c] + _mm(t[c], _bdx(p[c], bd)) for c in cs]
    av = [_mm(jnp.concatenate([a_kk[c], a_rk[c]], axis=0), _bdx(v_s[sl[c], :], bd)) for c in cs]
    tu = [_mm(t[c], jnp.concatenate([_bdx(av[c][0:CHUNK], bd), _bdx(kt[c], bd)], axis=1)) for c in cs]
    u0 = [tu[c][:, 0:256] for c in cs]
    tk = [tu[c][:, 256:512] for c in cs]
    ro = [_mm(a_rb[c], jnp.concatenate([_bdx(tk[c], bd), _bdx(u0[c], bd)], axis=1)) for c in cs]
    r_eff = [rt[c] - ro[c][:, 0:256] for c in cs]
    o0 = [av[c][CHUNK:] - ro[c][:, 256:512] for c in cs]
    a_st, b_st, decay = [], [], []
    for c in cs:
        gl = gc[c][CHUNK - 1:CHUNK]
        dk = jnp.exp(gl - gc[c])
        kd = k_s[sl[c], :] * dk
        bdk = b_s[sl[c], :] * dk
        a_st.append(jnp.where(bd, -_mm_tn(tk[c], bdk), 0.0))
        b_st.append(jnp.where(bd, _mm_tn(jnp.concatenate([v_s[sl[c], :], -u0[c]], axis=0),
                                         jnp.concatenate([kd, bdk], axis=0)), 0.0))
        decay.append(jnp.exp(gl))
    n = nbd_ref[...]
    o = []
    for c in cs:
        o.append(_mm_nt(r_eff[c], n) + o0[c])
        n = n * decay[c] + _mm(n, a_st[c]) + b_st[c]
    nbd_ref[...] = n
    o = jnp.concatenate(o, axis=0)
    mu = _mm_xc(o, pavg2)
    xc = o - mu
    var = _mm_xc(xc * xc, pavg2)
    on = xc * lax.rsqrt(var + RWKV_LN_EPS) * lg_ref[...] + lb_ref[...]
    bonus = _mm_xc(r_s[...] * k_s[...] * rk_ref[...], ones2) * v_s[...]
    y_ref[0] = (on + bonus) * g_s[...]

    @pl.when(j == n_blocks - 1)
    def _():
        _get_block_diag(nbd_ref, so_ref, N_C, N_C)
        sho_ref[0] = carry_ref[...]


def _rwkv_prompt(z3, sh0, s0, p, lb):
    bsz, seq, _ = z3.shape
    n_blocks = seq // lb
    const = lambda shape: pl.BlockSpec(shape, lambda i, j: (0,) * len(shape))
    st = pl.BlockSpec((1, N_HEADS, N_C, N_C), lambda i, j: (i, 0, 0, 0))
    sh = pl.BlockSpec((1, 1, RWKV_W), lambda i, j: (i, 0, 0))
    vec = const((1, 256))
    lowrank = const((128, 256))
    blk = pltpu.VMEM((lb, 256), F32)
    return pl.pallas_call(
        functools.partial(_rwkv_prompt_body, n_chunks=lb // CHUNK, n_blocks=n_blocks, lb=lb),
        grid=(bsz, n_blocks),
        in_specs=[pl.BlockSpec((1, lb, RWKV_W), lambda i, j: (i, j, RWKV_OFF // RWKV_W)), sh, st,
                  const((1, RWKV_W)), vec, lowrank, vec, lowrank, lowrank, vec, vec, vec, vec, vec],
        out_specs=[pl.BlockSpec((1, lb, W_GROUP), lambda i, j: (i, j, 0)), st, sh],
        out_shape=[jax.ShapeDtypeStruct((bsz, seq, W_GROUP), F32),
                   jax.ShapeDtypeStruct((bsz, N_HEADS, N_C, N_C), F32),
                   jax.ShapeDtypeStruct((bsz, 1, RWKV_W), F32)],
        scratch_shapes=[pltpu.VMEM((256, 256), F32), pltpu.VMEM((1, RWKV_W), F32),
                        blk, blk, blk, blk, blk, blk, blk],
        compiler_params=_cp("arbitrary", "arbitrary"),
        name="rwkv_prompt",
    )(z3, sh0, s0, p["mu"], p["w0"], p["w2"], p["a0"], p["a2"], p["g2"], p["k_k"], p["k_a"], p["r_k"],
      p["ln_g"], p["ln_b"])


def _mlstm_prompt_body(z_ref, zif_ref, cb0_ref, c0_ref, n0_ref, m0_ref, cw_ref, cbias_ref, bif_ref, ng_ref,
                       y_ref, co_ref, no_ref, mo_ref, cbo_ref,
                       cbd_ref, n_ref, m_ref, carry_ref, qk_s, *, n_chunks, n_blocks, lb):
    j = pl.program_id(1)

    @pl.when(j == 0)
    def _():
        _put_block_diag(cbd_ref, c0_ref[0], D_D, D_D)
        n_ref[...] = n0_ref[0]
        m_ref[...] = m0_ref[0]
        carry_ref[...] = cb0_ref[0]

    qk = z_ref[0, :, 0:512]
    ext = jnp.concatenate([carry_ref[...], qk], axis=0)
    conv = cbias_ref[...]
    for jj in range(CONV_D):
        conv = conv + cw_ref[jj:jj + 1, :] * ext[jj:jj + lb]
    qk_s[...] = _silu(conv)
    carry_ref[...] = ext[lb:lb + 3]

    bd = _bd256()
    ones2 = _twice(bd, 0)
    pavg2 = _twice(bd.astype(F32) * (1.0 / D_D), 0)
    causal, _, eye_exp = _exp_masks()
    er = _iota((128, 512), 0)
    ec = _iota((128, 512), 1)
    expand2 = _twice(((ec < 256) & (er == (ec >> 6))) | ((ec >= 256) & (er == 4 + ((ec - 256) >> 6))), 0)
    is_f = (_iota((lb, 128), 1) >= 4) & (_iota((lb, 128), 1) < 8)
    lane = _iota((CHUNK, 256), 1)

    gates = zif_ref[0] + bif_ref[...]
    gates = jnp.where(is_f, _log_sigmoid(gates), gates)
    gates = jnp.where(is_f, _mm_cx(_twice(_tril_blocks(lb), 1), gates), gates)
    ge_all = _mm_xc(gates, expand2)

    for c in range(n_chunks):
        rows = slice(c * CHUNK, (c + 1) * CHUNK)
        q = qk_s[rows, 0:256] * D_D ** -0.5
        k = qk_s[rows, 256:512]
        v = z_ref[0, rows, 512:768]
        og = z_ref[0, rows, 768:1024]
        iexp = ge_all[rows, 0:256]
        fexp = ge_all[rows, 256:512]
        d_row = jnp.sum(jnp.where(eye_exp, iexp - fexp, 0.0), axis=0, keepdims=True)
        dlog = jnp.where(causal, fexp + d_row, -jnp.inf)
        mx = [jnp.max(dlog[:, h * 64:(h + 1) * 64], axis=1, keepdims=True) for h in range(N_HEADS)]
        mx = jnp.where(lane < 64, mx[0], jnp.where(lane < 128, mx[1], jnp.where(lane < 192, mx[2], mx[3])))
        qk_att = _mm_nt(q, _bdx(k, bd))
        m_prev = m_ref[...]
        inter = fexp + m_prev
        m_t = jnp.maximum(inter, mx)
        w0 = jnp.exp(inter - m_t)
        s = jnp.exp(dlog - m_t) * qk_att
        cbd = cbd_ref[...]
        n_row = n_ref[...]
        num = w0 * _mm(q, cbd) + _mm(s, _bdx(v, bd))
        den = _mm_xc(w0 * (q * n_row) + s, ones2)
        hh = num / jnp.maximum(jnp.abs(den), jnp.exp(-m_t))
        ms = _mm_xc(hh * hh, pavg2)
        y_ref[0, rows, :] = _sigmoid(og) * (hh * lax.rsqrt(ms + 1e-6) * ng_ref[...])
        m_new = m_t[CHUNK - 1:CHUNK]
        f_last = fexp[CHUNK - 1:CHUNK]
        kw = k * jnp.exp(f_last - fexp + iexp - m_new)
        w0f = jnp.exp(f_last + m_prev - m_new)
        cbd_ref[...] = jnp.where(bd, w0f * cbd + _mm_tn(kw, v), 0.0)
        n_ref[...] = w0f * n_row + jnp.sum(kw, axis=0, keepdims=True)
        m_ref[...] = m_new

    @pl.when(j == n_blocks - 1)
    def _():
        _get_block_diag(cbd_ref, co_ref, D_D, D_D)
        no_ref[0] = n_ref[...]
        mo_ref[0] = m_ref[...]
        cbo_ref[0] = carry_ref[...]


def _mlstm_prompt(z3, cb0, c0, n0, m0, cw, cbias, bif, ng, lb):
    bsz, seq, _ = z3.shape
    n_blocks = seq // lb
    const = lambda shape: pl.BlockSpec(shape, lambda i, j: (0,) * len(shape))
    cst = pl.BlockSpec((1, N_HEADS, D_D, D_D), lambda i, j: (i, 0, 0, 0))
    row = pl.BlockSpec((1, 1, 256), lambda i, j: (i, 0, 0))
    cbs = pl.BlockSpec((1, CONV_D - 1, 512), lambda i, j: (i, 0, 0))
    return pl.pallas_call(
        functools.partial(_mlstm_prompt_body, n_chunks=lb // CHUNK, n_blocks=n_blocks, lb=lb),
        grid=(bsz, n_blocks),
        in_specs=[pl.BlockSpec((1, lb, ML_W), lambda i, j: (i, j, ML_OFF // ML_W)),
                  pl.BlockSpec((1, lb, MLIF_W), lambda i, j: (i, j, MLIF_OFF // MLIF_W)),
                  cbs, cst, row, row, const((CONV_D, 512)), const((1, 512)), const((1, 128)), const((1, 256))],
        out_specs=[pl.BlockSpec((1, lb, W_GROUP), lambda i, j: (i, j, 0)), cst, row, row, cbs],
        out_shape=[jax.ShapeDtypeStruct((bsz, seq, W_GROUP), F32),
                   jax.ShapeDtypeStruct((bsz, N_HEADS, D_D, D_D), F32),
                   jax.ShapeDtypeStruct((bsz, 1, 256), F32),
                   jax.ShapeDtypeStruct((bsz, 1, 256), F32),
                   jax.ShapeDtypeStruct((bsz, CONV_D - 1, 512), F32)],
        scratch_shapes=[pltpu.VMEM((256, 256), F32), pltpu.VMEM((1, 256), F32), pltpu.VMEM((1, 256), F32),
                        pltpu.VMEM((CONV_D - 1, 512), F32), pltpu.VMEM((lb, 512), F32)],
        compiler_params=_cp("arbitrary", "arbitrary"),
        name="mlstm_prompt",
    )(z3, z3, cb0, c0, n0, m0, cw, cbias, bif, ng)


T_COLS = 1024


def _to_rows(src_ref, dst_ref, n):
    for i in range(n // T_COLS):
        dst_ref[i * T_COLS:(i + 1) * T_COLS, :] = src_ref[:, i * T_COLS:(i + 1) * T_COLS].T


def _to_cols(src_ref, dst_ref, n):
    for i in range(n // T_COLS):
        dst_ref[:, i * T_COLS:(i + 1) * T_COLS] = src_ref[i * T_COLS:(i + 1) * T_COLS, :].T


def _per_head(x, fn):
    return jnp.concatenate([fn(x[h * 64:(h + 1) * 64], h) for h in range(N_HEADS)], axis=0)


def _expand_heads(x, nb):
    return jnp.concatenate([jnp.broadcast_to(x[h:h + 1], (64, nb)) for h in range(N_HEADS)], axis=0)


def _gla_sample_body(z_ref, s_ref, wa_ref, ba_ref, ng_ref, y_ref, so_ref,
                     st_s, dec_s, k_s, q_s, v_s, o_s, *, steps, nb):
    n_state = N_HEADS * DK_A * DV_A
    _to_rows(s_ref, st_s, n_state)
    for t in range(steps):
        zt = z_ref[t].T
        q_s[t] = zt[0:128] * DK_A ** -0.5
        k_s[t] = zt[128:256]
        v_s[t] = zt[256:512]
        lg = _log_sigmoid(_mm_hi(wa_ref[...], zt[768:896]) + ba_ref[...]) * (1.0 / GLA_TAU)
        dec_s[t] = jnp.exp(lg)
    o_s[...] = jnp.zeros(o_s.shape, F32)

    def per_key(hk, carry):
        r0 = pl.multiple_of(hk * DV_A, DV_A)
        v0 = pl.multiple_of((hk >> 5) * DV_A, DV_A)
        s = st_s[pl.ds(r0, DV_A), :]
        for t in range(steps):
            s = dec_s[t, pl.ds(hk, 1), :] * s + k_s[t, pl.ds(hk, 1), :] * v_s[t, pl.ds(v0, DV_A), :]
            o_s[t, pl.ds(v0, DV_A), :] += q_s[t, pl.ds(hk, 1), :] * s
        st_s[pl.ds(r0, DV_A), :] = s
        return carry

    lax.fori_loop(0, N_HEADS * DK_A, per_key, 0)
    for t in range(steps):
        g = z_ref[t, :, 512:768].T
        on = _per_head(o_s[t], lambda oh, h: oh * lax.rsqrt(jnp.mean(oh * oh, axis=0, keepdims=True) + 1e-6))
        y_ref[t] = (on * ng_ref[...] * _silu(g)).T
    _to_cols(st_s, so_ref, n_state)


def _gla_sample(zs, s, wa_t, ba_col, ng_col):
    steps, nb, _ = zs.shape
    n_state = N_HEADS * DK_A * DV_A
    full = lambda shape: pl.BlockSpec(shape, lambda i: (0,) * len(shape))
    return pl.pallas_call(
        functools.partial(_gla_sample_body, steps=steps, nb=nb),
        grid=(1,),
        in_specs=[pl.BlockSpec((steps, nb, GLA_W), lambda i: (0, 0, GLA_OFF // GLA_W)), full((nb, n_state)),
                  full((128, 128)), full((128, 1)), full((256, 1))],
        out_specs=[full((steps, nb, W_GROUP)), full((nb, n_state))],
        out_shape=[jax.ShapeDtypeStruct((steps, nb, W_GROUP), F32), jax.ShapeDtypeStruct((nb, n_state), F32)],
        scratch_shapes=[pltpu.VMEM((n_state, nb), F32), pltpu.VMEM((steps, 128, nb), F32),
                        pltpu.VMEM((steps, 128, nb), F32), pltpu.VMEM((steps, 128, nb), F32),
                        pltpu.VMEM((steps, 256, nb), F32), pltpu.VMEM((steps, 256, nb), F32)],
        compiler_params=_cp("arbitrary"),
        name="gla_sample",
    )(zs, s, wa_t, ba_col, ng_col)


def _pool_sample_body(z_ref, buf_ref, w_ref, sc_ref, y_ref, bo_ref, *, steps, nb, start_pos):
    ext = [buf_ref[:, j * W_GROUP:(j + 1) * W_GROUP] for j in range(POOL_BUF)] + [z_ref[t] for t in range(steps)]
    lane = _iota((nb, W_GROUP), 1)

    def by_group(vals):
        return jnp.where(lane < 64, vals[0], jnp.where(lane < 128, vals[1], jnp.where(lane < 192, vals[2], vals[3])))

    for t in range(steps):
        r = POOL_BUF + t
        acc = ext[r]
        sums = {}
        for back in range(1, max(POOL_WINDOWS)):
            acc = acc + ext[r - back]
            sums[back + 1] = acc
        win = by_group([sums[w] for w in POOL_WINDOWS])
        cnt = by_group([float(min(start_pos + t + 1, w)) for w in POOL_WINDOWS])
        y_ref[t] = _mm(win / cnt - ext[r], w_ref[...]) * sc_ref[...]
    for j in range(POOL_BUF):
        bo_ref[:, j * W_GROUP:(j + 1) * W_GROUP] = ext[steps + j]


def _pool_sample(zs, buf, wbd, scale, start_pos):
    steps, nb, _ = zs.shape
    full = lambda shape: pl.BlockSpec(shape, lambda i: (0,) * len(shape))
    return pl.pallas_call(
        functools.partial(_pool_sample_body, steps=steps, nb=nb, start_pos=start_pos),
        grid=(1,),
        in_specs=[pl.BlockSpec((steps, nb, W_GROUP), lambda i: (0, 0, POOL_OFF // W_GROUP)),
                  full((nb, POOL_BUF * W_GROUP)), full((256, 256)), full((1, 256))],
        out_specs=[full((steps, nb, W_GROUP)), full((nb, POOL_BUF * W_GROUP))],
        out_shape=[jax.ShapeDtypeStruct((steps, nb, W_GROUP), F32),
                   jax.ShapeDtypeStruct((nb, POOL_BUF * W_GROUP), F32)],
        compiler_params=_cp("arbitrary"),
        name="pool_sample",
    )(zs, buf, wbd, scale)


def _rwkv_sample_body(z_ref, sh_ref, s_ref, mu_ref, w0_ref, w2_ref, a0_ref, a2_ref, g2_ref, kk_ref, ka_ref,
                      rk_ref, lg_ref, lb_ref, y_ref, so_ref, sho_ref,
                      st_s, r_s, k_s, v_s, kk_s, b_s, w_s, g_s, o_s, *, steps, nb):
    n_state = N_HEADS * N_C * N_C
    _to_rows(s_ref, st_s, n_state)
    prev = sh_ref[...].T
    for t in range(steps):
        f = z_ref[t].T
        mixed = f + (prev - f) * mu_ref[...]
        prev = f
        k = mixed[256:512]
        low = mixed[768:896]
        w_s[t] = jnp.exp(-_sigmoid(w0_ref[...] + _mm_hi(w2_ref[...], jnp.tanh(low))) * math.exp(-0.5))
        a = _sigmoid(a0_ref[...] + _mm_hi(a2_ref[...], low))
        g_s[t] = _mm_hi(g2_ref[...], _sigmoid(low))
        kk = _per_head(k * kk_ref[...], lambda kh, h: kh * lax.rsqrt(
            jnp.maximum(jnp.sum(kh * kh, axis=0, keepdims=True), 1e-24)))
        r_s[t] = mixed[0:256]
        v_s[t] = mixed[512:768]
        kk_s[t] = kk
        k_s[t] = k * (1.0 + (a - 1.0) * ka_ref[...])
        b_s[t] = kk * a

    def per_value(hv, carry):
        r0 = pl.multiple_of(hv * N_C, N_C)
        hs = pl.ds(pl.multiple_of((hv >> 6) * N_C, N_C), N_C)
        s = st_s[pl.ds(r0, N_C), :]
        for t in range(steps):
            sa = -jnp.sum(s * kk_s[t, hs, :], axis=0, keepdims=True)
            s = s * w_s[t, hs, :] + sa * b_s[t, hs, :] + v_s[t, pl.ds(hv, 1), :] * k_s[t, hs, :]
            o_s[t, pl.ds(hv, 1), :] = jnp.sum(s * r_s[t, hs, :], axis=0, keepdims=True)
        st_s[pl.ds(r0, N_C), :] = s
        return carry

    lax.fori_loop(0, N_HEADS * N_C, per_value, 0)
    for t in range(steps):
        def norm(oh, h):
            mu = jnp.mean(oh, axis=0, keepdims=True)
            xc = oh - mu
            return xc * lax.rsqrt(jnp.mean(xc * xc, axis=0, keepdims=True) + RWKV_LN_EPS)
        on = _per_head(o_s[t], norm) * lg_ref[...] + lb_ref[...]
        rk = r_s[t] * k_s[t] * rk_ref[...]
        v = v_s[t]
        bonus = _per_head(rk, lambda x, h: jnp.sum(x, axis=0, keepdims=True) * v[h * 64:(h + 1) * 64])
        y_ref[t] = ((on + bonus) * g_s[t]).T
    _to_cols(st_s, so_ref, n_state)
    sho_ref[...] = z_ref[steps - 1]


def _rwkv_sample(zs, sh, s, p):
    steps, nb, _ = zs.shape
    n_state = N_HEADS * N_C * N_C
    full = lambda shape: pl.BlockSpec(shape, lambda i: (0,) * len(shape))
    col = full((256, 1))
    lowrank = full((256, 128))
    blk = pltpu.VMEM((steps, 256, nb), F32)
    return pl.pallas_call(
        functools.partial(_rwkv_sample_body, steps=steps, nb=nb),
        grid=(1,),
        in_specs=[pl.BlockSpec((steps, nb, RWKV_W), lambda i: (0, 0, RWKV_OFF // RWKV_W)), full((nb, RWKV_W)),
                  full((nb, n_state)), full((RWKV_W, 1)), col, lowrank, col, lowrank, lowrank, col, col, col, col, col],
        out_specs=[full((steps, nb, W_GROUP)), full((nb, n_state)), full((nb, RWKV_W))],
        out_shape=[jax.ShapeDtypeStruct((steps, nb, W_GROUP), F32), jax.ShapeDtypeStruct((nb, n_state), F32),
                   jax.ShapeDtypeStruct((nb, RWKV_W), F32)],
        scratch_shapes=[pltpu.VMEM((n_state, nb), F32), blk, blk, blk, blk, blk, blk, blk, blk],
        compiler_params=_cp("arbitrary"),
        name="rwkv_sample",
    )(zs, sh, s, p["mu_c"], p["w0_c"], p["w2_t"], p["a0_c"], p["a2_t"], p["g2_t"], p["k_k_c"], p["k_a_c"],
      p["r_k_c"], p["ln_g_c"], p["ln_b_c"])


def _mlstm_sample_body(z_ref, zif_ref, cb_ref, c_ref, n_ref, m_ref, cw_ref, cbias_ref, bif_ref, ng_ref,
                       y_ref, co_ref, no_ref, mo_ref, cbo_ref,
                       ct_s, q_s, k_s, v_s, wf_s, den_s, em_s, o_s, *, steps, nb):
    n_state = N_HEADS * D_D * D_D
    _to_rows(c_ref, ct_s, n_state)
    n = n_ref[...].T
    m = m_ref[...].T[0:N_HEADS]
    hist = [cb_ref[:, j * 512:(j + 1) * 512].T for j in range(CONV_D - 1)]
    for t in range(steps):
        hist = hist + [z_ref[t, :, 0:512].T]
        conv = cbias_ref[...]
        for jj in range(CONV_D):
            conv = conv + cw_ref[:, jj:jj + 1] * hist[jj]
        hist = hist[1:]
        act = _silu(conv)
        q = act[0:256] * D_D ** -0.5
        k = act[256:512]
        gates = zif_ref[t].T + bif_ref[...]
        ig = gates[0:N_HEADS]
        lf = _log_sigmoid(gates[N_HEADS:2 * N_HEADS])
        m_new = jnp.maximum(lf + m, ig)
        wf = _expand_heads(jnp.exp(lf + m - m_new), nb)
        kw = _expand_heads(jnp.exp(ig - m_new), nb) * k
        m = m_new
        n = wf * n + kw
        den = _per_head(q * n, lambda x, h: jnp.broadcast_to(jnp.sum(x, axis=0, keepdims=True), (64, nb)))
        q_s[t] = q
        k_s[t] = kw
        v_s[t] = z_ref[t, :, 512:768].T
        wf_s[t] = wf
        den_s[t] = den
        em_s[t] = _expand_heads(jnp.exp(-m_new), nb)
    o_s[...] = jnp.zeros(o_s.shape, F32)

    def per_key(hd, carry):
        r0 = pl.multiple_of(hd * D_D, D_D)
        e0 = pl.multiple_of((hd >> 6) * D_D, D_D)
        c = ct_s[pl.ds(r0, D_D), :]
        for t in range(steps):
            c = wf_s[t, pl.ds(hd, 1), :] * c + k_s[t, pl.ds(hd, 1), :] * v_s[t, pl.ds(e0, D_D), :]
            o_s[t, pl.ds(e0, D_D), :] += q_s[t, pl.ds(hd, 1), :] * c
        ct_s[pl.ds(r0, D_D), :] = c
        return carry

    lax.fori_loop(0, N_HEADS * D_D, per_key, 0)
    for t in range(steps):
        hh = o_s[t] / jnp.maximum(jnp.abs(den_s[t]), em_s[t])
        hn = _per_head(hh, lambda x, h: x * lax.rsqrt(jnp.mean(x * x, axis=0, keepdims=True) + 1e-6))
        y_ref[t] = (_sigmoid(z_ref[t, :, 768:1024].T) * (hn * ng_ref[...])).T
    _to_cols(ct_s, co_ref, n_state)
    no_ref[...] = n.T
    mo_ref[...] = jnp.concatenate([m, jnp.zeros((8 - N_HEADS, nb), F32)], axis=0)
    for j in range(CONV_D - 1):
        cbo_ref[:, j * 512:(j + 1) * 512] = hist[j].T


def _mlstm_sample(zs, cb, c, n, m_pad, cw_t, cbias_col, bif_col, ng_col):
    steps, nb, _ = zs.shape
    n_state = N_HEADS * D_D * D_D
    full = lambda shape: pl.BlockSpec(shape, lambda i: (0,) * len(shape))
    blk = pltpu.VMEM((steps, 256, nb), F32)
    return pl.pallas_call(
        functools.partial(_mlstm_sample_body, steps=steps, nb=nb),
        grid=(1,),
        in_specs=[pl.BlockSpec((steps, nb, ML_W), lambda i: (0, 0, ML_OFF // ML_W)),
                  pl.BlockSpec((steps, nb, MLIF_W), lambda i: (0, 0, MLIF_OFF // MLIF_W)),
                  full((nb, (CONV_D - 1) * 512)), full((nb, n_state)), full((nb, 256)), full((nb, 128)),
                  full((512, CONV_D)), full((512, 1)), full((128, 1)), full((256, 1))],
        out_specs=[full((steps, nb, W_GROUP)), full((nb, n_state)), full((nb, 256)), full((8, nb)),
                   full((nb, (CONV_D - 1) * 512))],
        out_shape=[jax.ShapeDtypeStruct((steps, nb, W_GROUP), F32), jax.ShapeDtypeStruct((nb, n_state), F32),
                   jax.ShapeDtypeStruct((nb, 256), F32), jax.ShapeDtypeStruct((8, nb), F32),
                   jax.ShapeDtypeStruct((nb, (CONV_D - 1) * 512), F32)],
        scratch_shapes=[pltpu.VMEM((n_state, nb), F32), blk, blk, blk, blk, blk, blk, blk],
        compiler_params=_cp("arbitrary"),
        name="mlstm_sample",
    )(zs, zs, cb, c, n, m_pad, cw_t, cbias_col, bif_col, ng_col)


def _pad_cols(x, n):
    return jnp.pad(x, [(0, 0)] * (x.ndim - 1) + [(0, n - x.shape[-1])])


def _rows_at(x, row0, n_rows):
    return jnp.pad(x, ((row0, n_rows - row0 - x.shape[0]), (0, 0)))


def _layer_params(l, w):
    w_in = w["w_in"][l]
    gla, pool, rwkv, ml = jnp.split(w_in, [784, 784 + 256, 784 + 256 + 832], axis=1)
    w_in_p = jnp.concatenate([_pad_cols(gla, GLA_W), _pad_cols(rwkv, RWKV_W), pool, ml[:, :1024],
                              _pad_cols(ml[:, 1024:], MLIF_W)], axis=1).astype(BF16)
    col = lambda v: v.reshape(-1, 1)
    row = lambda v: v.reshape(1, -1)
    wa = _rows_at(w["gla_w_a2"][l], 0, 128)
    w2 = _rows_at(w["rwkv_w2"][l], 0, 128)
    a2 = _rows_at(w["rwkv_a2"][l], R_W, 128)
    g2 = _rows_at(w["rwkv_g2"][l], R_W + R_AA, 128)
    mu = _pad_cols(w["rwkv_mu"][l], RWKV_W)
    bif = _pad_cols(jnp.concatenate([w["mlstm_b_i"][l], w["mlstm_b_f"][l]]), 128)
    pool_w = w["pool_w"][l]
    wbd = jnp.zeros((256, 256), F32)
    for gi in range(4):
        wbd = wbd.at[gi * 64:(gi + 1) * 64, gi * 64:(gi + 1) * 64].set(pool_w[gi])
    rw = dict(mu=row(mu), w0=row(w["rwkv_w0"][l]), w2=w2, a0=row(w["rwkv_a0"][l]), a2=a2, g2=g2,
              k_k=row(w["rwkv_k_k"][l]), k_a=row(w["rwkv_k_a"][l]), r_k=row(w["rwkv_r_k"][l]),
              ln_g=row(w["rwkv_ln_g"][l]), ln_b=row(w["rwkv_ln_b"][l]),
              mu_c=col(mu), w0_c=col(w["rwkv_w0"][l]), w2_t=w2.T, a0_c=col(w["rwkv_a0"][l]), a2_t=a2.T, g2_t=g2.T,
              k_k_c=col(w["rwkv_k_k"][l]), k_a_c=col(w["rwkv_k_a"][l]), r_k_c=col(w["rwkv_r_k"][l]),
              ln_g_c=col(w["rwkv_ln_g"][l]), ln_b_c=col(w["rwkv_ln_b"][l]))
    return dict(
        w_in=w_in_p, w_out=w["w_out"][l].astype(BF16), ln1_g=row(w["ln1_g"][l]), ln1_b=row(w["ln1_b"][l]),
        w_up=w["ffn_w_up"][l].astype(BF16), ffn_cw=w["ffn_conv_w"][l], ffn_cb=row(w["ffn_conv_b"][l]),
        w_down=w["ffn_w_down"][l].astype(BF16), ln2_g=row(w["ln2_g"][l]), ln2_b=row(w["ln2_b"][l]),
        gla_wa=wa, gla_wa_t=wa.T, gla_ba=row(w["gla_b_a"][l]), gla_ba_c=col(w["gla_b_a"][l]),
        gla_ng=row(w["gla_norm_g"][l]), gla_ng_c=col(w["gla_norm_g"][l]),
        pool_w=wbd, pool_scale=row(w["pool_scale"][l]), rwkv=rw,
        ml_cw=w["mlstm_conv_w"][l], ml_cw_t=w["mlstm_conv_w"][l].T, ml_cb=row(w["mlstm_conv_b"][l]),
        ml_cb_c=col(w["mlstm_conv_b"][l]), ml_bif=row(bif), ml_bif_c=col(bif),
        ml_ng=row(w["mlstm_norm_g"][l]), ml_ng_c=col(w["mlstm_norm_g"][l]))


PROMPT_BLOCK = 256
DENSE_TILE = 512
FFN_TILE = 256


def _prompt_layer(x, p):
    bsz, seq, d = x.shape
    z = _linear(x.reshape(bsz * seq, d), p["w_in"], DENSE_TILE).reshape(bsz, seq, ZC)
    zeros = lambda *s: jnp.zeros((bsz,) + s, F32)
    y_a, s_gla = _gla_prompt(z, zeros(N_HEADS, DK_A, DV_A), p["gla_wa"], p["gla_ba"], p["gla_ng"], PROMPT_BLOCK)
    y_b, s_pool = _pool_prompt(z, zeros(POOL_BUF, W_GROUP), p["pool_w"], p["pool_scale"], PROMPT_BLOCK, 0)
    y_c, s_rwkv, s_shift = _rwkv_prompt(z, zeros(1, RWKV_W), zeros(N_HEADS, N_C, N_C), p["rwkv"], PROMPT_BLOCK)
    y_d, s_c, s_n, s_m, s_conv = _mlstm_prompt(z, zeros(CONV_D - 1, 512), zeros(N_HEADS, D_D, D_D), zeros(1, 256),
                                               zeros(1, 256), p["ml_cw"], p["ml_cb"], p["ml_bif"], p["ml_ng"],
                                               PROMPT_BLOCK)
    flat = lambda y: y.reshape(bsz * seq, W_GROUP)
    x1 = _outproj_ln([flat(y_a), flat(y_b), flat(y_c), flat(y_d)], x.reshape(bsz * seq, d), p["w_out"],
                     p["ln1_g"], p["ln1_b"], DENSE_TILE)
    x2, s_ffn = _ffn_prompt(x1.reshape(bsz, seq, d), zeros(FFN_CONV - 1, 2 * D_FF), p["w_up"], p["ffn_cw"],
                            p["ffn_cb"], p["w_down"], p["ln2_g"], p["ln2_b"], FFN_TILE)
    states = (s_gla, s_pool, s_rwkv, s_shift[:, :, :RWKV_COLS], s_c, s_n.reshape(bsz, N_HEADS, D_D),
              s_m[:, 0, ::D_D], s_conv, s_ffn)
    return x2, states


def _sample_layer(x, st, p, steps, nb):
    s_gla, s_pool, s_rwkv, s_shift, s_c, s_n, s_m, s_conv, s_ffn = st
    d = x.shape[1]
    z = _linear(x, p["w_in"], steps * nb).reshape(steps, nb, ZC)
    y_a, n_gla = _gla_sample(z, s_gla.reshape(nb, -1), p["gla_wa_t"], p["gla_ba_c"], p["gla_ng_c"])
    y_b, n_pool = _pool_sample(z, s_pool.reshape(nb, -1), p["pool_w"], p["pool_scale"], PAST_LEN)
    y_c, n_rwkv, n_shift = _rwkv_sample(z, _pad_cols(s_shift.reshape(nb, -1), RWKV_W), s_rwkv.reshape(nb, -1),
                                        p["rwkv"])
    y_d, n_c, n_n, n_m, n_conv = _mlstm_sample(z, s_conv.reshape(nb, -1), s_c.reshape(nb, -1), s_n.reshape(nb, -1),
                                               _pad_cols(s_m, 128), p["ml_cw_t"], p["ml_cb_c"], p["ml_bif_c"],
                                               p["ml_ng_c"])
    flat = lambda y: y.reshape(steps * nb, W_GROUP)
    x1 = _outproj_ln([flat(y_a), flat(y_b), flat(y_c), flat(y_d)], x, p["w_out"], p["ln1_g"], p["ln1_b"],
                     steps * nb)
    x2, n_ffn = _ffn_sample(x1, s_ffn.reshape(nb, -1), p["w_up"], p["ffn_cw"], p["ffn_cb"], p["w_down"],
                            p["ln2_g"], p["ln2_b"], nb, steps)
    states = (n_gla.reshape(s_gla.shape), n_pool.reshape(s_pool.shape), n_rwkv.reshape(s_rwkv.shape),
              n_shift[:, :RWKV_COLS].reshape(s_shift.shape), n_c.reshape(s_c.shape), n_n.reshape(s_n.shape),
              n_m[0:N_HEADS].T, n_conv.reshape(s_conv.shape), n_ffn.reshape(s_ffn.shape))
    return x2, states


def kernel(x_prompt, x_sample, state_gla, state_pool, state_rwkv, state_rwkv_shift, state_mlstm_c, state_mlstm_n, state_mlstm_m, state_mlstm_conv, state_ffn_conv, w_in, gla_w_a2, gla_b_a, gla_norm_g, pool_w, pool_scale, rwkv_mu, rwkv_w0, rwkv_w2, rwkv_a0, rwkv_a2, rwkv_g2, rwkv_k_k, rwkv_k_a, rwkv_r_k, rwkv_ln_g, rwkv_ln_b, mlstm_conv_w, mlstm_conv_b, mlstm_b_i, mlstm_b_f, mlstm_norm_g, w_out, ln1_g, ln1_b, ffn_w_up, ffn_conv_w, ffn_conv_b, ffn_w_down, ln2_g, ln2_b):
    w = dict(w_in=w_in, gla_w_a2=gla_w_a2, gla_b_a=gla_b_a, gla_norm_g=gla_norm_g, pool_w=pool_w,
             pool_scale=pool_scale, rwkv_mu=rwkv_mu, rwkv_w0=rwkv_w0, rwkv_w2=rwkv_w2, rwkv_a0=rwkv_a0,
             rwkv_a2=rwkv_a2, rwkv_g2=rwkv_g2, rwkv_k_k=rwkv_k_k, rwkv_k_a=rwkv_k_a, rwkv_r_k=rwkv_r_k,
             rwkv_ln_g=rwkv_ln_g, rwkv_ln_b=rwkv_ln_b, mlstm_conv_w=mlstm_conv_w, mlstm_conv_b=mlstm_conv_b,
             mlstm_b_i=mlstm_b_i, mlstm_b_f=mlstm_b_f, mlstm_norm_g=mlstm_norm_g, w_out=w_out, ln1_g=ln1_g,
             ln1_b=ln1_b, ffn_w_up=ffn_w_up, ffn_conv_w=ffn_conv_w, ffn_conv_b=ffn_conv_b, ffn_w_down=ffn_w_down,
             ln2_g=ln2_g, ln2_b=ln2_b)
    sample_states = (state_gla, state_pool, state_rwkv, state_rwkv_shift, state_mlstm_c, state_mlstm_n,
                     state_mlstm_m, state_mlstm_conv, state_ffn_conv)
    nb, steps, d = x_sample.shape
    yp = x_prompt
    ys = x_sample.transpose(1, 0, 2).reshape(steps * nb, d)
    acc_p = [[] for _ in sample_states]
    acc_s = [[] for _ in sample_states]
    for l in range(DEPTH):
        p = _layer_params(l, w)
        yp, st_p = _prompt_layer(yp, p)
        ys, st_s = _sample_layer(ys, tuple(s[l] for s in sample_states), p, steps, nb)
        for i in range(len(sample_states)):
            acc_p[i].append(st_p[i])
            acc_s[i].append(st_s[i])
    ys = ys.reshape(steps, nb, d).transpose(1, 0, 2)
    out = [yp, ys]
    for sp, ss in zip(acc_p, acc_s):
        out.append(jnp.stack(sp))
        out.append(jnp.stack(ss))
    return tuple(out)
```

```python
import functools
import math

import jax
import jax.numpy as jnp
from jax import lax
from jax.experimental import pallas as pl
from jax.experimental.pallas import tpu as pltpu

F32 = jnp.float32
BF16 = jnp.bfloat16
HI = lax.Precision.HIGHEST

D_MODEL = 1024
DEPTH = 4
PAST_LEN = 16384
W_GROUP = 256
N_HEADS = 4
DK_A = 32
DV_A = 64
R_GLA = 16
GLA_TAU = 16.0
POOL_WINDOWS = (2, 4, 8, 16)
POOL_BUF = 15
N_C = 64
R_W, R_AA, R_G = 16, 16, 32
RWKV_COLS = 832
RWKV_LN_EPS = 64e-5
D_D = 64
CONV_D = 4
D_FF = 2816
FFN_CONV = 3
ALPHA = (2 * DEPTH) ** 0.25
CHUNK = 64

GLA_W = 896
RWKV_W = 896
GLA_OFF, RWKV_OFF, POOL_OFF, ML_OFF, MLIF_OFF = 0, 896, 1792, 2048, 3072
ZC = 3200
ML_W = 1024
MLIF_W = 128

VMEM_LIMIT = 56 * 1024 * 1024


def _cp(*sem):
    return pltpu.CompilerParams(dimension_semantics=sem, vmem_limit_bytes=VMEM_LIMIT)


def _mm(a, b):
    return jnp.dot(a.astype(BF16), b.astype(BF16), preferred_element_type=F32)


def _mm_nt(a, b):
    return lax.dot_general(a.astype(BF16), b.astype(BF16), (((1,), (1,)), ((), ())), preferred_element_type=F32)


def _mm_tn(a, b):
    return lax.dot_general(a.astype(BF16), b.astype(BF16), (((0,), (0,)), ((), ())), preferred_element_type=F32)


def _mm_hi(a, b):
    return jnp.dot(a, b, preferred_element_type=F32, precision=HI)


def _split2(x):
    hi = x.astype(BF16)
    return hi, (x - hi.astype(F32)).astype(BF16)


def _mm_xc(x, c2):
    hi, mid = _split2(x)
    return jnp.dot(jnp.concatenate([hi, mid], axis=1), c2, preferred_element_type=F32)


def _mm_cx(c2, x):
    hi, mid = _split2(x)
    return jnp.dot(c2, jnp.concatenate([hi, mid], axis=0), preferred_element_type=F32)


def _twice(c, axis):
    c = c.astype(F32).astype(BF16)
    return jnp.concatenate([c, c], axis=axis)


def _iota(shape, dim):
    return lax.broadcasted_iota(jnp.int32, shape, dim)


def _sigmoid(x):
    return 1.0 / (1.0 + jnp.exp(-x))


def _silu(x):
    return x * _sigmoid(x)


def _log_sigmoid(x):
    return jnp.minimum(x, 0.0) - jnp.log(1.0 + jnp.exp(-jnp.abs(x)))


def _tile4(x):
    return jnp.concatenate([x, x, x, x], axis=0)


def _block_diag_mask(rows, cols, rshift, cshift):
    return (_iota((rows, cols), 0) >> rshift) == (_iota((rows, cols), 1) >> cshift)


def _layer_norm(h, g, b, eps):
    mu = jnp.mean(h, axis=-1, keepdims=True)
    xc = h - mu
    var = jnp.mean(xc * xc, axis=-1, keepdims=True)
    return xc * lax.rsqrt(var + eps) * g + b


class _Layer:
    def __init__(self, arr, l):
        self.arr, self.l, self.shape = arr, l, tuple(arr.shape[1:])


def _layer_spec(p):
    n, l = len(p.shape), p.l
    return pl.BlockSpec((None,) + p.shape, lambda *g: (l,) + (0,) * n)


def _linear_body(x_ref, w_ref, o_ref):
    o_ref[...] = jnp.dot(x_ref[...].astype(BF16), w_ref[...], preferred_element_type=F32)


def _linear(x, w, tm):
    t, k = x.shape
    n = w.shape[1]
    return pl.pallas_call(
        _linear_body,
        grid=(t // tm,),
        in_specs=[pl.BlockSpec((tm, k), lambda i: (i, 0)), _layer_spec(w)],
        out_specs=pl.BlockSpec((tm, n), lambda i: (i, 0)),
        out_shape=jax.ShapeDtypeStruct((t, n), F32),
        compiler_params=_cp("arbitrary"),
        name="in_proj",
    )(x, w.arr)


def _outproj(ys, x, w_ref, g_ref, b_ref):
    y = jnp.concatenate(ys, axis=1).astype(BF16)
    mixed = jnp.dot(y, w_ref[...], preferred_element_type=F32)
    return _layer_norm(ALPHA * x + mixed, g_ref[...], b_ref[...], 1e-5)


FF_CHUNK = 256
N_FF_CHUNKS = D_FF // FF_CHUNK
UP_AHEAD = 2


def _ffn_prompt_body(ya_ref, yb_ref, yc_ref, yd_ref, x_ref, buf_ref, wo_ref, g1_ref, b1_ref,
                     wup_ref, cw_ref, cb_ref, wdn_ref, g_ref, b_ref,
                     o_ref, st_ref, carry_ref, *, tm, n_tiles):
    j = pl.program_id(1)

    @pl.when(j == 0)
    def _():
        carry_ref[...] = buf_ref[0]

    x = _outproj([ya_ref[0], yb_ref[0], yc_ref[0], yd_ref[0]], x_ref[0], wo_ref, g1_ref, b1_ref)
    xb = x.astype(BF16)
    row = _iota((8, FF_CHUNK), 0)
    col = lambda c, half: slice(half * D_FF + c * FF_CHUNK, half * D_FF + (c + 1) * FF_CHUNK)
    up = lambda c: [jnp.dot(xb, wup_ref[:, col(c, half)], preferred_element_type=F32) for half in range(2)]
    f = None
    ahead = [up(c) for c in range(UP_AHEAD)]
    for c in range(N_FF_CHUNKS):
        if c + UP_AHEAD < N_FF_CHUNKS:
            ahead.append(up(c + UP_AHEAD))
        u_now = ahead.pop(0)
        acts = []
        for half in range(2):
            sl = col(c, half)
            u = u_now[half]
            two_back, one_back = carry_ref[0:1, sl], carry_ref[1:2, sl]
            w0, w1, w2 = cw_ref[0:1, sl], cw_ref[1:2, sl], cw_ref[2:3, sl]
            r1 = pltpu.roll(u, 1, 0)
            r2 = pltpu.roll(u, 2, 0)
            r1_head = jnp.where(row == 0, one_back, r1[0:8])
            r2_head = jnp.where(row == 0, two_back, jnp.where(row == 1, one_back, r2[0:8]))
            head = cb_ref[:, sl] + w0 * r2_head + w1 * r1_head + w2 * u[0:8]
            body = cb_ref[:, sl] + w0 * r2 + w1 * r1 + w2 * u
            carry_ref[:, sl] = u[tm - 2:tm]
            acts.append(jnp.concatenate([head, body[8:]], axis=0))
        h = (_silu(acts[0]) * acts[1]).astype(BF16)
        part = jnp.dot(h, wdn_ref[c * FF_CHUNK:(c + 1) * FF_CHUNK, :], preferred_element_type=F32)
        f = part if f is None else f + part
    o_ref[0] = _layer_norm(ALPHA * x + f, g_ref[...], b_ref[...], 1e-5)

    @pl.when(j == n_tiles - 1)
    def _():
        st_ref[0] = carry_ref[...]


_DENSE_PARAMS = ("w_out", "ln1_g", "ln1_b", "w_up", "ffn_cw", "ffn_cb", "w_down", "ln2_g", "ln2_b")


def _ffn_prompt(ys, x3, buf, p, tm):
    bsz, seq, d = x3.shape
    n_tiles = seq // tm
    params = [p[k] for k in _DENSE_PARAMS]
    tok = lambda w: pl.BlockSpec((1, tm, w), lambda i, j: (i, j, 0))
    st = pl.BlockSpec((1, 2, 2 * D_FF), lambda i, j: (i, 0, 0))
    return pl.pallas_call(
        functools.partial(_ffn_prompt_body, tm=tm, n_tiles=n_tiles),
        grid=(bsz, n_tiles),
        in_specs=[tok(W_GROUP)] * 4 + [tok(d), st] + [_layer_spec(q) for q in params],
        out_specs=[tok(d), st],
        out_shape=[jax.ShapeDtypeStruct((bsz, seq, d), F32), jax.ShapeDtypeStruct((bsz, 2, 2 * D_FF), F32)],
        scratch_shapes=[pltpu.VMEM((2, 2 * D_FF), F32)],
        compiler_params=_cp("arbitrary", "arbitrary"),
        name="ffn_prompt",
    )(*ys, x3, buf, *[q.arr for q in params])


def _ffn_sample_body(ya_ref, yb_ref, yc_ref, yd_ref, x_ref, buf_ref, wo_ref, g1_ref, b1_ref,
                     wup_ref, cw_ref, cb_ref, wdn_ref, g_ref, b_ref,
                     o_ref, st_ref, h_ref, *, nb, steps):
    t = nb * steps
    x = _outproj([ya_ref[...], yb_ref[...], yc_ref[...], yd_ref[...]], x_ref[...], wo_ref, g1_ref, b1_ref)
    xb = x.astype(BF16)
    for c in range(N_FF_CHUNKS):
        acts = []
        for half in range(2):
            lo = half * D_FF + c * FF_CHUNK
            sl = slice(lo, lo + FF_CHUNK)
            u = jnp.dot(xb, wup_ref[:, sl], preferred_element_type=F32)
            b0 = buf_ref[:, lo:lo + FF_CHUNK]
            b1 = buf_ref[:, 2 * D_FF + lo:2 * D_FF + lo + FF_CHUNK]
            back1 = jnp.concatenate([b1, u[0:t - nb]], axis=0)
            back2 = jnp.concatenate([b0, b1, u[0:t - 2 * nb]], axis=0)
            conv = cb_ref[:, sl] + cw_ref[0:1, sl] * back2 + cw_ref[1:2, sl] * back1 + cw_ref[2:3, sl] * u
            st_ref[:, lo:lo + FF_CHUNK] = u[t - 2 * nb:t - nb]
            st_ref[:, 2 * D_FF + lo:2 * D_FF + lo + FF_CHUNK] = u[t - nb:t]
            acts.append(conv)
        h_ref[:, c * FF_CHUNK:(c + 1) * FF_CHUNK] = (_silu(acts[0]) * acts[1]).astype(BF16)
    f = jnp.dot(h_ref[...], wdn_ref[...], preferred_element_type=F32)
    o_ref[...] = _layer_norm(ALPHA * x + f, g_ref[...], b_ref[...], 1e-5)


def _ffn_sample(ys, x, buf, p, nb, steps):
    t, d = x.shape
    params = [p[k] for k in _DENSE_PARAMS]
    full = lambda shape: pl.BlockSpec(shape, lambda i: (0,) * len(shape))
    return pl.pallas_call(
        functools.partial(_ffn_sample_body, nb=nb, steps=steps),
        grid=(1,),
        in_specs=[full((t, W_GROUP))] * 4 + [full((t, d)), _layer_spec(buf)] + [_layer_spec(q) for q in params],
        out_specs=[full((t, d)), full((nb, 4 * D_FF))],
        out_shape=[jax.ShapeDtypeStruct((t, d), F32), jax.ShapeDtypeStruct((nb, 4 * D_FF), F32)],
        scratch_shapes=[pltpu.VMEM((t, D_FF), BF16)],
        compiler_params=_cp("arbitrary"),
        name="ffn_sample",
    )(*ys, x, buf.arr, *[q.arr for q in params])


def _tril64():
    return (_iota((CHUNK, CHUNK), 0) >= _iota((CHUNK, CHUNK), 1)).astype(F32)


def _exp_masks():
    t = _iota((CHUNK, 256), 0)
    s = _iota((CHUNK, 256), 1) & (CHUNK - 1)
    return t >= s, t > s, t == s


def _bd256():
    return _block_diag_mask(256, 256, 6, 6)


def _tril_blocks(n):
    r = _iota((n, n), 0)
    c = _iota((n, n), 1)
    return ((r >= c) & ((r >> 6) == (c >> 6))).astype(F32)


def _mask_bf16(mask):
    return mask.astype(F32).astype(BF16)


def _bdx(x, mask01):
    return _tile4(x.astype(BF16)) * mask01


def _put_block_diag(dst_ref, blocks, rows, cols):
    dst_ref[...] = jnp.zeros(dst_ref.shape, F32)
    for h in range(N_HEADS):
        dst_ref[h * rows:(h + 1) * rows, h * cols:(h + 1) * cols] = blocks[h]


def _get_block_diag(src_ref, out_ref, rows, cols):
    for h in range(N_HEADS):
        out_ref[0, h] = src_ref[h * rows:(h + 1) * rows, h * cols:(h + 1) * cols]


def _gla_prompt_body(z_ref, s0_ref, wa_ref, ba_ref, ng_ref, y_ref, so_ref, sbd_ref, *, n_chunks, n_blocks):
    j = pl.program_id(1)

    @pl.when(j == 0)
    def _():
        _put_block_diag(sbd_ref, s0_ref[0], DK_A, DV_A)

    lb = n_chunks * CHUNK
    causal, _, _ = _exp_masks()
    bd_state = _block_diag_mask(128, 256, 5, 6)
    bd_k = _mask_bf16(_block_diag_mask(256, 128, 6, 5))
    bd_v = _mask_bf16(_bd256())
    pavg2 = _twice(bd_v.astype(F32) * (1.0 / DV_A), 0)

    lg = _log_sigmoid(_mm(z_ref[0, :, 768:896], wa_ref[...]) + ba_ref[...]) * (1.0 / GLA_TAU)
    bc_all = _mm_cx(_twice(_tril_blocks(lb), 1), lg)

    o_intra, q_dec, incr, decay = [], [], [], []
    for c in range(n_chunks):
        rows = slice(c * CHUNK, (c + 1) * CHUNK)
        q = z_ref[0, rows, 0:128] * DK_A ** -0.5
        k = z_ref[0, rows, 128:256]
        v = z_ref[0, rows, 256:512]
        bc = bc_all[rows]
        bl = bc[CHUNK - 1:CHUNK]
        rho = bc[CHUNK // 2 - 1:CHUNK // 2]
        att = _mm_nt(q * jnp.exp(bc - rho), _bdx(k * jnp.exp(rho - bc), bd_k))
        o_intra.append(_mm(jnp.where(causal, att, 0.0), _bdx(v, bd_v)))
        q_dec.append(q * jnp.exp(bc))
        incr.append(jnp.where(bd_state, _mm_tn(k * jnp.exp(bl - bc), v), 0.0))
        decay_col = jnp.broadcast_to(jnp.exp(bl), (128, 128)).T
        decay.append(jnp.concatenate([decay_col, decay_col], axis=1))
    s = sbd_ref[...]
    o = []
    for c in range(n_chunks):
        o.append(_mm(q_dec[c], s) + o_intra[c])
        s = decay[c] * s + incr[c]
    sbd_ref[...] = s
    o = jnp.concatenate(o, axis=0)
    ms = _mm_xc(o * o, pavg2)
    y_ref[0] = o * lax.rsqrt(ms + 1e-6) * ng_ref[...] * _silu(z_ref[0, :, 512:768])

    @pl.when(j == n_blocks - 1)
    def _():
        _get_block_diag(sbd_ref, so_ref, DK_A, DV_A)


def _gla_prompt(z3, s0, wa, ba, ng, lb):
    bsz, seq, _ = z3.shape
    n_blocks = seq // lb
    st = pl.BlockSpec((1, N_HEADS, DK_A, DV_A), lambda i, j: (i, 0, 0, 0))
    return pl.pallas_call(
        functools.partial(_gla_prompt_body, n_chunks=lb // CHUNK, n_blocks=n_blocks),
        grid=(bsz, n_blocks),
        in_specs=[pl.BlockSpec((1, lb, GLA_W), lambda i, j: (i, j, GLA_OFF // GLA_W)), st,
                  _layer_spec(wa), _layer_spec(ba), _layer_spec(ng)],
        out_specs=[pl.BlockSpec((1, lb, W_GROUP), lambda i, j: (i, j, 0)), st],
        out_shape=[jax.ShapeDtypeStruct((bsz, seq, W_GROUP), F32),
                   jax.ShapeDtypeStruct((bsz, N_HEADS, DK_A, DV_A), F32)],
        scratch_shapes=[pltpu.VMEM((128, 256), F32)],
        compiler_params=_cp("arbitrary", "arbitrary"),
        name="gla_prompt",
    )(z3, s0, wa.arr, ba.arr, ng.arr)


def _pool_prompt_body(z_ref, buf_ref, w_ref, sc_ref, y_ref, bo_ref, hist_ref, *, lb, n_blocks, start_pos):
    j = pl.program_id(1)

    @pl.when(j == 0)
    def _():
        hist_ref[0:1, :] = jnp.zeros((1, W_GROUP), F32)
        hist_ref[1:16, :] = buf_ref[0]

    zp = z_ref[0]
    e = jnp.concatenate([hist_ref[...], zp], axis=0)
    s2 = e[1:] + e[:-1]
    s4 = s2[2:] + s2[:-2]
    s8 = s4[4:] + s4[:-4]
    s16 = s8[8:] + s8[:-8]
    lane = _iota((lb, W_GROUP), 1)
    win = jnp.where(lane < 64, s2[15:], jnp.where(lane < 128, s4[13:], jnp.where(lane < 192, s8[9:], s16[1:])))
    width = jnp.where(lane < 64, 2, jnp.where(lane < 128, 4, jnp.where(lane < 192, 8, 16)))
    pos = start_pos + j * lb + _iota((lb, W_GROUP), 0)
    cnt = jnp.minimum(pos + 1, width).astype(F32)
    y_ref[0] = _mm(win / cnt - zp, w_ref[...]) * sc_ref[...]
    hist_ref[1:16, :] = e[lb + 1:lb + 16]

    @pl.when(j == n_blocks - 1)
    def _():
        bo_ref[0] = hist_ref[1:16, :]


def _pool_prompt(z3, buf, wbd, scale, lb, start_pos):
    bsz, seq, _ = z3.shape
    n_blocks = seq // lb
    st = pl.BlockSpec((1, POOL_BUF, W_GROUP), lambda i, j: (i, 0, 0))
    return pl.pallas_call(
        functools.partial(_pool_prompt_body, lb=lb, n_blocks=n_blocks, start_pos=start_pos),
        grid=(bsz, n_blocks),
        in_specs=[pl.BlockSpec((1, lb, W_GROUP), lambda i, j: (i, j, POOL_OFF // W_GROUP)), st,
                  _layer_spec(wbd), _layer_spec(scale)],
        out_specs=[pl.BlockSpec((1, lb, W_GROUP), lambda i, j: (i, j, 0)), st],
        out_shape=[jax.ShapeDtypeStruct((bsz, seq, W_GROUP), F32),
                   jax.ShapeDtypeStruct((bsz, POOL_BUF, W_GROUP), F32)],
        scratch_shapes=[pltpu.VMEM((16, W_GROUP), F32)],
        compiler_params=_cp("arbitrary", "arbitrary"),
        name="pool_prompt",
    )(z3, buf, wbd.arr, scale.arr)


def _rwkv_prompt_body(z_ref, sh0_ref, s0_ref, mu_ref, w0_ref, w2_ref, a0_ref, a2_ref, g2_ref, kk_ref, ka_ref,
                      rk_ref, lg_ref, lb_ref, y_ref, so_ref, sho_ref,
                      nbd_ref, carry_ref, r_s, k_s, v_s, kk_s, b_s, lw_s, g_s, *, n_chunks, n_blocks, lb):
    j = pl.program_id(1)

    @pl.when(j == 0)
    def _():
        _put_block_diag(nbd_ref, s0_ref[0], N_C, N_C)
        carry_ref[...] = sh0_ref[0]

    bd = _bd256()
    bd_m = _mask_bf16(bd)
    ones2 = _twice(bd, 0)
    pavg2 = _twice(bd.astype(F32) * (1.0 / N_C), 0)

    f = z_ref[0]
    prev = jnp.concatenate([carry_ref[...], f[:-1]], axis=0)
    mixed = f + (prev - f) * mu_ref[...]
    carry_ref[...] = f[lb - 1:lb]
    r = mixed[:, 0:256]
    k = mixed[:, 256:512]
    v = mixed[:, 512:768]
    low = mixed[:, 768:896]
    lw_all = -_sigmoid(w0_ref[...] + _mm(jnp.tanh(low), w2_ref[...])) * math.exp(-0.5)
    lw_s[...] = lw_all
    a = _sigmoid(a0_ref[...] + _mm(low, a2_ref[...]))
    g_s[...] = _mm(_sigmoid(low), g2_ref[...])
    kk = k * kk_ref[...]
    kk = kk * lax.rsqrt(jnp.maximum(_mm_xc(kk * kk, ones2), 1e-24))
    r_s[...] = r
    v_s[...] = v
    kk_s[...] = kk
    k_s[...] = k * (1.0 + (a - 1.0) * ka_ref[...])
    b_s[...] = kk * a
    gc_all = _mm_cx(_twice(_tril_blocks(lb), 1), lw_all)

    lower, strict, eye_exp = _exp_masks()
    eye_f = eye_exp.astype(F32)

    cs = range(n_chunks)
    sl = [slice(c * CHUNK, (c + 1) * CHUNK) for c in cs]
    gc = [gc_all[s] for s in sl]
    kt = [kk_s[sl[c], :] * jnp.exp(gc[c] - lw_s[sl[c], :]) for c in cs]
    rt = [r_s[sl[c], :] * jnp.exp(gc[c]) for c in cs]
    a_kk, a_kb, a_rk, a_rb = [], [], [], []
    for c in cs:
        einv = jnp.exp(-gc[c])
        lhs = jnp.concatenate([kt[c], rt[c]], axis=0)
        rhs = jnp.concatenate([_bdx(k_s[sl[c], :] * einv, bd_m), _bdx(b_s[sl[c], :] * einv, bd_m)], axis=0)
        aa = _mm_nt(lhs, rhs)
        a_kk.append(jnp.where(strict, aa[0:CHUNK, 0:256], 0.0))
        a_kb.append(jnp.where(strict, aa[0:CHUNK, 256:512], 0.0))
        a_rk.append(jnp.where(lower, aa[CHUNK:, 0:256], 0.0))
        a_rb.append(jnp.where(lower, aa[CHUNK:, 256:512], 0.0))
    t = [eye_f - a_kb[c] for c in cs]
    p = [_mm(a_kb[c], _bdx(a_kb[c], bd_m)) for c in cs]
    for _ in range(4):
        m = [_mm(jnp.concatenate([p[c], t[c]], axis=0), _bdx(p[c], bd_m)) for c in cs]
        p = [m[c][0:CHUNK] for c in cs]
        t = [t[c] + m[c][CHUNK:] for c in cs]
    t = [t[c] + _mm(t[c], _bdx(p[c], bd_m)) for c in cs]
    av = [_mm(jnp.concatenate([a_kk[c], a_rk[c]], axis=0), _bdx(v_s[sl[c], :], bd_m)) for c in cs]
    tu = [_mm(t[c], jnp.concatenate([_bdx(av[c][0:CHUNK], bd_m), _bdx(kt[c], bd_m)], axis=1)) for c in cs]
    u0 = [tu[c][:, 0:256] for c in cs]
    tk = [tu[c][:, 256:512] for c in cs]
    ro = [_mm(a_rb[c], jnp.concatenate([_bdx(tk[c], bd_m), _bdx(u0[c], bd_m)], axis=1)) for c in cs]
    r_eff = [rt[c] - ro[c][:, 0:256] for c in cs]
    o0 = [av[c][CHUNK:] - ro[c][:, 256:512] for c in cs]
    a_st, b_st, decay = [], [], []
    for c in cs:
        gl = gc[c][CHUNK - 1:CHUNK]
        dk = jnp.exp(gl - gc[c])
        kd = k_s[sl[c], :] * dk
        bdk = b_s[sl[c], :] * dk
        a_st.append(jnp.where(bd, -_mm_tn(tk[c], bdk), 0.0))
        b_st.append(jnp.where(bd, _mm_tn(jnp.concatenate([v_s[sl[c], :], -u0[c]], axis=0),
                                         jnp.concatenate([kd, bdk], axis=0)), 0.0))
        decay.append(jnp.exp(gl))
    n = nbd_ref[...]
    o = []
    for c in cs:
        o.append(_mm_nt(r_eff[c], n) + o0[c])
        n = n * decay[c] + _mm(n, a_st[c]) + b_st[c]
    nbd_ref[...] = n
    o = jnp.concatenate(o, axis=0)
    mu = _mm_xc(o, pavg2)
    xc = o - mu
    var = _mm_xc(xc * xc, pavg2)
    on = xc * lax.rsqrt(var + RWKV_LN_EPS) * lg_ref[...] + lb_ref[...]
    bonus = _mm_xc(r_s[...] * k_s[...] * rk_ref[...], ones2) * v_s[...]
    y_ref[0] = (on + bonus) * g_s[...]

    @pl.when(j == n_blocks - 1)
    def _():
        _get_block_diag(nbd_ref, so_ref, N_C, N_C)
        sho_ref[0] = carry_ref[...]


_RWKV_PARAMS = ("mu", "w0", "w2", "a0", "a2", "g2", "k_k", "k_a", "r_k", "ln_g", "ln_b")


def _rwkv_prompt(z3, sh0, s0, p, lb):
    bsz, seq, _ = z3.shape
    n_blocks = seq // lb
    st = pl.BlockSpec((1, N_HEADS, N_C, N_C), lambda i, j: (i, 0, 0, 0))
    sh = pl.BlockSpec((1, 1, RWKV_W), lambda i, j: (i, 0, 0))
    blk = pltpu.VMEM((lb, 256), F32)
    params = [p[k] for k in _RWKV_PARAMS]
    return pl.pallas_call(
        functools.partial(_rwkv_prompt_body, n_chunks=lb // CHUNK, n_blocks=n_blocks, lb=lb),
        grid=(bsz, n_blocks),
        in_specs=[pl.BlockSpec((1, lb, RWKV_W), lambda i, j: (i, j, RWKV_OFF // RWKV_W)), sh, st]
        + [_layer_spec(q) for q in params],
        out_specs=[pl.BlockSpec((1, lb, W_GROUP), lambda i, j: (i, j, 0)), st, sh],
        out_shape=[jax.ShapeDtypeStruct((bsz, seq, W_GROUP), F32),
                   jax.ShapeDtypeStruct((bsz, N_HEADS, N_C, N_C), F32),
                   jax.ShapeDtypeStruct((bsz, 1, RWKV_W), F32)],
        scratch_shapes=[pltpu.VMEM((256, 256), F32), pltpu.VMEM((1, RWKV_W), F32),
                        blk, blk, blk, blk, blk, blk, blk],
        compiler_params=_cp("arbitrary", "arbitrary"),
        name="rwkv_prompt",
    )(z3, sh0, s0, *[q.arr for q in params])


def _mlstm_prompt_body(z_ref, zif_ref, cb0_ref, c0_ref, n0_ref, m0_ref, cw_ref, cbias_ref, bif_ref, ng_ref,
                       y_ref, co_ref, no_ref, mo_ref, cbo_ref,
                       cbd_ref, n_ref, m_ref, carry_ref, qk_s, *, n_chunks, n_blocks, lb):
    j = pl.program_id(1)

    @pl.when(j == 0)
    def _():
        _put_block_diag(cbd_ref, c0_ref[0], D_D, D_D)
        n_ref[...] = n0_ref[0]
        m_ref[...] = m0_ref[0]
        carry_ref[...] = cb0_ref[0]

    qk = z_ref[0, :, 0:512]
    ext = jnp.concatenate([carry_ref[...], qk], axis=0)
    conv = cbias_ref[...]
    for jj in range(CONV_D):
        conv = conv + cw_ref[jj:jj + 1, :] * ext[jj:jj + lb]
    qk_s[...] = _silu(conv)
    carry_ref[...] = ext[lb:lb + 3]

    bd = _bd256()
    bd_m = _mask_bf16(bd)
    ones2 = _twice(bd, 0)
    pavg2 = _twice(bd.astype(F32) * (1.0 / D_D), 0)
    causal, _, eye_exp = _exp_masks()
    er = _iota((128, 512), 0)
    ec = _iota((128, 512), 1)
    expand2 = _twice(((ec < 256) & (er == (ec >> 6))) | ((ec >= 256) & (er == 4 + ((ec - 256) >> 6))), 0)
    is_f = (_iota((lb, 128), 1) >= 4) & (_iota((lb, 128), 1) < 8)
    lane = _iota((CHUNK, 256), 1)

    gates = zif_ref[0] + bif_ref[...]
    gates = jnp.where(is_f, _log_sigmoid(gates), gates)
    gates = jnp.where(is_f, _mm_cx(_twice(_tril_blocks(lb), 1), gates), gates)
    ge_all = _mm_xc(gates, expand2)

    for c in range(n_chunks):
        rows = slice(c * CHUNK, (c + 1) * CHUNK)
        q = qk_s[rows, 0:256] * D_D ** -0.5
        k = qk_s[rows, 256:512]
        v = z_ref[0, rows, 512:768]
        og = z_ref[0, rows, 768:1024]
        iexp = ge_all[rows, 0:256]
        fexp = ge_all[rows, 256:512]
        d_row = jnp.sum(jnp.where(eye_exp, iexp - fexp, 0.0), axis=0, keepdims=True)
        dlog = jnp.where(causal, fexp + d_row, -jnp.inf)
        mx = [jnp.max(dlog[:, h * 64:(h + 1) * 64], axis=1, keepdims=True) for h in range(N_HEADS)]
        mx = jnp.where(lane < 64, mx[0], jnp.where(lane < 128, mx[1], jnp.where(lane < 192, mx[2], mx[3])))
        qk_att = _mm_nt(q, _bdx(k, bd_m))
        m_prev = m_ref[...]
        inter = fexp + m_prev
        m_t = jnp.maximum(inter, mx)
        w0 = jnp.exp(inter - m_t)
        s = jnp.exp(dlog - m_t) * qk_att
        cbd = cbd_ref[...]
        n_row = n_ref[...]
        num = w0 * _mm(q, cbd) + _mm(s, _bdx(v, bd_m))
        den = _mm_xc(w0 * (q * n_row) + s, ones2)
        hh = num / jnp.maximum(jnp.abs(den), jnp.exp(-m_t))
        ms = _mm_xc(hh * hh, pavg2)
        y_ref[0, rows, :] = _sigmoid(og) * (hh * lax.rsqrt(ms + 1e-6) * ng_ref[...])
        m_new = m_t[CHUNK - 1:CHUNK]
        f_last = fexp[CHUNK - 1:CHUNK]
        kw = k * jnp.exp(f_last - fexp + iexp - m_new)
        w0f = jnp.exp(f_last + m_prev - m_new)
        cbd_ref[...] = jnp.where(bd, w0f * cbd + _mm_tn(kw, v), 0.0)
        n_ref[...] = w0f * n_row + jnp.sum(kw, axis=0, keepdims=True)
        m_ref[...] = m_new

    @pl.when(j == n_blocks - 1)
    def _():
        _get_block_diag(cbd_ref, co_ref, D_D, D_D)
        no_ref[0] = n_ref[...]
        mo_ref[0] = m_ref[...]
        cbo_ref[0] = carry_ref[...]


def _mlstm_prompt(z3, cb0, c0, n0, m0, cw, cbias, bif, ng, lb):
    bsz, seq, _ = z3.shape
    n_blocks = seq // lb
    cst = pl.BlockSpec((1, N_HEADS, D_D, D_D), lambda i, j: (i, 0, 0, 0))
    row = pl.BlockSpec((1, 1, 256), lambda i, j: (i, 0, 0))
    cbs = pl.BlockSpec((1, CONV_D - 1, 512), lambda i, j: (i, 0, 0))
    return pl.pallas_call(
        functools.partial(_mlstm_prompt_body, n_chunks=lb // CHUNK, n_blocks=n_blocks, lb=lb),
        grid=(bsz, n_blocks),
        in_specs=[pl.BlockSpec((1, lb, ML_W), lambda i, j: (i, j, ML_OFF // ML_W)),
                  pl.BlockSpec((1, lb, MLIF_W), lambda i, j: (i, j, MLIF_OFF // MLIF_W)),
                  cbs, cst, row, row, _layer_spec(cw), _layer_spec(cbias), _layer_spec(bif), _layer_spec(ng)],
        out_specs=[pl.BlockSpec((1, lb, W_GROUP), lambda i, j: (i, j, 0)), cst, row, row, cbs],
        out_shape=[jax.ShapeDtypeStruct((bsz, seq, W_GROUP), F32),
                   jax.ShapeDtypeStruct((bsz, N_HEADS, D_D, D_D), F32),
                   jax.ShapeDtypeStruct((bsz, 1, 256), F32),
                   jax.ShapeDtypeStruct((bsz, 1, 256), F32),
                   jax.ShapeDtypeStruct((bsz, CONV_D - 1, 512), F32)],
        scratch_shapes=[pltpu.VMEM((256, 256), F32), pltpu.VMEM((1, 256), F32), pltpu.VMEM((1, 256), F32),
                        pltpu.VMEM((CONV_D - 1, 512), F32), pltpu.VMEM((lb, 512), F32)],
        compiler_params=_cp("arbitrary", "arbitrary"),
        name="mlstm_prompt",
    )(z3, z3, cb0, c0, n0, m0, cw.arr, cbias.arr, bif.arr, ng.arr)


T_COLS = 1024


def _to_rows(src_ref, dst_ref, n):
    for i in range(n // T_COLS):
        dst_ref[i * T_COLS:(i + 1) * T_COLS, :] = src_ref[:, i * T_COLS:(i + 1) * T_COLS].T


def _to_cols(src_ref, dst_ref, n):
    for i in range(n // T_COLS):
        dst_ref[:, i * T_COLS:(i + 1) * T_COLS] = src_ref[i * T_COLS:(i + 1) * T_COLS, :].T


def _per_head(x, fn):
    return jnp.concatenate([fn(x[h * 64:(h + 1) * 64], h) for h in range(N_HEADS)], axis=0)


def _expand_heads(x, nb):
    return jnp.concatenate([jnp.broadcast_to(x[h:h + 1], (64, nb)) for h in range(N_HEADS)], axis=0)


def _gla_sample_body(z_ref, s_ref, wa_ref, ba_ref, ng_ref, y_ref, so_ref,
                     st_s, dec_s, k_s, q_s, v_s, o_s, *, steps, nb):
    n_state = N_HEADS * DK_A * DV_A
    _to_rows(s_ref, st_s, n_state)
    for t in range(steps):
        zt = z_ref[t].T
        q_s[t] = zt[0:128] * DK_A ** -0.5
        k_s[t] = zt[128:256]
        v_s[t] = zt[256:512]
        lg = _log_sigmoid(_mm_hi(wa_ref[...], zt[768:896]) + ba_ref[...]) * (1.0 / GLA_TAU)
        dec_s[t] = jnp.exp(lg)
    o_s[...] = jnp.zeros(o_s.shape, F32)

    def per_key(hk, carry):
        r0 = pl.multiple_of(hk * DV_A, DV_A)
        v0 = pl.multiple_of((hk >> 5) * DV_A, DV_A)
        s = st_s[pl.ds(r0, DV_A), :]
        for t in range(steps):
            s = dec_s[t, pl.ds(hk, 1), :] * s + k_s[t, pl.ds(hk, 1), :] * v_s[t, pl.ds(v0, DV_A), :]
            o_s[t, pl.ds(v0, DV_A), :] += q_s[t, pl.ds(hk, 1), :] * s
        st_s[pl.ds(r0, DV_A), :] = s
        return carry

    lax.fori_loop(0, N_HEADS * DK_A, per_key, 0)
    for t in range(steps):
        g = z_ref[t, :, 512:768].T
        on = _per_head(o_s[t], lambda oh, h: oh * lax.rsqrt(jnp.mean(oh * oh, axis=0, keepdims=True) + 1e-6))
        y_ref[t] = (on * ng_ref[...] * _silu(g)).T
    _to_cols(st_s, so_ref, n_state)


def _gla_sample(zs, s, wa_t, ba_col, ng_col):
    steps, nb, _ = zs.shape
    n_state = N_HEADS * DK_A * DV_A
    full = lambda shape: pl.BlockSpec(shape, lambda i: (0,) * len(shape))
    return pl.pallas_call(
        functools.partial(_gla_sample_body, steps=steps, nb=nb),
        grid=(1,),
        in_specs=[pl.BlockSpec((steps, nb, GLA_W), lambda i: (0, 0, GLA_OFF // GLA_W)), _layer_spec(s),
                  _layer_spec(wa_t), _layer_spec(ba_col), _layer_spec(ng_col)],
        out_specs=[full((steps, nb, W_GROUP)), full((nb, n_state))],
        out_shape=[jax.ShapeDtypeStruct((steps, nb, W_GROUP), F32), jax.ShapeDtypeStruct((nb, n_state), F32)],
        scratch_shapes=[pltpu.VMEM((n_state, nb), F32), pltpu.VMEM((steps, 128, nb), F32),
                        pltpu.VMEM((steps, 128, nb), F32), pltpu.VMEM((steps, 128, nb), F32),
                        pltpu.VMEM((steps, 256, nb), F32), pltpu.VMEM((steps, 256, nb), F32)],
        compiler_params=_cp("arbitrary"),
        name="gla_sample",
    )(zs, s.arr, wa_t.arr, ba_col.arr, ng_col.arr)


def _pool_sample_body(z_ref, buf_ref, w_ref, sc_ref, y_ref, bo_ref, *, steps, nb, start_pos):
    ext = [buf_ref[:, j * W_GROUP:(j + 1) * W_GROUP] for j in range(POOL_BUF)] + [z_ref[t] for t in range(steps)]
    lane = _iota((nb, W_GROUP), 1)

    def by_group(vals):
        return jnp.where(lane < 64, vals[0], jnp.where(lane < 128, vals[1], jnp.where(lane < 192, vals[2], vals[3])))

    for t in range(steps):
        r = POOL_BUF + t
        acc = ext[r]
        sums = {}
        for back in range(1, max(POOL_WINDOWS)):
            acc = acc + ext[r - back]
            sums[back + 1] = acc
        win = by_group([sums[w] for w in POOL_WINDOWS])
        cnt = by_group([float(min(start_pos + t + 1, w)) for w in POOL_WINDOWS])
        y_ref[t] = _mm(win / cnt - ext[r], w_ref[...]) * sc_ref[...]
    for j in range(POOL_BUF):
        bo_ref[:, j * W_GROUP:(j + 1) * W_GROUP] = ext[steps + j]


def _pool_sample(zs, buf, wbd, scale, start_pos):
    steps, nb, _ = zs.shape
    full = lambda shape: pl.BlockSpec(shape, lambda i: (0,) * len(shape))
    return pl.pallas_call(
        functools.partial(_pool_sample_body, steps=steps, nb=nb, start_pos=start_pos),
        grid=(1,),
        in_specs=[pl.BlockSpec((steps, nb, W_GROUP), lambda i: (0, 0, POOL_OFF // W_GROUP)),
                  _layer_spec(buf), _layer_spec(wbd), _layer_spec(scale)],
        out_specs=[full((steps, nb, W_GROUP)), full((nb, POOL_BUF * W_GROUP))],
        out_shape=[jax.ShapeDtypeStruct((steps, nb, W_GROUP), F32),
                   jax.ShapeDtypeStruct((nb, POOL_BUF * W_GROUP), F32)],
        compiler_params=_cp("arbitrary"),
        name="pool_sample",
    )(zs, buf.arr, wbd.arr, scale.arr)


def _rwkv_sample_body(z_ref, sh_ref, s_ref, mu_ref, w0_ref, w2_ref, a0_ref, a2_ref, g2_ref, kk_ref, ka_ref,
                      rk_ref, lg_ref, lb_ref, y_ref, so_ref, sho_ref,
                      st_s, r_s, k_s, v_s, kk_s, b_s, w_s, g_s, o_s, *, steps, nb):
    n_state = N_HEADS * N_C * N_C
    _to_rows(s_ref, st_s, n_state)
    prev = sh_ref[...].T
    for t in range(steps):
        f = z_ref[t].T
        mixed = f + (prev - f) * mu_ref[...]
        prev = f
        k = mixed[256:512]
        low = mixed[768:896]
        w_s[t] = jnp.exp(-_sigmoid(w0_ref[...] + _mm_hi(w2_ref[...], jnp.tanh(low))) * math.exp(-0.5))
        a = _sigmoid(a0_ref[...] + _mm_hi(a2_ref[...], low))
        g_s[t] = _mm_hi(g2_ref[...], _sigmoid(low))
        kk = _per_head(k * kk_ref[...], lambda kh, h: kh * lax.rsqrt(
            jnp.maximum(jnp.sum(kh * kh, axis=0, keepdims=True), 1e-24)))
        r_s[t] = mixed[0:256]
        v_s[t] = mixed[512:768]
        kk_s[t] = kk
        k_s[t] = k * (1.0 + (a - 1.0) * ka_ref[...])
        b_s[t] = kk * a

    def per_value(hv, carry):
        r0 = pl.multiple_of(hv * N_C, N_C)
        hs = pl.ds(pl.multiple_of((hv >> 6) * N_C, N_C), N_C)
        s = st_s[pl.ds(r0, N_C), :]
        for t in range(steps):
            sa = -jnp.sum(s * kk_s[t, hs, :], axis=0, keepdims=True)
            s = s * w_s[t, hs, :] + sa * b_s[t, hs, :] + v_s[t, pl.ds(hv, 1), :] * k_s[t, hs, :]
            o_s[t, pl.ds(hv, 1), :] = jnp.sum(s * r_s[t, hs, :], axis=0, keepdims=True)
        st_s[pl.ds(r0, N_C), :] = s
        return carry

    lax.fori_loop(0, N_HEADS * N_C, per_value, 0)
    for t in range(steps):
        def norm(oh, h):
            mu = jnp.mean(oh, axis=0, keepdims=True)
            xc = oh - mu
            return xc * lax.rsqrt(jnp.mean(xc * xc, axis=0, keepdims=True) + RWKV_LN_EPS)
        on = _per_head(o_s[t], norm) * lg_ref[...] + lb_ref[...]
        rk = r_s[t] * k_s[t] * rk_ref[...]
        v = v_s[t]
        bonus = _per_head(rk, lambda x, h: jnp.sum(x, axis=0, keepdims=True) * v[h * 64:(h + 1) * 64])
        y_ref[t] = ((on + bonus) * g_s[t]).T
    _to_cols(st_s, so_ref, n_state)
    sho_ref[...] = z_ref[steps - 1]


def _rwkv_sample(zs, sh, s, p):
    steps, nb, _ = zs.shape
    n_state = N_HEADS * N_C * N_C
    full = lambda shape: pl.BlockSpec(shape, lambda i: (0,) * len(shape))
    blk = pltpu.VMEM((steps, 256, nb), F32)
    params = [p[k + "_c"] for k in _RWKV_PARAMS]
    return pl.pallas_call(
        functools.partial(_rwkv_sample_body, steps=steps, nb=nb),
        grid=(1,),
        in_specs=[pl.BlockSpec((steps, nb, RWKV_W), lambda i: (0, 0, RWKV_OFF // RWKV_W)), _layer_spec(sh),
                  _layer_spec(s)] + [_layer_spec(q) for q in params],
        out_specs=[full((steps, nb, W_GROUP)), full((nb, n_state)), full((nb, RWKV_W))],
        out_shape=[jax.ShapeDtypeStruct((steps, nb, W_GROUP), F32), jax.ShapeDtypeStruct((nb, n_state), F32),
                   jax.ShapeDtypeStruct((nb, RWKV_W), F32)],
        scratch_shapes=[pltpu.VMEM((n_state, nb), F32), blk, blk, blk, blk, blk, blk, blk, blk],
        compiler_params=_cp("arbitrary"),
        name="rwkv_sample",
    )(zs, sh.arr, s.arr, *[q.arr for q in params])


def _mlstm_sample_body(z_ref, zif_ref, cb_ref, c_ref, n_ref, m_ref, cw_ref, cbias_ref, bif_ref, ng_ref,
                       y_ref, co_ref, no_ref, mo_ref, cbo_ref,
                       ct_s, q_s, k_s, v_s, wf_s, den_s, em_s, o_s, *, steps, nb):
    n_state = N_HEADS * D_D * D_D
    _to_rows(c_ref, ct_s, n_state)
    n = n_ref[...].T
    m = m_ref[...].T[0:N_HEADS]
    hist = [cb_ref[:, j * 512:(j + 1) * 512].T for j in range(CONV_D - 1)]
    for t in range(steps):
        hist = hist + [z_ref[t, :, 0:512].T]
        conv = cbias_ref[...]
        for jj in range(CONV_D):
            conv = conv + cw_ref[:, jj:jj + 1] * hist[jj]
        hist = hist[1:]
        act = _silu(conv)
        q = act[0:256] * D_D ** -0.5
        k = act[256:512]
        gates = zif_ref[t].T + bif_ref[...]
        ig = gates[0:N_HEADS]
        lf = _log_sigmoid(gates[N_HEADS:2 * N_HEADS])
        m_new = jnp.maximum(lf + m, ig)
        wf = _expand_heads(jnp.exp(lf + m - m_new), nb)
        kw = _expand_heads(jnp.exp(ig - m_new), nb) * k
        m = m_new
        n = wf * n + kw
        den = _per_head(q * n, lambda x, h: jnp.broadcast_to(jnp.sum(x, axis=0, keepdims=True), (64, nb)))
        q_s[t] = q
        k_s[t] = kw
        v_s[t] = z_ref[t, :, 512:768].T
        wf_s[t] = wf
        den_s[t] = den
        em_s[t] = _expand_heads(jnp.exp(-m_new), nb)
    o_s[...] = jnp.zeros(o_s.shape, F32)

    def per_key(hd, carry):
        r0 = pl.multiple_of(hd * D_D, D_D)
        e0 = pl.multiple_of((hd >> 6) * D_D, D_D)
        c = ct_s[pl.ds(r0, D_D), :]
        for t in range(steps):
            c = wf_s[t, pl.ds(hd, 1), :] * c + k_s[t, pl.ds(hd, 1), :] * v_s[t, pl.ds(e0, D_D), :]
            o_s[t, pl.ds(e0, D_D), :] += q_s[t, pl.ds(hd, 1), :] * c
        ct_s[pl.ds(r0, D_D), :] = c
        return carry

    lax.fori_loop(0, N_HEADS * D_D, per_key, 0)
    for t in range(steps):
        hh = o_s[t] / jnp.maximum(jnp.abs(den_s[t]), em_s[t])
        hn = _per_head(hh, lambda x, h: x * lax.rsqrt(jnp.mean(x * x, axis=0, keepdims=True) + 1e-6))
        y_ref[t] = (_sigmoid(z_ref[t, :, 768:1024].T) * (hn * ng_ref[...])).T
    _to_cols(ct_s, co_ref, n_state)
    no_ref[...] = n.T
    mo_ref[...] = jnp.concatenate([m, jnp.zeros((8 - N_HEADS, nb), F32)], axis=0)
    for j in range(CONV_D - 1):
        cbo_ref[:, j * 512:(j + 1) * 512] = hist[j].T


def _mlstm_sample(zs, cb, c, n, m_pad, cw_t, cbias_col, bif_col, ng_col):
    steps, nb, _ = zs.shape
    n_state = N_HEADS * D_D * D_D
    full = lambda shape: pl.BlockSpec(shape, lambda i: (0,) * len(shape))
    blk = pltpu.VMEM((steps, 256, nb), F32)
    return pl.pallas_call(
        functools.partial(_mlstm_sample_body, steps=steps, nb=nb),
        grid=(1,),
        in_specs=[pl.BlockSpec((steps, nb, ML_W), lambda i: (0, 0, ML_OFF // ML_W)),
                  pl.BlockSpec((steps, nb, MLIF_W), lambda i: (0, 0, MLIF_OFF // MLIF_W)),
                  _layer_spec(cb), _layer_spec(c), _layer_spec(n), _layer_spec(m_pad),
                  _layer_spec(cw_t), _layer_spec(cbias_col), _layer_spec(bif_col), _layer_spec(ng_col)],
        out_specs=[full((steps, nb, W_GROUP)), full((nb, n_state)), full((nb, 256)), full((8, nb)),
                   full((nb, (CONV_D - 1) * 512))],
        out_shape=[jax.ShapeDtypeStruct((steps, nb, W_GROUP), F32), jax.ShapeDtypeStruct((nb, n_state), F32),
                   jax.ShapeDtypeStruct((nb, 256), F32), jax.ShapeDtypeStruct((8, nb), F32),
                   jax.ShapeDtypeStruct((nb, (CONV_D - 1) * 512), F32)],
        scratch_shapes=[pltpu.VMEM((n_state, nb), F32), blk, blk, blk, blk, blk, blk, blk],
        compiler_params=_cp("arbitrary"),
        name="mlstm_sample",
    )(zs, zs, cb.arr, c.arr, n.arr, m_pad.arr, cw_t.arr, cbias_col.arr, bif_col.arr, ng_col.arr)


def _pad_cols(x, n):
    return jnp.pad(x, [(0, 0)] * (x.ndim - 1) + [(0, n - x.shape[-1])])


def _rows_at(x, row0, n_rows):
    return jnp.pad(x, ((0, 0), (row0, n_rows - row0 - x.shape[1]), (0, 0)))


def _stacked_params(w):
    gla, pool, rwkv, ml = jnp.split(w["w_in"], [784, 784 + 256, 784 + 256 + 832], axis=2)
    w_in_p = jnp.concatenate([_pad_cols(gla, GLA_W), _pad_cols(rwkv, RWKV_W), pool, ml[:, :, :1024],
                              _pad_cols(ml[:, :, 1024:], MLIF_W)], axis=2).astype(BF16)
    col = lambda v: v.reshape(DEPTH, -1, 1)
    row = lambda v: v.reshape(DEPTH, 1, -1)
    tr = lambda m: m.transpose(0, 2, 1)
    wa = _rows_at(w["gla_w_a2"], 0, 128)
    w2 = _rows_at(w["rwkv_w2"], 0, 128)
    a2 = _rows_at(w["rwkv_a2"], R_W, 128)
    g2 = _rows_at(w["rwkv_g2"], R_W + R_AA, 128)
    mu = _pad_cols(w["rwkv_mu"], RWKV_W)
    bif = _pad_cols(jnp.concatenate([w["mlstm_b_i"], w["mlstm_b_f"]], axis=1), 128)
    wbd = jnp.zeros((DEPTH, 256, 256), F32)
    for gi in range(4):
        wbd = wbd.at[:, gi * 64:(gi + 1) * 64, gi * 64:(gi + 1) * 64].set(w["pool_w"][:, gi])
    vecs = dict(w0=w["rwkv_w0"], a0=w["rwkv_a0"], k_k=w["rwkv_k_k"], k_a=w["rwkv_k_a"], r_k=w["rwkv_r_k"],
                ln_g=w["rwkv_ln_g"], ln_b=w["rwkv_ln_b"], mu=mu)
    rw = {k: row(v) for k, v in vecs.items()}
    rw.update({k + "_c": col(v) for k, v in vecs.items()})
    rw.update(w2=w2, a2=a2, g2=g2, w2_c=tr(w2), a2_c=tr(a2), g2_c=tr(g2))
    return dict(
        w_in=w_in_p, w_out=w["w_out"].astype(BF16), ln1_g=row(w["ln1_g"]), ln1_b=row(w["ln1_b"]),
        w_up=w["ffn_w_up"].astype(BF16), ffn_cw=w["ffn_conv_w"], ffn_cb=row(w["ffn_conv_b"]),
        w_down=w["ffn_w_down"].astype(BF16), ln2_g=row(w["ln2_g"]), ln2_b=row(w["ln2_b"]),
        gla_wa=wa, gla_wa_t=tr(wa), gla_ba=row(w["gla_b_a"]), gla_ba_c=col(w["gla_b_a"]),
        gla_ng=row(w["gla_norm_g"]), gla_ng_c=col(w["gla_norm_g"]),
        pool_w=wbd, pool_scale=row(w["pool_scale"]), rwkv=rw,
        ml_cw=w["mlstm_conv_w"], ml_cw_t=tr(w["mlstm_conv_w"]), ml_cb=row(w["mlstm_conv_b"]),
        ml_cb_c=col(w["mlstm_conv_b"]), ml_bif=row(bif), ml_bif_c=col(bif),
        ml_ng=row(w["mlstm_norm_g"]), ml_ng_c=col(w["mlstm_norm_g"]))


def _layer_view(stacked, l):
    return {k: (_layer_view(v, l) if isinstance(v, dict) else _Layer(v, l)) for k, v in stacked.items()}


RWKV_BLOCK = 512
GLA_BLOCK = 512
MLSTM_BLOCK = 256
POOL_BLOCK = 256
DENSE_TILE = 512
FFN_TILE = 256


def _prompt_layer(x, p):
    bsz, seq, d = x.shape
    z = _linear(x.reshape(bsz * seq, d), p["w_in"], DENSE_TILE).reshape(bsz, seq, ZC)
    zeros = lambda *s: jnp.zeros((bsz,) + s, F32)
    y_a, s_gla = _gla_prompt(z, zeros(N_HEADS, DK_A, DV_A), p["gla_wa"], p["gla_ba"], p["gla_ng"], GLA_BLOCK)
    y_b, s_pool = _pool_prompt(z, zeros(POOL_BUF, W_GROUP), p["pool_w"], p["pool_scale"], POOL_BLOCK, 0)
    y_c, s_rwkv, s_shift = _rwkv_prompt(z, zeros(1, RWKV_W), zeros(N_HEADS, N_C, N_C), p["rwkv"], RWKV_BLOCK)
    y_d, s_c, s_n, s_m, s_conv = _mlstm_prompt(z, zeros(CONV_D - 1, 512), zeros(N_HEADS, D_D, D_D), zeros(1, 256),
                                               zeros(1, 256), p["ml_cw"], p["ml_cb"], p["ml_bif"], p["ml_ng"],
                                               MLSTM_BLOCK)
    x2, s_ffn = _ffn_prompt([y_a, y_b, y_c, y_d], x, zeros(FFN_CONV - 1, 2 * D_FF), p, FFN_TILE)
    states = (s_gla, s_pool, s_rwkv, s_shift[:, :, :RWKV_COLS], s_c, s_n.reshape(bsz, N_HEADS, D_D),
              s_m[:, 0, ::D_D], s_conv, s_ffn)
    return x2, states


def _sample_layer(x, st, p, steps, nb):
    s_gla, s_pool, s_rwkv, s_shift, s_c, s_n, s_m, s_conv, s_ffn = st
    z = _linear(x, p["w_in"], steps * nb).reshape(steps, nb, ZC)
    y_a, n_gla = _gla_sample(z, s_gla, p["gla_wa_t"], p["gla_ba_c"], p["gla_ng_c"])
    y_b, n_pool = _pool_sample(z, s_pool, p["pool_w"], p["pool_scale"], PAST_LEN)
    y_c, n_rwkv, n_shift = _rwkv_sample(z, s_shift, s_rwkv, p["rwkv"])
    y_d, n_c, n_n, n_m, n_conv = _mlstm_sample(z, s_conv, s_c, s_n, s_m, p["ml_cw_t"], p["ml_cb_c"], p["ml_bif_c"],
                                               p["ml_ng_c"])
    flat = lambda y: y.reshape(steps * nb, W_GROUP)
    x2, n_ffn = _ffn_sample([flat(y_a), flat(y_b), flat(y_c), flat(y_d)], x, s_ffn, p, nb, steps)
    return x2, (n_gla, n_pool, n_rwkv, n_shift, n_c, n_n, n_m, n_conv, n_ffn)


def kernel(x_prompt, x_sample, state_gla, state_pool, state_rwkv, state_rwkv_shift, state_mlstm_c, state_mlstm_n, state_mlstm_m, state_mlstm_conv, state_ffn_conv, w_in, gla_w_a2, gla_b_a, gla_norm_g, pool_w, pool_scale, rwkv_mu, rwkv_w0, rwkv_w2, rwkv_a0, rwkv_a2, rwkv_g2, rwkv_k_k, rwkv_k_a, rwkv_r_k, rwkv_ln_g, rwkv_ln_b, mlstm_conv_w, mlstm_conv_b, mlstm_b_i, mlstm_b_f, mlstm_norm_g, w_out, ln1_g, ln1_b, ffn_w_up, ffn_conv_w, ffn_conv_b, ffn_w_down, ln2_g, ln2_b):
    w = dict(w_in=w_in, gla_w_a2=gla_w_a2, gla_b_a=gla_b_a, gla_norm_g=gla_norm_g, pool_w=pool_w,
             pool_scale=pool_scale, rwkv_mu=rwkv_mu, rwkv_w0=rwkv_w0, rwkv_w2=rwkv_w2, rwkv_a0=rwkv_a0,
             rwkv_a2=rwkv_a2, rwkv_g2=rwkv_g2, rwkv_k_k=rwkv_k_k, rwkv_k_a=rwkv_k_a, rwkv_r_k=rwkv_r_k,
             rwkv_ln_g=rwkv_ln_g, rwkv_ln_b=rwkv_ln_b, mlstm_conv_w=mlstm_conv_w, mlstm_conv_b=mlstm_conv_b,
             mlstm_b_i=mlstm_b_i, mlstm_b_f=mlstm_b_f, mlstm_norm_g=mlstm_norm_g, w_out=w_out, ln1_g=ln1_g,
             ln1_b=ln1_b, ffn_w_up=ffn_w_up, ffn_conv_w=ffn_conv_w, ffn_conv_b=ffn_conv_b, ffn_w_down=ffn_w_down,
             ln2_g=ln2_g, ln2_b=ln2_b)
    sample_states = (state_gla, state_pool, state_rwkv, state_rwkv_shift, state_mlstm_c, state_mlstm_n,
                     state_mlstm_m, state_mlstm_conv, state_ffn_conv)
    nb, steps, d = x_sample.shape
    flat_in = [s.reshape(DEPTH, nb, -1) for s in sample_states]
    flat_in[3] = _pad_cols(flat_in[3], RWKV_W)
    flat_in[6] = _pad_cols(flat_in[6], 128)
    stacked = _stacked_params(w)
    yp = x_prompt
    ys = x_sample.transpose(1, 0, 2).reshape(steps * nb, d)
    acc_p = [[] for _ in sample_states]
    acc_s = [[] for _ in sample_states]
    for l in range(DEPTH):
        p = _layer_view(stacked, l)
        yp, st_p = _prompt_layer(yp, p)
        ys, st_s = _sample_layer(ys, tuple(_Layer(s, l) for s in flat_in), p, steps, nb)
        for i in range(len(sample_states)):
            acc_p[i].append(st_p[i])
            acc_s[i].append(st_s[i])
    ys = ys.reshape(steps, nb, d).transpose(1, 0, 2)
    out_s = [jnp.stack(a) for a in acc_s]
    out_s[3] = out_s[3][:, :, :RWKV_COLS]
    out_s[6] = out_s[6][:, 0:N_HEADS, :].transpose(0, 2, 1)
    out = [yp, ys]
    for sp, ss, ref in zip(acc_p, out_s, sample_states):
        out.append(jnp.stack(sp))
        out.append(ss.reshape(ref.shape))
    return tuple(out)
```

```python
import functools
import math

import jax
import jax.numpy as jnp
from jax import lax
from jax.experimental import pallas as pl
from jax.experimental.pallas import tpu as pltpu

F32 = jnp.float32
BF16 = jnp.bfloat16
HI = lax.Precision.HIGHEST

D_MODEL = 1024
DEPTH = 4
PAST_LEN = 16384
W_GROUP = 256
N_HEADS = 4
DK_A = 32
DV_A = 64
R_GLA = 16
GLA_TAU = 16.0
POOL_WINDOWS = (2, 4, 8, 16)
POOL_BUF = 15
N_C = 64
R_W, R_AA, R_G = 16, 16, 32
RWKV_COLS = 832
RWKV_LN_EPS = 64e-5
D_D = 64
CONV_D = 4
D_FF = 2816
FFN_CONV = 3
ALPHA = (2 * DEPTH) ** 0.25
CHUNK = 64

GLA_W = 896
RWKV_W = 896
GLA_OFF, RWKV_OFF, POOL_OFF, ML_OFF, MLIF_OFF = 0, 896, 1792, 2048, 3072
ZC = 3200
ML_W = 1024
MLIF_W = 128

VMEM_LIMIT = 56 * 1024 * 1024


def _cp(*sem):
    return pltpu.CompilerParams(dimension_semantics=sem, vmem_limit_bytes=VMEM_LIMIT)


def _mm(a, b):
    return jnp.dot(a.astype(BF16), b.astype(BF16), preferred_element_type=F32)


def _mm_nt(a, b):
    return lax.dot_general(a.astype(BF16), b.astype(BF16), (((1,), (1,)), ((), ())), preferred_element_type=F32)


def _mm_tn(a, b):
    return lax.dot_general(a.astype(BF16), b.astype(BF16), (((0,), (0,)), ((), ())), preferred_element_type=F32)


def _mm_hi(a, b):
    return jnp.dot(a, b, preferred_element_type=F32, precision=HI)


def _split2(x):
    hi = x.astype(BF16)
    return hi, (x - hi.astype(F32)).astype(BF16)


def _mm_xc(x, c2):
    hi, mid = _split2(x)
    return jnp.dot(jnp.concatenate([hi, mid], axis=1), c2, preferred_element_type=F32)


def _mm_cx(c2, x):
    hi, mid = _split2(x)
    return jnp.dot(c2, jnp.concatenate([hi, mid], axis=0), preferred_element_type=F32)


def _twice(c, axis):
    c = c.astype(F32).astype(BF16)
    return jnp.concatenate([c, c], axis=axis)


def _iota(shape, dim):
    return lax.broadcasted_iota(jnp.int32, shape, dim)


def _sigmoid(x):
    return 1.0 / (1.0 + jnp.exp(-x))


def _silu(x):
    return x * _sigmoid(x)


def _log_sigmoid(x):
    return jnp.minimum(x, 0.0) - jnp.log(1.0 + jnp.exp(-jnp.abs(x)))


def _tile4(x):
    return jnp.concatenate([x, x, x, x], axis=0)


def _block_diag_mask(rows, cols, rshift, cshift):
    return (_iota((rows, cols), 0) >> rshift) == (_iota((rows, cols), 1) >> cshift)


def _layer_norm(h, g, b, eps):
    mu = jnp.mean(h, axis=-1, keepdims=True)
    xc = h - mu
    var = jnp.mean(xc * xc, axis=-1, keepdims=True)
    return xc * lax.rsqrt(var + eps) * g + b


class _Layer:
    def __init__(self, arr, l):
        self.arr, self.l, self.shape = arr, l, tuple(arr.shape[1:])


def _layer_spec(p):
    n, l = len(p.shape), p.l
    return pl.BlockSpec((None,) + p.shape, lambda *g: (l,) + (0,) * n)


def _stacked_like(p):
    return jax.ShapeDtypeStruct(p.arr.shape, p.arr.dtype)


def _linear_body(x_ref, w_ref, o_ref):
    o_ref[...] = jnp.dot(x_ref[...].astype(BF16), w_ref[...], preferred_element_type=F32)


def _linear(x, w, tm):
    t, k = x.shape
    n = w.shape[1]
    return pl.pallas_call(
        _linear_body,
        grid=(t // tm,),
        in_specs=[pl.BlockSpec((tm, k), lambda i: (i, 0)), _layer_spec(w)],
        out_specs=pl.BlockSpec((tm, n), lambda i: (i, 0)),
        out_shape=jax.ShapeDtypeStruct((t, n), F32),
        compiler_params=_cp("arbitrary"),
        name="in_proj",
    )(x, w.arr)


def _outproj(ys, x, w_ref, g_ref, b_ref):
    y = jnp.concatenate(ys, axis=1).astype(BF16)
    mixed = jnp.dot(y, w_ref[...], preferred_element_type=F32)
    return _layer_norm(ALPHA * x + mixed, g_ref[...], b_ref[...], 1e-5)


FF_CHUNK = 256
N_FF_CHUNKS = D_FF // FF_CHUNK
UP_AHEAD = 2


def _ffn_prompt_body(ya_ref, yb_ref, yc_ref, yd_ref, x_ref, buf_ref, wo_ref, g1_ref, b1_ref,
                     wup_ref, cw_ref, cb_ref, wdn_ref, g_ref, b_ref,
                     o_ref, st_ref, carry_ref, *, tm, n_tiles):
    j = pl.program_id(1)

    @pl.when(j == 0)
    def _():
        carry_ref[...] = buf_ref[0]

    x = _outproj([ya_ref[0], yb_ref[0], yc_ref[0], yd_ref[0]], x_ref[0], wo_ref, g1_ref, b1_ref)
    xb = x.astype(BF16)
    row = _iota((8, FF_CHUNK), 0)
    col = lambda c, half: slice(half * D_FF + c * FF_CHUNK, half * D_FF + (c + 1) * FF_CHUNK)
    up = lambda c: [jnp.dot(xb, wup_ref[:, col(c, half)], preferred_element_type=F32) for half in range(2)]
    f = None
    ahead = [up(c) for c in range(UP_AHEAD)]
    for c in range(N_FF_CHUNKS):
        if c + UP_AHEAD < N_FF_CHUNKS:
            ahead.append(up(c + UP_AHEAD))
        u_now = ahead.pop(0)
        acts = []
        for half in range(2):
            sl = col(c, half)
            u = u_now[half]
            two_back, one_back = carry_ref[0:1, sl], carry_ref[1:2, sl]
            w0, w1, w2 = cw_ref[0:1, sl], cw_ref[1:2, sl], cw_ref[2:3, sl]
            r1 = pltpu.roll(u, 1, 0)
            r2 = pltpu.roll(u, 2, 0)
            r1_head = jnp.where(row == 0, one_back, r1[0:8])
            r2_head = jnp.where(row == 0, two_back, jnp.where(row == 1, one_back, r2[0:8]))
            head = cb_ref[:, sl] + w0 * r2_head + w1 * r1_head + w2 * u[0:8]
            body = cb_ref[:, sl] + w0 * r2 + w1 * r1 + w2 * u
            carry_ref[:, sl] = u[tm - 2:tm]
            acts.append(jnp.concatenate([head, body[8:]], axis=0))
        h = (_silu(acts[0]) * acts[1]).astype(BF16)
        part = jnp.dot(h, wdn_ref[c * FF_CHUNK:(c + 1) * FF_CHUNK, :], preferred_element_type=F32)
        f = part if f is None else f + part
    o_ref[0] = _layer_norm(ALPHA * x + f, g_ref[...], b_ref[...], 1e-5)

    @pl.when(j == n_tiles - 1)
    def _():
        st_ref[0] = carry_ref[...]


_DENSE_PARAMS = ("w_out", "ln1_g", "ln1_b", "w_up", "ffn_cw", "ffn_cb", "w_down", "ln2_g", "ln2_b")


def _ffn_prompt(ys, x3, buf, p, tm):
    bsz, seq, d = x3.shape
    n_tiles = seq // tm
    params = [p[k] for k in _DENSE_PARAMS]
    tok = lambda w: pl.BlockSpec((1, tm, w), lambda i, j: (i, j, 0))
    st = pl.BlockSpec((1, 2, 2 * D_FF), lambda i, j: (i, 0, 0))
    return pl.pallas_call(
        functools.partial(_ffn_prompt_body, tm=tm, n_tiles=n_tiles),
        grid=(bsz, n_tiles),
        in_specs=[tok(W_GROUP)] * 4 + [tok(d), st] + [_layer_spec(q) for q in params],
        out_specs=[tok(d), st],
        out_shape=[jax.ShapeDtypeStruct((bsz, seq, d), F32), jax.ShapeDtypeStruct((bsz, 2, 2 * D_FF), F32)],
        scratch_shapes=[pltpu.VMEM((2, 2 * D_FF), F32)],
        compiler_params=_cp("arbitrary", "arbitrary"),
        name="ffn_prompt",
    )(*ys, x3, buf, *[q.arr for q in params])


def _ffn_sample_body(ya_ref, yb_ref, yc_ref, yd_ref, x_ref, buf_ref, wo_ref, g1_ref, b1_ref,
                     wup_ref, cw_ref, cb_ref, wdn_ref, g_ref, b_ref,
                     o_ref, st_ref, h_ref, *, nb, steps):
    t = nb * steps
    x = _outproj([ya_ref[...], yb_ref[...], yc_ref[...], yd_ref[...]], x_ref[...], wo_ref, g1_ref, b1_ref)
    xb = x.astype(BF16)
    for c in range(N_FF_CHUNKS):
        acts = []
        for half in range(2):
            lo = half * D_FF + c * FF_CHUNK
            sl = slice(lo, lo + FF_CHUNK)
            u = jnp.dot(xb, wup_ref[:, sl], preferred_element_type=F32)
            b0 = buf_ref[:, lo:lo + FF_CHUNK]
            b1 = buf_ref[:, 2 * D_FF + lo:2 * D_FF + lo + FF_CHUNK]
            back1 = jnp.concatenate([b1, u[0:t - nb]], axis=0)
            back2 = jnp.concatenate([b0, b1, u[0:t - 2 * nb]], axis=0)
            conv = cb_ref[:, sl] + cw_ref[0:1, sl] * back2 + cw_ref[1:2, sl] * back1 + cw_ref[2:3, sl] * u
            st_ref[:, lo:lo + FF_CHUNK] = u[t - 2 * nb:t - nb]
            st_ref[:, 2 * D_FF + lo:2 * D_FF + lo + FF_CHUNK] = u[t - nb:t]
            acts.append(conv)
        h_ref[:, c * FF_CHUNK:(c + 1) * FF_CHUNK] = (_silu(acts[0]) * acts[1]).astype(BF16)
    f = jnp.dot(h_ref[...], wdn_ref[...], preferred_element_type=F32)
    o_ref[...] = _layer_norm(ALPHA * x + f, g_ref[...], b_ref[...], 1e-5)


def _ffn_sample(ys, x, buf, p, nb, steps):
    t, d = x.shape
    params = [p[k] for k in _DENSE_PARAMS]
    full = lambda shape: pl.BlockSpec(shape, lambda i: (0,) * len(shape))
    return pl.pallas_call(
        functools.partial(_ffn_sample_body, nb=nb, steps=steps),
        grid=(1,),
        in_specs=[full((t, W_GROUP))] * 4 + [full((t, d)), _layer_spec(buf)] + [_layer_spec(q) for q in params],
        out_specs=[full((t, d)), _layer_spec(buf)],
        out_shape=[jax.ShapeDtypeStruct((t, d), F32), _stacked_like(buf)],
        input_output_aliases={5: 1},
        scratch_shapes=[pltpu.VMEM((t, D_FF), BF16)],
        compiler_params=_cp("arbitrary"),
        name="ffn_sample",
    )(*ys, x, buf.arr, *[q.arr for q in params])


def _tril64():
    return (_iota((CHUNK, CHUNK), 0) >= _iota((CHUNK, CHUNK), 1)).astype(F32)


def _exp_masks():
    t = _iota((CHUNK, 256), 0)
    s = _iota((CHUNK, 256), 1) & (CHUNK - 1)
    return t >= s, t > s, t == s


def _bd256():
    return _block_diag_mask(256, 256, 6, 6)


def _tril_blocks(n):
    r = _iota((n, n), 0)
    c = _iota((n, n), 1)
    return ((r >= c) & ((r >> 6) == (c >> 6))).astype(F32)


def _mask_bf16(mask):
    return mask.astype(F32).astype(BF16)


def _bdx(x, mask01):
    return _tile4(x.astype(BF16)) * mask01


def _put_block_diag(dst_ref, blocks, rows, cols):
    dst_ref[...] = jnp.zeros(dst_ref.shape, F32)
    for h in range(N_HEADS):
        dst_ref[h * rows:(h + 1) * rows, h * cols:(h + 1) * cols] = blocks[h]


def _get_block_diag(src_ref, out_ref, rows, cols):
    for h in range(N_HEADS):
        out_ref[0, h] = src_ref[h * rows:(h + 1) * rows, h * cols:(h + 1) * cols]


def _gla_prompt_body(z_ref, s0_ref, wa_ref, ba_ref, ng_ref, y_ref, so_ref, sbd_ref, *, n_chunks, n_blocks):
    j = pl.program_id(1)

    @pl.when(j == 0)
    def _():
        _put_block_diag(sbd_ref, s0_ref[0], DK_A, DV_A)

    lb = n_chunks * CHUNK
    causal, _, _ = _exp_masks()
    bd_state = _block_diag_mask(128, 256, 5, 6)
    bd_k = _mask_bf16(_block_diag_mask(256, 128, 6, 5))
    bd_v = _mask_bf16(_bd256())
    pavg2 = _twice(bd_v.astype(F32) * (1.0 / DV_A), 0)

    lg = _log_sigmoid(_mm(z_ref[0, :, 768:896], wa_ref[...]) + ba_ref[...]) * (1.0 / GLA_TAU)
    bc_all = _mm_cx(_twice(_tril_blocks(lb), 1), lg)

    o_intra, q_dec, incr, decay = [], [], [], []
    for c in range(n_chunks):
        rows = slice(c * CHUNK, (c + 1) * CHUNK)
        q = z_ref[0, rows, 0:128] * DK_A ** -0.5
        k = z_ref[0, rows, 128:256]
        v = z_ref[0, rows, 256:512]
        bc = bc_all[rows]
        bl = bc[CHUNK - 1:CHUNK]
        rho = bc[CHUNK // 2 - 1:CHUNK // 2]
        att = _mm_nt(q * jnp.exp(bc - rho), _bdx(k * jnp.exp(rho - bc), bd_k))
        o_intra.append(_mm(jnp.where(causal, att, 0.0), _bdx(v, bd_v)))
        q_dec.append(q * jnp.exp(bc))
        incr.append(jnp.where(bd_state, _mm_tn(k * jnp.exp(bl - bc), v), 0.0))
        decay_col = jnp.broadcast_to(jnp.exp(bl), (128, 128)).T
        decay.append(jnp.concatenate([decay_col, decay_col], axis=1))
    s = sbd_ref[...]
    o = []
    for c in range(n_chunks):
        o.append(_mm(q_dec[c], s) + o_intra[c])
        s = decay[c] * s + incr[c]
    sbd_ref[...] = s
    o = jnp.concatenate(o, axis=0)
    ms = _mm_xc(o * o, pavg2)
    y_ref[0] = (o * lax.rsqrt(ms + 1e-6) * ng_ref[...] * _silu(z_ref[0, :, 512:768])).astype(BF16)

    @pl.when(j == n_blocks - 1)
    def _():
        _get_block_diag(sbd_ref, so_ref, DK_A, DV_A)


def _gla_prompt(z3, s0, wa, ba, ng, lb):
    bsz, seq, _ = z3.shape
    n_blocks = seq // lb
    st = pl.BlockSpec((1, N_HEADS, DK_A, DV_A), lambda i, j: (i, 0, 0, 0))
    return pl.pallas_call(
        functools.partial(_gla_prompt_body, n_chunks=lb // CHUNK, n_blocks=n_blocks),
        grid=(bsz, n_blocks),
        in_specs=[pl.BlockSpec((1, lb, GLA_W), lambda i, j: (i, j, GLA_OFF // GLA_W)), st,
                  _layer_spec(wa), _layer_spec(ba), _layer_spec(ng)],
        out_specs=[pl.BlockSpec((1, lb, W_GROUP), lambda i, j: (i, j, 0)), st],
        out_shape=[jax.ShapeDtypeStruct((bsz, seq, W_GROUP), BF16),
                   jax.ShapeDtypeStruct((bsz, N_HEADS, DK_A, DV_A), F32)],
        scratch_shapes=[pltpu.VMEM((128, 256), F32)],
        compiler_params=_cp("arbitrary", "arbitrary"),
        name="gla_prompt",
    )(z3, s0, wa.arr, ba.arr, ng.arr)


def _pool_prompt_body(z_ref, buf_ref, w_ref, sc_ref, y_ref, bo_ref, hist_ref, *, lb, n_blocks, start_pos):
    j = pl.program_id(1)

    @pl.when(j == 0)
    def _():
        hist_ref[0:1, :] = jnp.zeros((1, W_GROUP), F32)
        hist_ref[1:16, :] = buf_ref[0]

    zp = z_ref[0]
    e = jnp.concatenate([hist_ref[...], zp], axis=0)
    s2 = e + pltpu.roll(e, 1, 0)
    s4 = s2 + pltpu.roll(s2, 2, 0)
    s8 = s4 + pltpu.roll(s4, 4, 0)
    s16 = s8 + pltpu.roll(s8, 8, 0)
    lane = _iota((lb, W_GROUP), 1)
    win = jnp.where(lane < 64, s2[16:], jnp.where(lane < 128, s4[16:], jnp.where(lane < 192, s8[16:], s16[16:])))
    width = jnp.where(lane < 64, 2, jnp.where(lane < 128, 4, jnp.where(lane < 192, 8, 16)))
    pos = start_pos + j * lb + _iota((lb, W_GROUP), 0)
    cnt = jnp.minimum(pos + 1, width).astype(F32)
    y_ref[0] = (_mm(win / cnt - zp, w_ref[...]) * sc_ref[...]).astype(BF16)
    hist_ref[1:16, :] = e[lb + 1:lb + 16]

    @pl.when(j == n_blocks - 1)
    def _():
        bo_ref[0] = hist_ref[1:16, :]


def _pool_prompt(z3, buf, wbd, scale, lb, start_pos):
    bsz, seq, _ = z3.shape
    n_blocks = seq // lb
    st = pl.BlockSpec((1, POOL_BUF, W_GROUP), lambda i, j: (i, 0, 0))
    return pl.pallas_call(
        functools.partial(_pool_prompt_body, lb=lb, n_blocks=n_blocks, start_pos=start_pos),
        grid=(bsz, n_blocks),
        in_specs=[pl.BlockSpec((1, lb, W_GROUP), lambda i, j: (i, j, POOL_OFF // W_GROUP)), st,
                  _layer_spec(wbd), _layer_spec(scale)],
        out_specs=[pl.BlockSpec((1, lb, W_GROUP), lambda i, j: (i, j, 0)), st],
        out_shape=[jax.ShapeDtypeStruct((bsz, seq, W_GROUP), BF16),
                   jax.ShapeDtypeStruct((bsz, POOL_BUF, W_GROUP), F32)],
        scratch_shapes=[pltpu.VMEM((16, W_GROUP), F32)],
        compiler_params=_cp("arbitrary", "arbitrary"),
        name="pool_prompt",
    )(z3, buf, wbd.arr, scale.arr)


def _rwkv_prompt_body(z_ref, sh0_ref, s0_ref, mu_ref, w0_ref, w2_ref, a0_ref, a2_ref, g2_ref, kk_ref, ka_ref,
                      rk_ref, lg_ref, lb_ref, y_ref, so_ref, sho_ref,
                      nbd_ref, carry_ref, r_s, k_s, v_s, kk_s, b_s, lw_s, g_s, *, n_chunks, n_blocks, lb):
    j = pl.program_id(1)

    @pl.when(j == 0)
    def _():
        _put_block_diag(nbd_ref, s0_ref[0], N_C, N_C)
        carry_ref[...] = sh0_ref[0]

    bd = _bd256()
    bd_m = _mask_bf16(bd)
    ones2 = _twice(bd, 0)
    pavg2 = _twice(bd.astype(F32) * (1.0 / N_C), 0)

    f = z_ref[0]
    prev = jnp.concatenate([carry_ref[...], f[:-1]], axis=0)
    mixed = f + (prev - f) * mu_ref[...]
    carry_ref[...] = f[lb - 1:lb]
    r = mixed[:, 0:256]
    k = mixed[:, 256:512]
    v = mixed[:, 512:768]
    low = mixed[:, 768:896]
    lw_all = -_sigmoid(w0_ref[...] + _mm(jnp.tanh(low), w2_ref[...])) * math.exp(-0.5)
    lw_s[...] = lw_all
    a = _sigmoid(a0_ref[...] + _mm(low, a2_ref[...]))
    g_s[...] = _mm(_sigmoid(low), g2_ref[...])
    kk = k * kk_ref[...]
    kk = kk * lax.rsqrt(jnp.maximum(_mm_xc(kk * kk, ones2), 1e-24))
    r_s[...] = r
    v_s[...] = v
    kk_s[...] = kk
    k_s[...] = k * (1.0 + (a - 1.0) * ka_ref[...])
    b_s[...] = kk * a
    gc_all = _mm_cx(_twice(_tril_blocks(lb), 1), lw_all)

    lower, strict, eye_exp = _exp_masks()
    eye_f = eye_exp.astype(F32)

    cs = range(n_chunks)
    sl = [slice(c * CHUNK, (c + 1) * CHUNK) for c in cs]
    gc = [gc_all[s] for s in sl]
    kt = [kk_s[sl[c], :] * jnp.exp(gc[c] - lw_s[sl[c], :]) for c in cs]
    rt = [r_s[sl[c], :] * jnp.exp(gc[c]) for c in cs]
    a_kk, a_kb, a_rk, a_rb = [], [], [], []
    for c in cs:
        einv = jnp.exp(-gc[c])
        lhs = jnp.concatenate([kt[c], rt[c]], axis=0)
        rhs = jnp.concatenate([_bdx(k_s[sl[c], :] * einv, bd_m), _bdx(b_s[sl[c], :] * einv, bd_m)], axis=0)
        aa = _mm_nt(lhs, rhs)
        a_kk.append(jnp.where(strict, aa[0:CHUNK, 0:256], 0.0))
        a_kb.append(jnp.where(strict, aa[0:CHUNK, 256:512], 0.0))
        a_rk.append(jnp.where(lower, aa[CHUNK:, 0:256], 0.0))
        a_rb.append(jnp.where(lower, aa[CHUNK:, 256:512], 0.0))
    t = [eye_f - a_kb[c] for c in cs]
    p = [_mm(a_kb[c], _bdx(a_kb[c], bd_m)) for c in cs]
    for _ in range(4):
        m = [_mm(jnp.concatenate([p[c], t[c]], axis=0), _bdx(p[c], bd_m)) for c in cs]
        p = [m[c][0:CHUNK] for c in cs]
        t = [t[c] + m[c][CHUNK:] for c in cs]
    t = [t[c] + _mm(t[c], _bdx(p[c], bd_m)) for c in cs]
    av = [_mm(jnp.concatenate([a_kk[c], a_rk[c]], axis=0), _bdx(v_s[sl[c], :], bd_m)) for c in cs]
    tu = [_mm(t[c], jnp.concatenate([_bdx(av[c][0:CHUNK], bd_m), _bdx(kt[c], bd_m)], axis=1)) for c in cs]
    u0 = [tu[c][:, 0:256] for c in cs]
    tk = [tu[c][:, 256:512] for c in cs]
    ro = [_mm(a_rb[c], jnp.concatenate([_bdx(tk[c], bd_m), _bdx(u0[c], bd_m)], axis=1)) for c in cs]
    r_eff = [rt[c] - ro[c][:, 0:256] for c in cs]
    o0 = [av[c][CHUNK:] - ro[c][:, 256:512] for c in cs]
    a_st, b_st, decay = [], [], []
    for c in cs:
        gl = gc[c][CHUNK - 1:CHUNK]
        dk = jnp.exp(gl - gc[c])
        kd = k_s[sl[c], :] * dk
        bdk = b_s[sl[c], :] * dk
        a_st.append(jnp.where(bd, -_mm_tn(tk[c], bdk), 0.0))
        b_st.append(jnp.where(bd, _mm_tn(jnp.concatenate([v_s[sl[c], :], -u0[c]], axis=0),
                                         jnp.concatenate([kd, bdk], axis=0)), 0.0))
        decay.append(jnp.exp(gl))
    n = nbd_ref[...]
    o = []
    for c in cs:
        o.append(_mm_nt(r_eff[c], n) + o0[c])
        n = n * decay[c] + _mm(n, a_st[c]) + b_st[c]
    nbd_ref[...] = n
    o = jnp.concatenate(o, axis=0)
    mu = _mm_xc(o, pavg2)
    xc = o - mu
    var = _mm_xc(xc * xc, pavg2)
    on = xc * lax.rsqrt(var + RWKV_LN_EPS) * lg_ref[...] + lb_ref[...]
    bonus = _mm_xc(r_s[...] * k_s[...] * rk_ref[...], ones2) * v_s[...]
    y_ref[0] = ((on + bonus) * g_s[...]).astype(BF16)

    @pl.when(j == n_blocks - 1)
    def _():
        _get_block_diag(nbd_ref, so_ref, N_C, N_C)
        sho_ref[0] = carry_ref[...]


_RWKV_PARAMS = ("mu", "w0", "w2", "a0", "a2", "g2", "k_k", "k_a", "r_k", "ln_g", "ln_b")


def _rwkv_prompt(z3, sh0, s0, p, lb):
    bsz, seq, _ = z3.shape
    n_blocks = seq // lb
    st = pl.BlockSpec((1, N_HEADS, N_C, N_C), lambda i, j: (i, 0, 0, 0))
    sh = pl.BlockSpec((1, 1, RWKV_W), lambda i, j: (i, 0, 0))
    blk = pltpu.VMEM((lb, 256), F32)
    params = [p[k] for k in _RWKV_PARAMS]
    return pl.pallas_call(
        functools.partial(_rwkv_prompt_body, n_chunks=lb // CHUNK, n_blocks=n_blocks, lb=lb),
        grid=(bsz, n_blocks),
        in_specs=[pl.BlockSpec((1, lb, RWKV_W), lambda i, j: (i, j, RWKV_OFF // RWKV_W)), sh, st]
        + [_layer_spec(q) for q in params],
        out_specs=[pl.BlockSpec((1, lb, W_GROUP), lambda i, j: (i, j, 0)), st, sh],
        out_shape=[jax.ShapeDtypeStruct((bsz, seq, W_GROUP), BF16),
                   jax.ShapeDtypeStruct((bsz, N_HEADS, N_C, N_C), F32),
                   jax.ShapeDtypeStruct((bsz, 1, RWKV_W), F32)],
        scratch_shapes=[pltpu.VMEM((256, 256), F32), pltpu.VMEM((1, RWKV_W), F32),
                        blk, blk, blk, blk, blk, blk, blk],
        compiler_params=_cp("arbitrary", "arbitrary"),
        name="rwkv_prompt",
    )(z3, sh0, s0, *[q.arr for q in params])


def _mlstm_prompt_body(z_ref, zif_ref, cb0_ref, c0_ref, n0_ref, m0_ref, cw_ref, cbias_ref, bif_ref, ng_ref,
                       y_ref, co_ref, no_ref, mo_ref, cbo_ref,
                       cbd_ref, n_ref, m_ref, carry_ref, qk_s, *, n_chunks, n_blocks, lb):
    j = pl.program_id(1)

    @pl.when(j == 0)
    def _():
        _put_block_diag(cbd_ref, c0_ref[0], D_D, D_D)
        n_ref[...] = n0_ref[0]
        m_ref[...] = m0_ref[0]
        carry_ref[...] = cb0_ref[0]

    qk = z_ref[0, :, 0:512]
    conv = cbias_ref[...] + cw_ref[CONV_D - 1:CONV_D, :] * qk
    head = cbias_ref[...] + cw_ref[CONV_D - 1:CONV_D, :] * qk[0:8]
    row = _iota((8, 512), 0)
    for back in range(1, CONV_D):
        w = cw_ref[CONV_D - 1 - back:CONV_D - back, :]
        rolled = pltpu.roll(qk, back, 0)
        conv = conv + w * rolled
        fixed = rolled[0:8]
        for r in range(back):
            fixed = jnp.where(row == r, carry_ref[CONV_D - 1 - back + r:CONV_D - back + r, :], fixed)
        head = head + w * fixed
    qk_s[...] = _silu(jnp.concatenate([head, conv[8:]], axis=0))
    carry_ref[...] = qk[lb - (CONV_D - 1):lb]

    bd = _bd256()
    bd_m = _mask_bf16(bd)
    ones2 = _twice(bd, 0)
    pavg2 = _twice(bd.astype(F32) * (1.0 / D_D), 0)
    causal, _, eye_exp = _exp_masks()
    er = _iota((128, 512), 0)
    ec = _iota((128, 512), 1)
    expand2 = _twice(((ec < 256) & (er == (ec >> 6))) | ((ec >= 256) & (er == 4 + ((ec - 256) >> 6))), 0)
    is_f = (_iota((lb, 128), 1) >= 4) & (_iota((lb, 128), 1) < 8)
    lane = _iota((CHUNK, 256), 1)

    gates = zif_ref[0] + bif_ref[...]
    gates = jnp.where(is_f, _log_sigmoid(gates), gates)
    gates = jnp.where(is_f, _mm_cx(_twice(_tril_blocks(lb), 1), gates), gates)
    ge_all = _mm_xc(gates, expand2)

    for c in range(n_chunks):
        rows = slice(c * CHUNK, (c + 1) * CHUNK)
        q = qk_s[rows, 0:256] * D_D ** -0.5
        k = qk_s[rows, 256:512]
        v = z_ref[0, rows, 512:768]
        og = z_ref[0, rows, 768:1024]
        iexp = ge_all[rows, 0:256]
        fexp = ge_all[rows, 256:512]
        d_row = jnp.sum(jnp.where(eye_exp, iexp - fexp, 0.0), axis=0, keepdims=True)
        dlog = jnp.where(causal, fexp + d_row, -jnp.inf)
        mx = [jnp.max(dlog[:, h * 64:(h + 1) * 64], axis=1, keepdims=True) for h in range(N_HEADS)]
        mx = jnp.where(lane < 64, mx[0], jnp.where(lane < 128, mx[1], jnp.where(lane < 192, mx[2], mx[3])))
        qk_att = _mm_nt(q, _bdx(k, bd_m))
        m_prev = m_ref[...]
        inter = fexp + m_prev
        m_t = jnp.maximum(inter, mx)
        w0 = jnp.exp(inter - m_t)
        s = jnp.exp(dlog - m_t) * qk_att
        cbd = cbd_ref[...]
        n_row = n_ref[...]
        num = w0 * _mm(q, cbd) + _mm(s, _bdx(v, bd_m))
        den = _mm_xc(w0 * (q * n_row) + s, ones2)
        hh = num / jnp.maximum(jnp.abs(den), jnp.exp(-m_t))
        ms = _mm_xc(hh * hh, pavg2)
        y_ref[0, rows, :] = (_sigmoid(og) * (hh * lax.rsqrt(ms + 1e-6) * ng_ref[...])).astype(BF16)
        m_new = m_t[CHUNK - 1:CHUNK]
        f_last = fexp[CHUNK - 1:CHUNK]
        kw = k * jnp.exp(f_last - fexp + iexp - m_new)
        w0f = jnp.exp(f_last + m_prev - m_new)
        cbd_ref[...] = jnp.where(bd, w0f * cbd + _mm_tn(kw, v), 0.0)
        n_ref[...] = w0f * n_row + jnp.sum(kw, axis=0, keepdims=True)
        m_ref[...] = m_new

    @pl.when(j == n_blocks - 1)
    def _():
        _get_block_diag(cbd_ref, co_ref, D_D, D_D)
        no_ref[0] = n_ref[...]
        mo_ref[0] = m_ref[...]
        cbo_ref[0] = carry_ref[...]


def _mlstm_prompt(z3, cb0, c0, n0, m0, cw, cbias, bif, ng, lb):
    bsz, seq, _ = z3.shape
    n_blocks = seq // lb
    cst = pl.BlockSpec((1, N_HEADS, D_D, D_D), lambda i, j: (i, 0, 0, 0))
    row = pl.BlockSpec((1, 1, 256), lambda i, j: (i, 0, 0))
    cbs = pl.BlockSpec((1, CONV_D - 1, 512), lambda i, j: (i, 0, 0))
    return pl.pallas_call(
        functools.partial(_mlstm_prompt_body, n_chunks=lb // CHUNK, n_blocks=n_blocks, lb=lb),
        grid=(bsz, n_blocks),
        in_specs=[pl.BlockSpec((1, lb, ML_W), lambda i, j: (i, j, ML_OFF // ML_W)),
                  pl.BlockSpec((1, lb, MLIF_W), lambda i, j: (i, j, MLIF_OFF // MLIF_W)),
                  cbs, cst, row, row, _layer_spec(cw), _layer_spec(cbias), _layer_spec(bif), _layer_spec(ng)],
        out_specs=[pl.BlockSpec((1, lb, W_GROUP), lambda i, j: (i, j, 0)), cst, row, row, cbs],
        out_shape=[jax.ShapeDtypeStruct((bsz, seq, W_GROUP), BF16),
                   jax.ShapeDtypeStruct((bsz, N_HEADS, D_D, D_D), F32),
                   jax.ShapeDtypeStruct((bsz, 1, 256), F32),
                   jax.ShapeDtypeStruct((bsz, 1, 256), F32),
                   jax.ShapeDtypeStruct((bsz, CONV_D - 1, 512), F32)],
        scratch_shapes=[pltpu.VMEM((256, 256), F32), pltpu.VMEM((1, 256), F32), pltpu.VMEM((1, 256), F32),
                        pltpu.VMEM((CONV_D - 1, 512), F32), pltpu.VMEM((lb, 512), F32)],
        compiler_params=_cp("arbitrary", "arbitrary"),
        name="mlstm_prompt",
    )(z3, z3, cb0, c0, n0, m0, cw.arr, cbias.arr, bif.arr, ng.arr)


T_COLS = 1024


def _to_rows(src_ref, dst_ref, n):
    for i in range(n // T_COLS):
        dst_ref[i * T_COLS:(i + 1) * T_COLS, :] = src_ref[:, i * T_COLS:(i + 1) * T_COLS].T


def _to_cols(src_ref, dst_ref, n):
    for i in range(n // T_COLS):
        dst_ref[:, i * T_COLS:(i + 1) * T_COLS] = src_ref[i * T_COLS:(i + 1) * T_COLS, :].T


def _per_head(x, fn):
    return jnp.concatenate([fn(x[h * 64:(h + 1) * 64], h) for h in range(N_HEADS)], axis=0)


def _expand_heads(x, nb):
    return jnp.concatenate([jnp.broadcast_to(x[h:h + 1], (64, nb)) for h in range(N_HEADS)], axis=0)


def _gla_sample_body(z_ref, s_ref, wa_ref, ba_ref, ng_ref, y_ref, so_ref,
                     st_s, dec_s, k_s, q_s, v_s, o_s, *, steps, nb):
    n_state = N_HEADS * DK_A * DV_A
    _to_rows(s_ref, st_s, n_state)
    for t in range(steps):
        zt = z_ref[t].T
        q_s[t] = zt[0:128] * DK_A ** -0.5
        k_s[t] = zt[128:256]
        v_s[t] = zt[256:512]
        lg = _log_sigmoid(_mm_hi(wa_ref[...], zt[768:896]) + ba_ref[...]) * (1.0 / GLA_TAU)
        dec_s[t] = jnp.exp(lg)
    o_s[...] = jnp.zeros(o_s.shape, F32)

    def per_key(hk, carry):
        r0 = pl.multiple_of(hk * DV_A, DV_A)
        v0 = pl.multiple_of((hk >> 5) * DV_A, DV_A)
        s = st_s[pl.ds(r0, DV_A), :]
        for t in range(steps):
            s = dec_s[t, pl.ds(hk, 1), :] * s + k_s[t, pl.ds(hk, 1), :] * v_s[t, pl.ds(v0, DV_A), :]
            o_s[t, pl.ds(v0, DV_A), :] += q_s[t, pl.ds(hk, 1), :] * s
        st_s[pl.ds(r0, DV_A), :] = s
        return carry

    lax.fori_loop(0, N_HEADS * DK_A, per_key, 0)
    for t in range(steps):
        g = z_ref[t, :, 512:768].T
        on = _per_head(o_s[t], lambda oh, h: oh * lax.rsqrt(jnp.mean(oh * oh, axis=0, keepdims=True) + 1e-6))
        y_ref[t] = (on * ng_ref[...] * _silu(g)).T.astype(BF16)
    _to_cols(st_s, so_ref, n_state)


def _gla_sample(zs, s, wa_t, ba_col, ng_col):
    steps, nb, _ = zs.shape
    n_state = N_HEADS * DK_A * DV_A
    full = lambda shape: pl.BlockSpec(shape, lambda i: (0,) * len(shape))
    return pl.pallas_call(
        functools.partial(_gla_sample_body, steps=steps, nb=nb),
        grid=(1,),
        in_specs=[pl.BlockSpec((steps, nb, GLA_W), lambda i: (0, 0, GLA_OFF // GLA_W)), _layer_spec(s),
                  _layer_spec(wa_t), _layer_spec(ba_col), _layer_spec(ng_col)],
        out_specs=[full((steps, nb, W_GROUP)), _layer_spec(s)],
        out_shape=[jax.ShapeDtypeStruct((steps, nb, W_GROUP), BF16), _stacked_like(s)],
        input_output_aliases={1: 1},
        scratch_shapes=[pltpu.VMEM((n_state, nb), F32), pltpu.VMEM((steps, 128, nb), F32),
                        pltpu.VMEM((steps, 128, nb), F32), pltpu.VMEM((steps, 128, nb), F32),
                        pltpu.VMEM((steps, 256, nb), F32), pltpu.VMEM((steps, 256, nb), F32)],
        compiler_params=_cp("arbitrary"),
        name="gla_sample",
    )(zs, s.arr, wa_t.arr, ba_col.arr, ng_col.arr)


def _pool_sample_body(z_ref, buf_ref, w_ref, sc_ref, y_ref, bo_ref, *, steps, nb, start_pos):
    ext = [buf_ref[:, j * W_GROUP:(j + 1) * W_GROUP] for j in range(POOL_BUF)] + [z_ref[t] for t in range(steps)]
    lane = _iota((nb, W_GROUP), 1)

    def by_group(vals):
        return jnp.where(lane < 64, vals[0], jnp.where(lane < 128, vals[1], jnp.where(lane < 192, vals[2], vals[3])))

    for t in range(steps):
        r = POOL_BUF + t
        acc = ext[r]
        sums = {}
        for back in range(1, max(POOL_WINDOWS)):
            acc = acc + ext[r - back]
            sums[back + 1] = acc
        win = by_group([sums[w] for w in POOL_WINDOWS])
        cnt = by_group([float(min(start_pos + t + 1, w)) for w in POOL_WINDOWS])
        y_ref[t] = (_mm(win / cnt - ext[r], w_ref[...]) * sc_ref[...]).astype(BF16)
    for j in range(POOL_BUF):
        bo_ref[:, j * W_GROUP:(j + 1) * W_GROUP] = ext[steps + j]


def _pool_sample(zs, buf, wbd, scale, start_pos):
    steps, nb, _ = zs.shape
    full = lambda shape: pl.BlockSpec(shape, lambda i: (0,) * len(shape))
    return pl.pallas_call(
        functools.partial(_pool_sample_body, steps=steps, nb=nb, start_pos=start_pos),
        grid=(1,),
        in_specs=[pl.BlockSpec((steps, nb, W_GROUP), lambda i: (0, 0, POOL_OFF // W_GROUP)),
                  _layer_spec(buf), _layer_spec(wbd), _layer_spec(scale)],
        out_specs=[full((steps, nb, W_GROUP)), _layer_spec(buf)],
        out_shape=[jax.ShapeDtypeStruct((steps, nb, W_GROUP), BF16), _stacked_like(buf)],
        input_output_aliases={1: 1},
        compiler_params=_cp("arbitrary"),
        name="pool_sample",
    )(zs, buf.arr, wbd.arr, scale.arr)


def _rwkv_sample_body(z_ref, sh_ref, s_ref, mu_ref, w0_ref, w2_ref, a0_ref, a2_ref, g2_ref, kk_ref, ka_ref,
                      rk_ref, lg_ref, lb_ref, y_ref, so_ref, sho_ref,
                      st_s, r_s, k_s, v_s, kk_s, b_s, w_s, g_s, o_s, *, steps, nb):
    n_state = N_HEADS * N_C * N_C
    _to_rows(s_ref, st_s, n_state)
    prev = sh_ref[...].T
    for t in range(steps):
        f = z_ref[t].T
        mixed = f + (prev - f) * mu_ref[...]
        prev = f
        k = mixed[256:512]
        low = mixed[768:896]
        w_s[t] = jnp.exp(-_sigmoid(w0_ref[...] + _mm_hi(w2_ref[...], jnp.tanh(low))) * math.exp(-0.5))
        a = _sigmoid(a0_ref[...] + _mm_hi(a2_ref[...], low))
        g_s[t] = _mm_hi(g2_ref[...], _sigmoid(low))
        kk = _per_head(k * kk_ref[...], lambda kh, h: kh * lax.rsqrt(
            jnp.maximum(jnp.sum(kh * kh, axis=0, keepdims=True), 1e-24)))
        r_s[t] = mixed[0:256]
        v_s[t] = mixed[512:768]
        kk_s[t] = kk
        k_s[t] = k * (1.0 + (a - 1.0) * ka_ref[...])
        b_s[t] = kk * a

    def per_value(hv, carry):
        r0 = pl.multiple_of(hv * N_C, N_C)
        hs = pl.ds(pl.multiple_of((hv >> 6) * N_C, N_C), N_C)
        s = st_s[pl.ds(r0, N_C), :]
        for t in range(steps):
            sa = -jnp.sum(s * kk_s[t, hs, :], axis=0, keepdims=True)
            s = s * w_s[t, hs, :] + sa * b_s[t, hs, :] + v_s[t, pl.ds(hv, 1), :] * k_s[t, hs, :]
            o_s[t, pl.ds(hv, 1), :] = jnp.sum(s * r_s[t, hs, :], axis=0, keepdims=True)
        st_s[pl.ds(r0, N_C), :] = s
        return carry

    lax.fori_loop(0, N_HEADS * N_C, per_value, 0)
    for t in range(steps):
        def norm(oh, h):
            mu = jnp.mean(oh, axis=0, keepdims=True)
            xc = oh - mu
            return xc * lax.rsqrt(jnp.mean(xc * xc, axis=0, keepdims=True) + RWKV_LN_EPS)
        on = _per_head(o_s[t], norm) * lg_ref[...] + lb_ref[...]
        rk = r_s[t] * k_s[t] * rk_ref[...]
        v = v_s[t]
        bonus = _per_head(rk, lambda x, h: jnp.sum(x, axis=0, keepdims=True) * v[h * 64:(h + 1) * 64])
        y_ref[t] = ((on + bonus) * g_s[t]).T.astype(BF16)
    _to_cols(st_s, so_ref, n_state)
    sho_ref[...] = z_ref[steps - 1]


def _rwkv_sample(zs, sh, s, p):
    steps, nb, _ = zs.shape
    n_state = N_HEADS * N_C * N_C
    full = lambda shape: pl.BlockSpec(shape, lambda i: (0,) * len(shape))
    blk = pltpu.VMEM((steps, 256, nb), F32)
    params = [p[k + "_c"] for k in _RWKV_PARAMS]
    return pl.pallas_call(
        functools.partial(_rwkv_sample_body, steps=steps, nb=nb),
        grid=(1,),
        in_specs=[pl.BlockSpec((steps, nb, RWKV_W), lambda i: (0, 0, RWKV_OFF // RWKV_W)), _layer_spec(sh),
                  _layer_spec(s)] + [_layer_spec(q) for q in params],
        out_specs=[full((steps, nb, W_GROUP)), _layer_spec(s), _layer_spec(sh)],
        out_shape=[jax.ShapeDtypeStruct((steps, nb, W_GROUP), BF16), _stacked_like(s), _stacked_like(sh)],
        input_output_aliases={2: 1, 1: 2},
        scratch_shapes=[pltpu.VMEM((n_state, nb), F32), blk, blk, blk, blk, blk, blk, blk, blk],
        compiler_params=_cp("arbitrary"),
        name="rwkv_sample",
    )(zs, sh.arr, s.arr, *[q.arr for q in params])


def _mlstm_sample_body(z_ref, zif_ref, cb_ref, c_ref, n_ref, m_ref, cw_ref, cbias_ref, bif_ref, ng_ref,
                       y_ref, co_ref, no_ref, mo_ref, cbo_ref,
                       ct_s, q_s, k_s, v_s, wf_s, den_s, em_s, o_s, *, steps, nb):
    n_state = N_HEADS * D_D * D_D
    _to_rows(c_ref, ct_s, n_state)
    n = n_ref[...].T
    m = m_ref[...].T[0:N_HEADS]
    hist = [cb_ref[:, j * 512:(j + 1) * 512].T for j in range(CONV_D - 1)]
    for t in range(steps):
        hist = hist + [z_ref[t, :, 0:512].T]
        conv = cbias_ref[...]
        for jj in range(CONV_D):
            conv = conv + cw_ref[:, jj:jj + 1] * hist[jj]
        hist = hist[1:]
        act = _silu(conv)
        q = act[0:256] * D_D ** -0.5
        k = act[256:512]
        gates = zif_ref[t].T + bif_ref[...]
        ig = gates[0:N_HEADS]
        lf = _log_sigmoid(gates[N_HEADS:2 * N_HEADS])
        m_new = jnp.maximum(lf + m, ig)
        wf = _expand_heads(jnp.exp(lf + m - m_new), nb)
        kw = _expand_heads(jnp.exp(ig - m_new), nb) * k
        m = m_new
        n = wf * n + kw
        den = _per_head(q * n, lambda x, h: jnp.broadcast_to(jnp.sum(x, axis=0, keepdims=True), (64, nb)))
        q_s[t] = q
        k_s[t] = kw
        v_s[t] = z_ref[t, :, 512:768].T
        wf_s[t] = wf
        den_s[t] = den
        em_s[t] = _expand_heads(jnp.exp(-m_new), nb)
    o_s[...] = jnp.zeros(o_s.shape, F32)

    def per_key(hd, carry):
        r0 = pl.multiple_of(hd * D_D, D_D)
        e0 = pl.multiple_of((hd >> 6) * D_D, D_D)
        c = ct_s[pl.ds(r0, D_D), :]
        for t in range(steps):
            c = wf_s[t, pl.ds(hd, 1), :] * c + k_s[t, pl.ds(hd, 1), :] * v_s[t, pl.ds(e0, D_D), :]
            o_s[t, pl.ds(e0, D_D), :] += q_s[t, pl.ds(hd, 1), :] * c
        ct_s[pl.ds(r0, D_D), :] = c
        return carry

    lax.fori_loop(0, N_HEADS * D_D, per_key, 0)
    for t in range(steps):
        hh = o_s[t] / jnp.maximum(jnp.abs(den_s[t]), em_s[t])
        hn = _per_head(hh, lambda x, h: x * lax.rsqrt(jnp.mean(x * x, axis=0, keepdims=True) + 1e-6))
        y_ref[t] = (_sigmoid(z_ref[t, :, 768:1024].T) * (hn * ng_ref[...])).T.astype(BF16)
    _to_cols(ct_s, co_ref, n_state)
    no_ref[...] = n.T
    mo_ref[...] = jnp.concatenate([m, jnp.zeros((8 - N_HEADS, nb), F32)], axis=0)
    for j in range(CONV_D - 1):
        cbo_ref[:, j * 512:(j + 1) * 512] = hist[j].T


def _mlstm_sample(zs, cb, c, n, m_pad, cw_t, cbias_col, bif_col, ng_col):
    steps, nb, _ = zs.shape
    n_state = N_HEADS * D_D * D_D
    full = lambda shape: pl.BlockSpec(shape, lambda i: (0,) * len(shape))
    blk = pltpu.VMEM((steps, 256, nb), F32)
    return pl.pallas_call(
        functools.partial(_mlstm_sample_body, steps=steps, nb=nb),
        grid=(1,),
        in_specs=[pl.BlockSpec((steps, nb, ML_W), lambda i: (0, 0, ML_OFF // ML_W)),
                  pl.BlockSpec((steps, nb, MLIF_W), lambda i: (0, 0, MLIF_OFF // MLIF_W)),
                  _layer_spec(cb), _layer_spec(c), _layer_spec(n), _layer_spec(m_pad),
                  _layer_spec(cw_t), _layer_spec(cbias_col), _layer_spec(bif_col), _layer_spec(ng_col)],
        out_specs=[full((steps, nb, W_GROUP)), _layer_spec(c), _layer_spec(n), full((8, nb)), _layer_spec(cb)],
        out_shape=[jax.ShapeDtypeStruct((steps, nb, W_GROUP), BF16), _stacked_like(c), _stacked_like(n),
                   jax.ShapeDtypeStruct((8, nb), F32), _stacked_like(cb)],
        input_output_aliases={3: 1, 4: 2, 2: 4},
        scratch_shapes=[pltpu.VMEM((n_state, nb), F32), blk, blk, blk, blk, blk, blk, blk],
        compiler_params=_cp("arbitrary"),
        name="mlstm_sample",
    )(zs, zs, cb.arr, c.arr, n.arr, m_pad.arr, cw_t.arr, cbias_col.arr, bif_col.arr, ng_col.arr)


def _pad_cols(x, n):
    return jnp.pad(x, [(0, 0)] * (x.ndim - 1) + [(0, n - x.shape[-1])])


def _rows_at(x, row0, n_rows):
    return jnp.pad(x, ((0, 0), (row0, n_rows - row0 - x.shape[1]), (0, 0)))


def _stacked_params(w):
    gla, pool, rwkv, ml = jnp.split(w["w_in"], [784, 784 + 256, 784 + 256 + 832], axis=2)
    w_in_p = jnp.concatenate([_pad_cols(gla, GLA_W), _pad_cols(rwkv, RWKV_W), pool, ml[:, :, :1024],
                              _pad_cols(ml[:, :, 1024:], MLIF_W)], axis=2).astype(BF16)
    col = lambda v: v.reshape(DEPTH, -1, 1)
    row = lambda v: v.reshape(DEPTH, 1, -1)
    tr = lambda m: m.transpose(0, 2, 1)
    wa = _rows_at(w["gla_w_a2"], 0, 128)
    w2 = _rows_at(w["rwkv_w2"], 0, 128)
    a2 = _rows_at(w["rwkv_a2"], R_W, 128)
    g2 = _rows_at(w["rwkv_g2"], R_W + R_AA, 128)
    mu = _pad_cols(w["rwkv_mu"], RWKV_W)
    bif = _pad_cols(jnp.concatenate([w["mlstm_b_i"], w["mlstm_b_f"]], axis=1), 128)
    wbd = jnp.zeros((DEPTH, 256, 256), F32)
    for gi in range(4):
        wbd = wbd.at[:, gi * 64:(gi + 1) * 64, gi * 64:(gi + 1) * 64].set(w["pool_w"][:, gi])
    vecs = dict(w0=w["rwkv_w0"], a0=w["rwkv_a0"], k_k=w["rwkv_k_k"], k_a=w["rwkv_k_a"], r_k=w["rwkv_r_k"],
                ln_g=w["rwkv_ln_g"], ln_b=w["rwkv_ln_b"], mu=mu)
    rw = {k: row(v) for k, v in vecs.items()}
    rw.update({k + "_c": col(v) for k, v in vecs.items()})
    rw.update(w2=w2, a2=a2, g2=g2, w2_c=tr(w2), a2_c=tr(a2), g2_c=tr(g2))
    return dict(
        w_in=w_in_p, w_out=w["w_out"].astype(BF16), ln1_g=row(w["ln1_g"]), ln1_b=row(w["ln1_b"]),
        w_up=w["ffn_w_up"].astype(BF16), ffn_cw=w["ffn_conv_w"], ffn_cb=row(w["ffn_conv_b"]),
        w_down=w["ffn_w_down"].astype(BF16), ln2_g=row(w["ln2_g"]), ln2_b=row(w["ln2_b"]),
        gla_wa=wa, gla_wa_t=tr(wa), gla_ba=row(w["gla_b_a"]), gla_ba_c=col(w["gla_b_a"]),
        gla_ng=row(w["gla_norm_g"]), gla_ng_c=col(w["gla_norm_g"]),
        pool_w=wbd, pool_scale=row(w["pool_scale"]), rwkv=rw,
        ml_cw=w["mlstm_conv_w"], ml_cw_t=tr(w["mlstm_conv_w"]), ml_cb=row(w["mlstm_conv_b"]),
        ml_cb_c=col(w["mlstm_conv_b"]), ml_bif=row(bif), ml_bif_c=col(bif),
        ml_ng=row(w["mlstm_norm_g"]), ml_ng_c=col(w["mlstm_norm_g"]))


def _layer_view(stacked, l):
    return {k: (_layer_view(v, l) if isinstance(v, dict) else _Layer(v, l)) for k, v in stacked.items()}


RWKV_BLOCK = 512
GLA_BLOCK = 512
MLSTM_BLOCK = 256
POOL_BLOCK = 1024
DENSE_TILE = 512
FFN_TILE = 256


def _prompt_layer(x, p):
    bsz, seq, d = x.shape
    z = _linear(x.reshape(bsz * seq, d), p["w_in"], DENSE_TILE).reshape(bsz, seq, ZC)
    zeros = lambda *s: jnp.zeros((bsz,) + s, F32)
    y_a, s_gla = _gla_prompt(z, zeros(N_HEADS, DK_A, DV_A), p["gla_wa"], p["gla_ba"], p["gla_ng"], GLA_BLOCK)
    y_b, s_pool = _pool_prompt(z, zeros(POOL_BUF, W_GROUP), p["pool_w"], p["pool_scale"], POOL_BLOCK, 0)
    y_c, s_rwkv, s_shift = _rwkv_prompt(z, zeros(1, RWKV_W), zeros(N_HEADS, N_C, N_C), p["rwkv"], RWKV_BLOCK)
    y_d, s_c, s_n, s_m, s_conv = _mlstm_prompt(z, zeros(CONV_D - 1, 512), zeros(N_HEADS, D_D, D_D), zeros(1, 256),
                                               zeros(1, 256), p["ml_cw"], p["ml_cb"], p["ml_bif"], p["ml_ng"],
                                               MLSTM_BLOCK)
    x2, s_ffn = _ffn_prompt([y_a, y_b, y_c, y_d], x, zeros(FFN_CONV - 1, 2 * D_FF), p, FFN_TILE)
    states = (s_gla, s_pool, s_rwkv, s_shift[:, :, :RWKV_COLS], s_c, s_n.reshape(bsz, N_HEADS, D_D),
              s_m[:, 0, ::D_D], s_conv, s_ffn)
    return x2, states


def _sample_layer(x, st, p, steps, nb):
    s_gla, s_pool, s_rwkv, s_shift, s_c, s_n, s_m, s_conv, s_ffn = st
    z = _linear(x, p["w_in"], steps * nb).reshape(steps, nb, ZC)
    y_a, n_gla = _gla_sample(z, s_gla, p["gla_wa_t"], p["gla_ba_c"], p["gla_ng_c"])
    y_b, n_pool = _pool_sample(z, s_pool, p["pool_w"], p["pool_scale"], PAST_LEN)
    y_c, n_rwkv, n_shift = _rwkv_sample(z, s_shift, s_rwkv, p["rwkv"])
    y_d, n_c, n_n, n_m, n_conv = _mlstm_sample(z, s_conv, s_c, s_n, s_m, p["ml_cw_t"], p["ml_cb_c"], p["ml_bif_c"],
                                               p["ml_ng_c"])
    flat = lambda y: y.reshape(steps * nb, W_GROUP)
    x2, n_ffn = _ffn_sample([flat(y_a), flat(y_b), flat(y_c), flat(y_d)], x, s_ffn, p, nb, steps)
    return x2, (n_gla, n_pool, n_rwkv, n_shift, n_c, n_n, n_m, n_conv, n_ffn)


def kernel(x_prompt, x_sample, state_gla, state_pool, state_rwkv, state_rwkv_shift, state_mlstm_c, state_mlstm_n, state_mlstm_m, state_mlstm_conv, state_ffn_conv, w_in, gla_w_a2, gla_b_a, gla_norm_g, pool_w, pool_scale, rwkv_mu, rwkv_w0, rwkv_w2, rwkv_a0, rwkv_a2, rwkv_g2, rwkv_k_k, rwkv_k_a, rwkv_r_k, rwkv_ln_g, rwkv_ln_b, mlstm_conv_w, mlstm_conv_b, mlstm_b_i, mlstm_b_f, mlstm_norm_g, w_out, ln1_g, ln1_b, ffn_w_up, ffn_conv_w, ffn_conv_b, ffn_w_down, ln2_g, ln2_b):
    w = dict(w_in=w_in, gla_w_a2=gla_w_a2, gla_b_a=gla_b_a, gla_norm_g=gla_norm_g, pool_w=pool_w,
             pool_scale=pool_scale, rwkv_mu=rwkv_mu, rwkv_w0=rwkv_w0, rwkv_w2=rwkv_w2, rwkv_a0=rwkv_a0,
             rwkv_a2=rwkv_a2, rwkv_g2=rwkv_g2, rwkv_k_k=rwkv_k_k, rwkv_k_a=rwkv_k_a, rwkv_r_k=rwkv_r_k,
             rwkv_ln_g=rwkv_ln_g, rwkv_ln_b=rwkv_ln_b, mlstm_conv_w=mlstm_conv_w, mlstm_conv_b=mlstm_conv_b,
             mlstm_b_i=mlstm_b_i, mlstm_b_f=mlstm_b_f, mlstm_norm_g=mlstm_norm_g, w_out=w_out, ln1_g=ln1_g,
             ln1_b=ln1_b, ffn_w_up=ffn_w_up, ffn_conv_w=ffn_conv_w, ffn_conv_b=ffn_conv_b, ffn_w_down=ffn_w_down,
             ln2_g=ln2_g, ln2_b=ln2_b)
    sample_states = (state_gla, state_pool, state_rwkv, state_rwkv_shift, state_mlstm_c, state_mlstm_n,
                     state_mlstm_m, state_mlstm_conv, state_ffn_conv)
    nb, steps, d = x_sample.shape
    flat_in = [s.reshape(DEPTH, nb, -1) for s in sample_states]
    flat_in[3] = _pad_cols(flat_in[3], RWKV_W)
    flat_in[6] = _pad_cols(flat_in[6], 128)
    stacked = _stacked_params(w)
    yp = x_prompt
    ys = x_sample.transpose(1, 0, 2).reshape(steps * nb, d)
    acc_p = [[] for _ in sample_states]
    m_new = []
    flat = list(flat_in)
    for l in range(DEPTH):
        p = _layer_view(stacked, l)
        yp, st_p = _prompt_layer(yp, p)
        ys, st_s = _sample_layer(ys, tuple(_Layer(s, l) for s in flat), p, steps, nb)
        for i in range(len(sample_states)):
            acc_p[i].append(st_p[i])
            if i == 6:
                m_new.append(st_s[i])
            else:
                flat[i] = st_s[i]
    ys = ys.reshape(steps, nb, d).transpose(1, 0, 2)
    out_s = list(flat)
    out_s[3] = out_s[3][:, :, :RWKV_COLS]
    out_s[6] = jnp.stack(m_new)[:, 0:N_HEADS, :].transpose(0, 2, 1)
    out = [yp, ys]
    for sp, ss, ref in zip(acc_p, out_s, sample_states):
        out.append(jnp.stack(sp))
        out.append(ss.reshape(ref.shape))
    return tuple(out)
```

```python
import functools
import math

import jax
import jax.numpy as jnp
from jax import lax
from jax.experimental import pallas as pl
from jax.experimental.pallas import tpu as pltpu

F32 = jnp.float32
BF16 = jnp.bfloat16
HI = lax.Precision.HIGHEST

D_MODEL = 1024
DEPTH = 4
PAST_LEN = 16384
W_GROUP = 256
N_HEADS = 4
DK_A = 32
DV_A = 64
R_GLA = 16
GLA_TAU = 16.0
POOL_WINDOWS = (2, 4, 8, 16)
POOL_BUF = 15
N_C = 64
R_W, R_AA, R_G = 16, 16, 32
RWKV_COLS = 832
RWKV_LN_EPS = 64e-5
D_D = 64
CONV_D = 4
D_FF = 2816
FFN_CONV = 3
ALPHA = (2 * DEPTH) ** 0.25
CHUNK = 64

GLA_W = 896
RWKV_W = 896
GLA_OFF, RWKV_OFF, POOL_OFF, ML_OFF, MLIF_OFF = 0, 896, 1792, 2048, 3072
ZC = 3200
ML_W = 1024
MLIF_W = 128

VMEM_LIMIT = 56 * 1024 * 1024


def _cp(*sem):
    return pltpu.CompilerParams(dimension_semantics=sem, vmem_limit_bytes=VMEM_LIMIT)


def _mm(a, b):
    return jnp.dot(a.astype(BF16), b.astype(BF16), preferred_element_type=F32)


def _mm_nt(a, b):
    return lax.dot_general(a.astype(BF16), b.astype(BF16), (((1,), (1,)), ((), ())), preferred_element_type=F32)


def _mm_tn(a, b):
    return lax.dot_general(a.astype(BF16), b.astype(BF16), (((0,), (0,)), ((), ())), preferred_element_type=F32)


def _mm_hi(a, b):
    return jnp.dot(a, b, preferred_element_type=F32, precision=HI)


def _split2(x):
    hi = x.astype(BF16)
    return hi, (x - hi.astype(F32)).astype(BF16)


def _mm_xc(x, c2):
    hi, mid = _split2(x)
    return jnp.dot(jnp.concatenate([hi, mid], axis=1), c2, preferred_element_type=F32)


def _mm_cx(c2, x):
    hi, mid = _split2(x)
    return jnp.dot(c2, jnp.concatenate([hi, mid], axis=0), preferred_element_type=F32)


def _twice(c, axis):
    c = c.astype(F32).astype(BF16)
    return jnp.concatenate([c, c], axis=axis)


def _iota(shape, dim):
    return lax.broadcasted_iota(jnp.int32, shape, dim)


def _sigmoid(x):
    return 1.0 / (1.0 + jnp.exp(-x))


def _silu(x):
    return x * _sigmoid(x)


def _log_sigmoid(x):
    return jnp.minimum(x, 0.0) - jnp.log(1.0 + jnp.exp(-jnp.abs(x)))


def _tile4(x):
    return jnp.concatenate([x, x, x, x], axis=0)


def _block_diag_mask(rows, cols, rshift, cshift):
    return (_iota((rows, cols), 0) >> rshift) == (_iota((rows, cols), 1) >> cshift)


def _layer_norm(h, g, b, eps):
    mu = jnp.mean(h, axis=-1, keepdims=True)
    xc = h - mu
    var = jnp.mean(xc * xc, axis=-1, keepdims=True)
    return xc * lax.rsqrt(var + eps) * g + b


class _Layer:
    def __init__(self, arr, l):
        self.arr, self.l, self.shape = arr, l, tuple(arr.shape[1:])


def _layer_spec(p, single_buffer=False):
    n, l = len(p.shape), p.l
    if single_buffer:
        return pl.BlockSpec((None,) + p.shape, lambda *g: (l,) + (0,) * n, pipeline_mode=pl.Buffered(1))
    return pl.BlockSpec((None,) + p.shape, lambda *g: (l,) + (0,) * n)


def _stacked_like(p):
    return jax.ShapeDtypeStruct(p.arr.shape, p.arr.dtype)


def _linear_body(x_ref, w_ref, o_ref):
    o_ref[...] = jnp.dot(x_ref[...].astype(BF16), w_ref[...], preferred_element_type=F32)


def _linear(x, w, tm):
    t, k = x.shape
    n = w.shape[1]
    return pl.pallas_call(
        _linear_body,
        grid=(t // tm,),
        in_specs=[pl.BlockSpec((tm, k), lambda i: (i, 0)), _layer_spec(w, single_buffer=True)],
        out_specs=pl.BlockSpec((tm, n), lambda i: (i, 0)),
        out_shape=jax.ShapeDtypeStruct((t, n), F32),
        compiler_params=_cp("arbitrary"),
        name="in_proj",
    )(x, w.arr)


def _outproj(ys, x, w_ref, g_ref, b_ref):
    y = jnp.concatenate(ys, axis=1).astype(BF16)
    mixed = jnp.dot(y, w_ref[...], preferred_element_type=F32)
    return _layer_norm(ALPHA * x + mixed, g_ref[...], b_ref[...], 1e-5)


FF_CHUNK = 256
N_FF_CHUNKS = D_FF // FF_CHUNK
UP_AHEAD = 2
DOWN_ROWS = 256


def _ffn_prompt_body(ya_ref, yb_ref, yc_ref, yd_ref, x_ref, buf_ref, wo_ref, g1_ref, b1_ref,
                     wup_ref, cw_ref, cb_ref, wdn_ref, g_ref, b_ref,
                     o_ref, st_ref, carry_ref, h_ref, x1_ref, *, tm, n_tiles):
    j = pl.program_id(1)

    @pl.when(j == 0)
    def _():
        carry_ref[...] = buf_ref[0]

    x = _outproj([ya_ref[0], yb_ref[0], yc_ref[0], yd_ref[0]], x_ref[0], wo_ref, g1_ref, b1_ref)
    x1_ref[...] = x
    xb = x.astype(BF16)
    row = _iota((8, FF_CHUNK), 0)
    col = lambda c, half: slice(half * D_FF + c * FF_CHUNK, half * D_FF + (c + 1) * FF_CHUNK)
    up = lambda c: [jnp.dot(xb, wup_ref[:, col(c, half)], preferred_element_type=F32) for half in range(2)]
    ahead = [up(c) for c in range(UP_AHEAD)]
    for c in range(N_FF_CHUNKS):
        if c + UP_AHEAD < N_FF_CHUNKS:
            ahead.append(up(c + UP_AHEAD))
        u_now = ahead.pop(0)
        acts = []
        for half in range(2):
            sl = col(c, half)
            u = u_now[half]
            two_back, one_back = carry_ref[0:1, sl], carry_ref[1:2, sl]
            w0, w1, w2 = cw_ref[0:1, sl], cw_ref[1:2, sl], cw_ref[2:3, sl]
            r1 = pltpu.roll(u, 1, 0)
            r2 = pltpu.roll(u, 2, 0)
            r1_head = jnp.where(row == 0, one_back, r1[0:8])
            r2_head = jnp.where(row == 0, two_back, jnp.where(row == 1, one_back, r2[0:8]))
            head = cb_ref[:, sl] + w0 * r2_head + w1 * r1_head + w2 * u[0:8]
            body = cb_ref[:, sl] + w0 * r2 + w1 * r1 + w2 * u
            carry_ref[:, sl] = u[tm - 2:tm]
            acts.append(jnp.concatenate([head, body[8:]], axis=0))
        h_ref[:, c * FF_CHUNK:(c + 1) * FF_CHUNK] = (_silu(acts[0]) * acts[1]).astype(BF16)
    for r0 in range(0, tm, DOWN_ROWS):
        rows = slice(r0, r0 + DOWN_ROWS)
        f = jnp.dot(h_ref[rows, :], wdn_ref[...], preferred_element_type=F32)
        o_ref[0, rows, :] = _layer_norm(ALPHA * x1_ref[rows, :] + f, g_ref[...], b_ref[...], 1e-5)

    @pl.when(j == n_tiles - 1)
    def _():
        st_ref[0] = carry_ref[...]


_DENSE_PARAMS = ("w_out", "ln1_g", "ln1_b", "w_up", "ffn_cw", "ffn_cb", "w_down", "ln2_g", "ln2_b")


def _ffn_prompt(ys, x3, buf, p, tm):
    bsz, seq, d = x3.shape
    n_tiles = seq // tm
    params = [p[k] for k in _DENSE_PARAMS]
    tok = lambda w: pl.BlockSpec((1, tm, w), lambda i, j: (i, j, 0))
    st = pl.BlockSpec((1, 2, 2 * D_FF), lambda i, j: (i, 0, 0))
    return pl.pallas_call(
        functools.partial(_ffn_prompt_body, tm=tm, n_tiles=n_tiles),
        grid=(bsz, n_tiles),
        in_specs=[tok(W_GROUP)] * 4 + [tok(d), st] + [_layer_spec(q, single_buffer=True) for q in params],
        out_specs=[tok(d), st],
        out_shape=[jax.ShapeDtypeStruct((bsz, seq, d), F32), jax.ShapeDtypeStruct((bsz, 2, 2 * D_FF), F32)],
        scratch_shapes=[pltpu.VMEM((2, 2 * D_FF), F32), pltpu.VMEM((tm, D_FF), BF16), pltpu.VMEM((tm, d), F32)],
        compiler_params=_cp("arbitrary", "arbitrary"),
        name="ffn_prompt",
    )(*ys, x3, buf, *[q.arr for q in params])


def _ffn_sample_body(ya_ref, yb_ref, yc_ref, yd_ref, x_ref, buf_ref, wo_ref, g1_ref, b1_ref,
                     wup_ref, cw_ref, cb_ref, wdn_ref, g_ref, b_ref,
                     o_ref, st_ref, h_ref, *, nb, steps):
    t = nb * steps
    x = _outproj([ya_ref[...], yb_ref[...], yc_ref[...], yd_ref[...]], x_ref[...], wo_ref, g1_ref, b1_ref)
    xb = x.astype(BF16)
    for c in range(N_FF_CHUNKS):
        acts = []
        for half in range(2):
            lo = half * D_FF + c * FF_CHUNK
            sl = slice(lo, lo + FF_CHUNK)
            u = jnp.dot(xb, wup_ref[:, sl], preferred_element_type=F32)
            b0 = buf_ref[:, lo:lo + FF_CHUNK]
            b1 = buf_ref[:, 2 * D_FF + lo:2 * D_FF + lo + FF_CHUNK]
            back1 = jnp.concatenate([b1, u[0:t - nb]], axis=0)
            back2 = jnp.concatenate([b0, b1, u[0:t - 2 * nb]], axis=0)
            conv = cb_ref[:, sl] + cw_ref[0:1, sl] * back2 + cw_ref[1:2, sl] * back1 + cw_ref[2:3, sl] * u
            st_ref[:, lo:lo + FF_CHUNK] = u[t - 2 * nb:t - nb]
            st_ref[:, 2 * D_FF + lo:2 * D_FF + lo + FF_CHUNK] = u[t - nb:t]
            acts.append(conv)
        h_ref[:, c * FF_CHUNK:(c + 1) * FF_CHUNK] = (_silu(acts[0]) * acts[1]).astype(BF16)
    f = jnp.dot(h_ref[...], wdn_ref[...], preferred_element_type=F32)
    o_ref[...] = _layer_norm(ALPHA * x + f, g_ref[...], b_ref[...], 1e-5)


def _ffn_sample(ys, x, buf, p, nb, steps):
    t, d = x.shape
    params = [p[k] for k in _DENSE_PARAMS]
    full = lambda shape: pl.BlockSpec(shape, lambda i: (0,) * len(shape))
    return pl.pallas_call(
        functools.partial(_ffn_sample_body, nb=nb, steps=steps),
        grid=(1,),
        in_specs=[full((t, W_GROUP))] * 4 + [full((t, d)), _layer_spec(buf)] + [_layer_spec(q) for q in params],
        out_specs=[full((t, d)), _layer_spec(buf)],
        out_shape=[jax.ShapeDtypeStruct((t, d), F32), _stacked_like(buf)],
        input_output_aliases={5: 1},
        scratch_shapes=[pltpu.VMEM((t, D_FF), BF16)],
        compiler_params=_cp("arbitrary"),
        name="ffn_sample",
    )(*ys, x, buf.arr, *[q.arr for q in params])


def _exp_masks():
    t = _iota((CHUNK, 256), 0)
    s = _iota((CHUNK, 256), 1) & (CHUNK - 1)
    return t >= s, t > s, t == s


def _bd256():
    return _block_diag_mask(256, 256, 6, 6)


def _tril_blocks(n):
    r = _iota((n, n), 0)
    c = _iota((n, n), 1)
    return ((r >= c) & ((r >> 6) == (c >> 6))).astype(F32)


def _mask_bf16(mask):
    return mask.astype(F32).astype(BF16)


def _bdx(x, mask01):
    return _tile4(x.astype(BF16)) * mask01


def _put_block_diag(dst_ref, blocks, rows, cols):
    dst_ref[...] = jnp.zeros(dst_ref.shape, F32)
    for h in range(N_HEADS):
        dst_ref[h * rows:(h + 1) * rows, h * cols:(h + 1) * cols] = blocks[h]


def _get_block_diag(src_ref, out_ref, rows, cols):
    for h in range(N_HEADS):
        out_ref[0, h] = src_ref[h * rows:(h + 1) * rows, h * cols:(h + 1) * cols]


def _gla_prompt_body(z_ref, s0_ref, wa_ref, ba_ref, ng_ref, y_ref, so_ref, sbd_ref, *, n_chunks, n_blocks):
    j = pl.program_id(1)

    @pl.when(j == 0)
    def _():
        _put_block_diag(sbd_ref, s0_ref[0], DK_A, DV_A)

    lb = n_chunks * CHUNK
    causal, _, _ = _exp_masks()
    bd_state = _block_diag_mask(128, 256, 5, 6)
    bd_k = _mask_bf16(_block_diag_mask(256, 128, 6, 5))
    bd_v = _mask_bf16(_bd256())
    pavg2 = _twice(bd_v.astype(F32) * (1.0 / DV_A), 0)

    lg = _log_sigmoid(_mm(z_ref[0, :, 768:896], wa_ref[...]) + ba_ref[...]) * (1.0 / GLA_TAU)
    bc_all = _mm_cx(_twice(_tril_blocks(lb), 1), lg)

    o_intra, q_dec, incr, decay = [], [], [], []
    for c in range(n_chunks):
        rows = slice(c * CHUNK, (c + 1) * CHUNK)
        q = z_ref[0, rows, 0:128] * DK_A ** -0.5
        k = z_ref[0, rows, 128:256]
        v = z_ref[0, rows, 256:512]
        bc = bc_all[rows]
        bl = bc[CHUNK - 1:CHUNK]
        rho = bc[CHUNK // 2 - 1:CHUNK // 2]
        att = _mm_nt(q * jnp.exp(bc - rho), _bdx(k * jnp.exp(rho - bc), bd_k))
        o_intra.append(_mm(jnp.where(causal, att, 0.0), _bdx(v, bd_v)))
        q_dec.append(q * jnp.exp(bc))
        incr.append(jnp.where(bd_state, _mm_tn(k * jnp.exp(bl - bc), v), 0.0))
        decay_col = jnp.broadcast_to(jnp.exp(bl), (128, 128)).T
        decay.append(jnp.concatenate([decay_col, decay_col], axis=1))
    s = sbd_ref[...]
    o = []
    for c in range(n_chunks):
        o.append(_mm(q_dec[c], s) + o_intra[c])
        s = decay[c] * s + incr[c]
    sbd_ref[...] = s
    o = jnp.concatenate(o, axis=0)
    ms = _mm_xc(o * o, pavg2)
    y_ref[0] = (o * lax.rsqrt(ms + 1e-6) * ng_ref[...] * _silu(z_ref[0, :, 512:768])).astype(BF16)

    @pl.when(j == n_blocks - 1)
    def _():
        _get_block_diag(sbd_ref, so_ref, DK_A, DV_A)


def _gla_prompt(z3, s0, wa, ba, ng, lb):
    bsz, seq, _ = z3.shape
    n_blocks = seq // lb
    st = pl.BlockSpec((1, N_HEADS, DK_A, DV_A), lambda i, j: (i, 0, 0, 0))
    return pl.pallas_call(
        functools.partial(_gla_prompt_body, n_chunks=lb // CHUNK, n_blocks=n_blocks),
        grid=(bsz, n_blocks),
        in_specs=[pl.BlockSpec((1, lb, GLA_W), lambda i, j: (i, j, GLA_OFF // GLA_W)), st,
                  _layer_spec(wa), _layer_spec(ba), _layer_spec(ng)],
        out_specs=[pl.BlockSpec((1, lb, W_GROUP), lambda i, j: (i, j, 0)), st],
        out_shape=[jax.ShapeDtypeStruct((bsz, seq, W_GROUP), BF16),
                   jax.ShapeDtypeStruct((bsz, N_HEADS, DK_A, DV_A), F32)],
        scratch_shapes=[pltpu.VMEM((128, 256), F32)],
        compiler_params=_cp("arbitrary", "arbitrary"),
        name="gla_prompt",
    )(z3, s0, wa.arr, ba.arr, ng.arr)


def _pool_prompt_body(z_ref, buf_ref, w_ref, sc_ref, y_ref, bo_ref, hist_ref, *, lb, n_blocks, start_pos):
    j = pl.program_id(1)

    @pl.when(j == 0)
    def _():
        hist_ref[0:1, :] = jnp.zeros((1, W_GROUP), F32)
        hist_ref[1:16, :] = buf_ref[0]

    zp = z_ref[0]
    e = jnp.concatenate([hist_ref[...], zp], axis=0)
    s2 = e + pltpu.roll(e, 1, 0)
    s4 = s2 + pltpu.roll(s2, 2, 0)
    s8 = s4 + pltpu.roll(s4, 4, 0)
    s16 = s8 + pltpu.roll(s8, 8, 0)
    lane = _iota((lb, W_GROUP), 1)
    win = jnp.where(lane < 64, s2[16:], jnp.where(lane < 128, s4[16:], jnp.where(lane < 192, s8[16:], s16[16:])))
    width = jnp.where(lane < 64, 2, jnp.where(lane < 128, 4, jnp.where(lane < 192, 8, 16)))
    pos = start_pos + j * lb + _iota((lb, W_GROUP), 0)
    cnt = jnp.minimum(pos + 1, width).astype(F32)
    y_ref[0] = (_mm(win / cnt - zp, w_ref[...]) * sc_ref[...]).astype(BF16)
    hist_ref[1:16, :] = e[lb + 1:lb + 16]

    @pl.when(j == n_blocks - 1)
    def _():
        bo_ref[0] = hist_ref[1:16, :]


def _pool_prompt(z3, buf, wbd, scale, lb, start_pos):
    bsz, seq, _ = z3.shape
    n_blocks = seq // lb
    st = pl.BlockSpec((1, POOL_BUF, W_GROUP), lambda i, j: (i, 0, 0))
    return pl.pallas_call(
        functools.partial(_pool_prompt_body, lb=lb, n_blocks=n_blocks, start_pos=start_pos),
        grid=(bsz, n_blocks),
        in_specs=[pl.BlockSpec((1, lb, W_GROUP), lambda i, j: (i, j, POOL_OFF // W_GROUP)), st,
                  _layer_spec(wbd), _layer_spec(scale)],
        out_specs=[pl.BlockSpec((1, lb, W_GROUP), lambda i, j: (i, j, 0)), st],
        out_shape=[jax.ShapeDtypeStruct((bsz, seq, W_GROUP), BF16),
                   jax.ShapeDtypeStruct((bsz, POOL_BUF, W_GROUP), F32)],
        scratch_shapes=[pltpu.VMEM((16, W_GROUP), F32)],
        compiler_params=_cp("arbitrary", "arbitrary"),
        name="pool_prompt",
    )(z3, buf, wbd.arr, scale.arr)


def _rwkv_prompt_body(z_ref, sh0_ref, s0_ref, mu_ref, w0_ref, w2_ref, a0_ref, a2_ref, g2_ref, kk_ref, ka_ref,
                      rk_ref, lg_ref, lb_ref, y_ref, so_ref, sho_ref,
                      nbd_ref, carry_ref, r_s, k_s, v_s, kk_s, b_s, lw_s, g_s, *, n_chunks, n_blocks, lb):
    j = pl.program_id(1)

    @pl.when(j == 0)
    def _():
        _put_block_diag(nbd_ref, s0_ref[0], N_C, N_C)
        carry_ref[...] = sh0_ref[0]

    bd = _bd256()
    bd_m = _mask_bf16(bd)
    ones2 = _twice(bd, 0)
    pavg2 = _twice(bd.astype(F32) * (1.0 / N_C), 0)

    f = z_ref[0]
    prev = jnp.concatenate([carry_ref[...], f[:-1]], axis=0)
    mixed = f + (prev - f) * mu_ref[...]
    carry_ref[...] = f[lb - 1:lb]
    r = mixed[:, 0:256]
    k = mixed[:, 256:512]
    v = mixed[:, 512:768]
    low = mixed[:, 768:896]
    lw_all = -_sigmoid(w0_ref[...] + _mm(jnp.tanh(low), w2_ref[...])) * math.exp(-0.5)
    lw_s[...] = lw_all
    a = _sigmoid(a0_ref[...] + _mm(low, a2_ref[...]))
    g_s[...] = _mm(_sigmoid(low), g2_ref[...])
    kk = k * kk_ref[...]
    kk = kk * lax.rsqrt(jnp.maximum(_mm_xc(kk * kk, ones2), 1e-24))
    r_s[...] = r
    v_s[...] = v
    kk_s[...] = kk
    k_s[...] = k * (1.0 + (a - 1.0) * ka_ref[...])
    b_s[...] = kk * a
    gc_all = _mm_cx(_twice(_tril_blocks(lb), 1), lw_all)

    lower, strict, eye_exp = _exp_masks()
    eye_f = eye_exp.astype(F32)

    cs = range(n_chunks)
    sl = [slice(c * CHUNK, (c + 1) * CHUNK) for c in cs]
    gc = [gc_all[s] for s in sl]
    kt = [kk_s[sl[c], :] * jnp.exp(gc[c] - lw_s[sl[c], :]) for c in cs]
    rt = [r_s[sl[c], :] * jnp.exp(gc[c]) for c in cs]
    a_kk, a_kb, a_rk, a_rb = [], [], [], []
    for c in cs:
        einv = jnp.exp(-gc[c])
        lhs = jnp.concatenate([kt[c], rt[c]], axis=0)
        rhs = jnp.concatenate([_bdx(k_s[sl[c], :] * einv, bd_m), _bdx(b_s[sl[c], :] * einv, bd_m)], axis=0)
        aa = _mm_nt(lhs, rhs)
        a_kk.append(jnp.where(strict, aa[0:CHUNK, 0:256], 0.0))
        a_kb.append(jnp.where(strict, aa[0:CHUNK, 256:512], 0.0))
        a_rk.append(jnp.where(lower, aa[CHUNK:, 0:256], 0.0))
        a_rb.append(jnp.where(lower, aa[CHUNK:, 256:512], 0.0))
    t = [eye_f - a_kb[c] for c in cs]
    p = [_mm(a_kb[c], _bdx(a_kb[c], bd_m)) for c in cs]
    for _ in range(4):
        m = [_mm(jnp.concatenate([p[c], t[c]], axis=0), _bdx(p[c], bd_m)) for c in cs]
        p = [m[c][0:CHUNK] for c in cs]
        t = [t[c] + m[c][CHUNK:] for c in cs]
    t = [t[c] + _mm(t[c], _bdx(p[c], bd_m)) for c in cs]
    av = [_mm(jnp.concatenate([a_kk[c], a_rk[c]], axis=0), _bdx(v_s[sl[c], :], bd_m)) for c in cs]
    tu = [_mm(t[c], jnp.concatenate([_bdx(av[c][0:CHUNK], bd_m), _bdx(kt[c], bd_m)], axis=1)) for c in cs]
    u0 = [tu[c][:, 0:256] for c in cs]
    tk = [tu[c][:, 256:512] for c in cs]
    ro = [_mm(a_rb[c], jnp.concatenate([_bdx(tk[c], bd_m), _bdx(u0[c], bd_m)], axis=1)) for c in cs]
    r_eff = [rt[c] - ro[c][:, 0:256] for c in cs]
    o0 = [av[c][CHUNK:] - ro[c][:, 256:512] for c in cs]
    a_st, b_st, decay = [], [], []
    for c in cs:
        gl = gc[c][CHUNK - 1:CHUNK]
        dk = jnp.exp(gl - gc[c])
        kd = k_s[sl[c], :] * dk
        bdk = b_s[sl[c], :] * dk
        a_st.append(jnp.where(bd, -_mm_tn(tk[c], bdk), 0.0))
        b_st.append(jnp.where(bd, _mm_tn(jnp.concatenate([v_s[sl[c], :], -u0[c]], axis=0),
                                         jnp.concatenate([kd, bdk], axis=0)), 0.0))
        decay.append(jnp.exp(gl))
    n = nbd_ref[...]
    o = []
    for c in cs:
        o.append(_mm_nt(r_eff[c], n) + o0[c])
        n = n * decay[c] + _mm(n, a_st[c]) + b_st[c]
    nbd_ref[...] = n
    o = jnp.concatenate(o, axis=0)
    mu = _mm_xc(o, pavg2)
    xc = o - mu
    var = _mm_xc(xc * xc, pavg2)
    on = xc * lax.rsqrt(var + RWKV_LN_EPS) * lg_ref[...] + lb_ref[...]
    bonus = _mm_xc(r_s[...] * k_s[...] * rk_ref[...], ones2) * v_s[...]
    y_ref[0] = ((on + bonus) * g_s[...]).astype(BF16)

    @pl.when(j == n_blocks - 1)
    def _():
        _get_block_diag(nbd_ref, so_ref, N_C, N_C)
        sho_ref[0] = carry_ref[...]


_RWKV_PARAMS = ("mu", "w0", "w2", "a0", "a2", "g2", "k_k", "k_a", "r_k", "ln_g", "ln_b")


def _rwkv_prompt(z3, sh0, s0, p, lb):
    bsz, seq, _ = z3.shape
    n_blocks = seq // lb
    st = pl.BlockSpec((1, N_HEADS, N_C, N_C), lambda i, j: (i, 0, 0, 0))
    sh = pl.BlockSpec((1, 1, RWKV_W), lambda i, j: (i, 0, 0))
    blk = pltpu.VMEM((lb, 256), F32)
    params = [p[k] for k in _RWKV_PARAMS]
    return pl.pallas_call(
        functools.partial(_rwkv_prompt_body, n_chunks=lb // CHUNK, n_blocks=n_blocks, lb=lb),
        grid=(bsz, n_blocks),
        in_specs=[pl.BlockSpec((1, lb, RWKV_W), lambda i, j: (i, j, RWKV_OFF // RWKV_W)), sh, st]
        + [_layer_spec(q) for q in params],
        out_specs=[pl.BlockSpec((1, lb, W_GROUP), lambda i, j: (i, j, 0)), st, sh],
        out_shape=[jax.ShapeDtypeStruct((bsz, seq, W_GROUP), BF16),
                   jax.ShapeDtypeStruct((bsz, N_HEADS, N_C, N_C), F32),
                   jax.ShapeDtypeStruct((bsz, 1, RWKV_W), F32)],
        scratch_shapes=[pltpu.VMEM((256, 256), F32), pltpu.VMEM((1, RWKV_W), F32),
                        blk, blk, blk, blk, blk, blk, blk],
        compiler_params=_cp("arbitrary", "arbitrary"),
        name="rwkv_prompt",
    )(z3, sh0, s0, *[q.arr for q in params])


def _mlstm_prompt_body(z_ref, zif_ref, cb0_ref, c0_ref, n0_ref, m0_ref, cw_ref, cbias_ref, bif_ref, ng_ref,
                       y_ref, co_ref, no_ref, mo_ref, cbo_ref,
                       cbd_ref, n_ref, m_ref, carry_ref, qk_s, *, n_chunks, n_blocks, lb):
    j = pl.program_id(1)

    @pl.when(j == 0)
    def _():
        _put_block_diag(cbd_ref, c0_ref[0], D_D, D_D)
        n_ref[...] = n0_ref[0]
        m_ref[...] = m0_ref[0]
        carry_ref[...] = cb0_ref[0]

    qk = z_ref[0, :, 0:512]
    conv = cbias_ref[...] + cw_ref[CONV_D - 1:CONV_D, :] * qk
    head = cbias_ref[...] + cw_ref[CONV_D - 1:CONV_D, :] * qk[0:8]
    row = _iota((8, 512), 0)
    for back in range(1, CONV_D):
        w = cw_ref[CONV_D - 1 - back:CONV_D - back, :]
        rolled = pltpu.roll(qk, back, 0)
        conv = conv + w * rolled
        fixed = rolled[0:8]
        for r in range(back):
            fixed = jnp.where(row == r, carry_ref[CONV_D - 1 - back + r:CONV_D - back + r, :], fixed)
        head = head + w * fixed
    qk_s[...] = _silu(jnp.concatenate([head, conv[8:]], axis=0))
    carry_ref[...] = qk[lb - (CONV_D - 1):lb]

    bd = _bd256()
    bd_m = _mask_bf16(bd)
    ones2 = _twice(bd, 0)
    pavg2 = _twice(bd.astype(F32) * (1.0 / D_D), 0)
    causal, _, eye_exp = _exp_masks()
    er = _iota((128, 512), 0)
    ec = _iota((128, 512), 1)
    expand2 = _twice(((ec < 256) & (er == (ec >> 6))) | ((ec >= 256) & (er == 4 + ((ec - 256) >> 6))), 0)
    is_f = (_iota((lb, 128), 1) >= 4) & (_iota((lb, 128), 1) < 8)
    lane = _iota((CHUNK, 256), 1)

    gates = zif_ref[0] + bif_ref[...]
    gates = jnp.where(is_f, _log_sigmoid(gates), gates)
    gates = jnp.where(is_f, _mm_cx(_twice(_tril_blocks(lb), 1), gates), gates)
    ge_all = _mm_xc(gates, expand2)

    for c in range(n_chunks):
        rows = slice(c * CHUNK, (c + 1) * CHUNK)
        q = qk_s[rows, 0:256] * D_D ** -0.5
        k = qk_s[rows, 256:512]
        v = z_ref[0, rows, 512:768]
        og = z_ref[0, rows, 768:1024]
        iexp = ge_all[rows, 0:256]
        fexp = ge_all[rows, 256:512]
        d_row = jnp.sum(jnp.where(eye_exp, iexp - fexp, 0.0), axis=0, keepdims=True)
        dlog = jnp.where(causal, fexp + d_row, -jnp.inf)
        mx = [jnp.max(dlog[:, h * 64:(h + 1) * 64], axis=1, keepdims=True) for h in range(N_HEADS)]
        mx = jnp.where(lane < 64, mx[0], jnp.where(lane < 128, mx[1], jnp.where(lane < 192, mx[2], mx[3])))
        qk_att = _mm_nt(q, _bdx(k, bd_m))
        m_prev = m_ref[...]
        inter = fexp + m_prev
        m_t = jnp.maximum(inter, mx)
        w0 = jnp.exp(inter - m_t)
        s = jnp.exp(dlog - m_t) * qk_att
        cbd = cbd_ref[...]
        n_row = n_ref[...]
        num = w0 * _mm(q, cbd) + _mm(s, _bdx(v, bd_m))
        den = _mm_xc(w0 * (q * n_row) + s, ones2)
        hh = num / jnp.maximum(jnp.abs(den), jnp.exp(-m_t))
        ms = _mm_xc(hh * hh, pavg2)
        y_ref[0, rows, :] = (_sigmoid(og) * (hh * lax.rsqrt(ms + 1e-6) * ng_ref[...])).astype(BF16)
        m_new = m_t[CHUNK - 1:CHUNK]
        f_last = fexp[CHUNK - 1:CHUNK]
        kw = k * jnp.exp(f_last - fexp + iexp - m_new)
        w0f = jnp.exp(f_last + m_prev - m_new)
        cbd_ref[...] = jnp.where(bd, w0f * cbd + _mm_tn(kw, v), 0.0)
        n_ref[...] = w0f * n_row + jnp.sum(kw, axis=0, keepdims=True)
        m_ref[...] = m_new

    @pl.when(j == n_blocks - 1)
    def _():
        _get_block_diag(cbd_ref, co_ref, D_D, D_D)
        no_ref[0] = n_ref[...]
        mo_ref[0] = m_ref[...]
        cbo_ref[0] = carry_ref[...]


def _mlstm_prompt(z3, cb0, c0, n0, m0, cw, cbias, bif, ng, lb):
    bsz, seq, _ = z3.shape
    n_blocks = seq // lb
    cst = pl.BlockSpec((1, N_HEADS, D_D, D_D), lambda i, j: (i, 0, 0, 0))
    row = pl.BlockSpec((1, 1, 256), lambda i, j: (i, 0, 0))
    cbs = pl.BlockSpec((1, CONV_D - 1, 512), lambda i, j: (i, 0, 0))
    return pl.pallas_call(
        functools.partial(_mlstm_prompt_body, n_chunks=lb // CHUNK, n_blocks=n_blocks, lb=lb),
        grid=(bsz, n_blocks),
        in_specs=[pl.BlockSpec((1, lb, ML_W), lambda i, j: (i, j, ML_OFF // ML_W)),
                  pl.BlockSpec((1, lb, MLIF_W), lambda i, j: (i, j, MLIF_OFF // MLIF_W)),
                  cbs, cst, row, row, _layer_spec(cw), _layer_spec(cbias), _layer_spec(bif), _layer_spec(ng)],
        out_specs=[pl.BlockSpec((1, lb, W_GROUP), lambda i, j: (i, j, 0)), cst, row, row, cbs],
        out_shape=[jax.ShapeDtypeStruct((bsz, seq, W_GROUP), BF16),
                   jax.ShapeDtypeStruct((bsz, N_HEADS, D_D, D_D), F32),
                   jax.ShapeDtypeStruct((bsz, 1, 256), F32),
                   jax.ShapeDtypeStruct((bsz, 1, 256), F32),
                   jax.ShapeDtypeStruct((bsz, CONV_D - 1, 512), F32)],
        scratch_shapes=[pltpu.VMEM((256, 256), F32), pltpu.VMEM((1, 256), F32), pltpu.VMEM((1, 256), F32),
                        pltpu.VMEM((CONV_D - 1, 512), F32), pltpu.VMEM((lb, 512), F32)],
        compiler_params=_cp("arbitrary", "arbitrary"),
        name="mlstm_prompt",
    )(z3, z3, cb0, c0, n0, m0, cw.arr, cbias.arr, bif.arr, ng.arr)


T_COLS = 1024


def _to_rows(src_ref, dst_ref, n):
    for i in range(n // T_COLS):
        dst_ref[i * T_COLS:(i + 1) * T_COLS, :] = src_ref[:, i * T_COLS:(i + 1) * T_COLS].T


def _to_cols(src_ref, dst_ref, n):
    for i in range(n // T_COLS):
        dst_ref[:, i * T_COLS:(i + 1) * T_COLS] = src_ref[i * T_COLS:(i + 1) * T_COLS, :].T


def _per_head(x, fn):
    return jnp.concatenate([fn(x[h * 64:(h + 1) * 64], h) for h in range(N_HEADS)], axis=0)


def _expand_heads(x, nb):
    return jnp.concatenate([jnp.broadcast_to(x[h:h + 1], (64, nb)) for h in range(N_HEADS)], axis=0)


def _gla_sample_body(z_ref, s_ref, wa_ref, ba_ref, ng_ref, y_ref, so_ref,
                     st_s, dec_s, k_s, q_s, v_s, o_s, *, steps, nb):
    n_state = N_HEADS * DK_A * DV_A
    _to_rows(s_ref, st_s, n_state)
    for t in range(steps):
        zt = z_ref[t].T
        q_s[t] = zt[0:128] * DK_A ** -0.5
        k_s[t] = zt[128:256]
        v_s[t] = zt[256:512]
        lg = _log_sigmoid(_mm_hi(wa_ref[...], zt[768:896]) + ba_ref[...]) * (1.0 / GLA_TAU)
        dec_s[t] = jnp.exp(lg)
    o_s[...] = jnp.zeros(o_s.shape, F32)

    def per_key(hk, carry):
        r0 = pl.multiple_of(hk * DV_A, DV_A)
        v0 = pl.multiple_of((hk >> 5) * DV_A, DV_A)
        s = st_s[pl.ds(r0, DV_A), :]
        for t in range(steps):
            s = dec_s[t, pl.ds(hk, 1), :] * s + k_s[t, pl.ds(hk, 1), :] * v_s[t, pl.ds(v0, DV_A), :]
            o_s[t, pl.ds(v0, DV_A), :] += q_s[t, pl.ds(hk, 1), :] * s
        st_s[pl.ds(r0, DV_A), :] = s
        return carry

    lax.fori_loop(0, N_HEADS * DK_A, per_key, 0)
    for t in range(steps):
        g = z_ref[t, :, 512:768].T
        on = _per_head(o_s[t], lambda oh, h: oh * lax.rsqrt(jnp.mean(oh * oh, axis=0, keepdims=True) + 1e-6))
        y_ref[t] = (on * ng_ref[...] * _silu(g)).T.astype(BF16)
    _to_cols(st_s, so_ref, n_state)


def _gla_sample(zs, s, wa_t, ba_col, ng_col):
    steps, nb, _ = zs.shape
    n_state = N_HEADS * DK_A * DV_A
    full = lambda shape: pl.BlockSpec(shape, lambda i: (0,) * len(shape))
    return pl.pallas_call(
        functools.partial(_gla_sample_body, steps=steps, nb=nb),
        grid=(1,),
        in_specs=[pl.BlockSpec((steps, nb, GLA_W), lambda i: (0, 0, GLA_OFF // GLA_W)), _layer_spec(s),
                  _layer_spec(wa_t), _layer_spec(ba_col), _layer_spec(ng_col)],
        out_specs=[full((steps, nb, W_GROUP)), _layer_spec(s)],
        out_shape=[jax.ShapeDtypeStruct((steps, nb, W_GROUP), BF16), _stacked_like(s)],
        input_output_aliases={1: 1},
        scratch_shapes=[pltpu.VMEM((n_state, nb), F32), pltpu.VMEM((steps, 128, nb), F32),
                        pltpu.VMEM((steps, 128, nb), F32), pltpu.VMEM((steps, 128, nb), F32),
                        pltpu.VMEM((steps, 256, nb), F32), pltpu.VMEM((steps, 256, nb), F32)],
        compiler_params=_cp("arbitrary"),
        name="gla_sample",
    )(zs, s.arr, wa_t.arr, ba_col.arr, ng_col.arr)


def _pool_sample_body(z_ref, buf_ref, w_ref, sc_ref, y_ref, bo_ref, *, steps, nb, start_pos):
    ext = [buf_ref[:, j * W_GROUP:(j + 1) * W_GROUP] for j in range(POOL_BUF)] + [z_ref[t] for t in range(steps)]
    lane = _iota((nb, W_GROUP), 1)

    def by_group(vals):
        return jnp.where(lane < 64, vals[0], jnp.where(lane < 128, vals[1], jnp.where(lane < 192, vals[2], vals[3])))

    for t in range(steps):
        r = POOL_BUF + t
        acc = ext[r]
        sums = {}
        for back in range(1, max(POOL_WINDOWS)):
            acc = acc + ext[r - back]
            sums[back + 1] = acc
        win = by_group([sums[w] for w in POOL_WINDOWS])
        cnt = by_group([float(min(start_pos + t + 1, w)) for w in POOL_WINDOWS])
        y_ref[t] = (_mm(win / cnt - ext[r], w_ref[...]) * sc_ref[...]).astype(BF16)
    for j in range(POOL_BUF):
        bo_ref[:, j * W_GROUP:(j + 1) * W_GROUP] = ext[steps + j]


def _pool_sample(zs, buf, wbd, scale, start_pos):
    steps, nb, _ = zs.shape
    full = lambda shape: pl.BlockSpec(shape, lambda i: (0,) * len(shape))
    return pl.pallas_call(
        functools.partial(_pool_sample_body, steps=steps, nb=nb, start_pos=start_pos),
        grid=(1,),
        in_specs=[pl.BlockSpec((steps, nb, W_GROUP), lambda i: (0, 0, POOL_OFF // W_GROUP)),
                  _layer_spec(buf), _layer_spec(wbd), _layer_spec(scale)],
        out_specs=[full((steps, nb, W_GROUP)), _layer_spec(buf)],
        out_shape=[jax.ShapeDtypeStruct((steps, nb, W_GROUP), BF16), _stacked_like(buf)],
        input_output_aliases={1: 1},
        compiler_params=_cp("arbitrary"),
        name="pool_sample",
    )(zs, buf.arr, wbd.arr, scale.arr)


def _rwkv_sample_body(z_ref, sh_ref, s_ref, mu_ref, w0_ref, w2_ref, a0_ref, a2_ref, g2_ref, kk_ref, ka_ref,
                      rk_ref, lg_ref, lb_ref, y_ref, so_ref, sho_ref,
                      st_s, r_s, k_s, v_s, kk_s, b_s, w_s, g_s, o_s, *, steps, nb):
    n_state = N_HEADS * N_C * N_C
    _to_rows(s_ref, st_s, n_state)
    prev = sh_ref[...].T
    for t in range(steps):
        f = z_ref[t].T
        mixed = f + (prev - f) * mu_ref[...]
        prev = f
        k = mixed[256:512]
        low = mixed[768:896]
        w_s[t] = jnp.exp(-_sigmoid(w0_ref[...] + _mm_hi(w2_ref[...], jnp.tanh(low))) * math.exp(-0.5))
        a = _sigmoid(a0_ref[...] + _mm_hi(a2_ref[...], low))
        g_s[t] = _mm_hi(g2_ref[...], _sigmoid(low))
        kk = _per_head(k * kk_ref[...], lambda kh, h: kh * lax.rsqrt(
            jnp.maximum(jnp.sum(kh * kh, axis=0, keepdims=True), 1e-24)))
        r_s[t] = mixed[0:256]
        v_s[t] = mixed[512:768]
        kk_s[t] = kk
        k_s[t] = k * (1.0 + (a - 1.0) * ka_ref[...])
        b_s[t] = kk * a

    def per_value(hv, carry):
        r0 = pl.multiple_of(hv * N_C, N_C)
        hs = pl.ds(pl.multiple_of((hv >> 6) * N_C, N_C), N_C)
        s = st_s[pl.ds(r0, N_C), :]
        for t in range(steps):
            sa = -jnp.sum(s * kk_s[t, hs, :], axis=0, keepdims=True)
            s = s * w_s[t, hs, :] + sa * b_s[t, hs, :] + v_s[t, pl.ds(hv, 1), :] * k_s[t, hs, :]
            o_s[t, pl.ds(hv, 1), :] = jnp.sum(s * r_s[t, hs, :], axis=0, keepdims=True)
        st_s[pl.ds(r0, N_C), :] = s
        return carry

    lax.fori_loop(0, N_HEADS * N_C, per_value, 0, unroll=4)
    for t in range(steps):
        def norm(oh, h):
            mu = jnp.mean(oh, axis=0, keepdims=True)
            xc = oh - mu
            return xc * lax.rsqrt(jnp.mean(xc * xc, axis=0, keepdims=True) + RWKV_LN_EPS)
        on = _per_head(o_s[t], norm) * lg_ref[...] + lb_ref[...]
        rk = r_s[t] * k_s[t] * rk_ref[...]
        v = v_s[t]
        bonus = _per_head(rk, lambda x, h: jnp.sum(x, axis=0, keepdims=True) * v[h * 64:(h + 1) * 64])
        y_ref[t] = ((on + bonus) * g_s[t]).T.astype(BF16)
    _to_cols(st_s, so_ref, n_state)
    sho_ref[...] = z_ref[steps - 1]


def _rwkv_sample(zs, sh, s, p):
    steps, nb, _ = zs.shape
    n_state = N_HEADS * N_C * N_C
    full = lambda shape: pl.BlockSpec(shape, lambda i: (0,) * len(shape))
    blk = pltpu.VMEM((steps, 256, nb), F32)
    params = [p[k + "_c"] for k in _RWKV_PARAMS]
    return pl.pallas_call(
        functools.partial(_rwkv_sample_body, steps=steps, nb=nb),
        grid=(1,),
        in_specs=[pl.BlockSpec((steps, nb, RWKV_W), lambda i: (0, 0, RWKV_OFF // RWKV_W)), _layer_spec(sh),
                  _layer_spec(s)] + [_layer_spec(q) for q in params],
        out_specs=[full((steps, nb, W_GROUP)), _layer_spec(s), _layer_spec(sh)],
        out_shape=[jax.ShapeDtypeStruct((steps, nb, W_GROUP), BF16), _stacked_like(s), _stacked_like(sh)],
        input_output_aliases={2: 1, 1: 2},
        scratch_shapes=[pltpu.VMEM((n_state, nb), F32), blk, blk, blk, blk, blk, blk, blk, blk],
        compiler_params=_cp("arbitrary"),
        name="rwkv_sample",
    )(zs, sh.arr, s.arr, *[q.arr for q in params])


def _mlstm_sample_body(z_ref, zif_ref, cb_ref, c_ref, n_ref, m_ref, cw_ref, cbias_ref, bif_ref, ng_ref,
                       y_ref, co_ref, no_ref, mo_ref, cbo_ref,
                       ct_s, q_s, k_s, v_s, wf_s, den_s, em_s, o_s, *, steps, nb):
    n_state = N_HEADS * D_D * D_D
    _to_rows(c_ref, ct_s, n_state)
    n = n_ref[...].T
    m = m_ref[...].T[0:N_HEADS]
    hist = [cb_ref[:, j * 512:(j + 1) * 512].T for j in range(CONV_D - 1)]
    for t in range(steps):
        hist = hist + [z_ref[t, :, 0:512].T]
        conv = cbias_ref[...]
        for jj in range(CONV_D):
            conv = conv + cw_ref[:, jj:jj + 1] * hist[jj]
        hist = hist[1:]
        act = _silu(conv)
        q = act[0:256] * D_D ** -0.5
        k = act[256:512]
        gates = zif_ref[t].T + bif_ref[...]
        ig = gates[0:N_HEADS]
        lf = _log_sigmoid(gates[N_HEADS:2 * N_HEADS])
        m_new = jnp.maximum(lf + m, ig)
        wf = _expand_heads(jnp.exp(lf + m - m_new), nb)
        kw = _expand_heads(jnp.exp(ig - m_new), nb) * k
        m = m_new
        n = wf * n + kw
        den = _per_head(q * n, lambda x, h: jnp.broadcast_to(jnp.sum(x, axis=0, keepdims=True), (64, nb)))
        q_s[t] = q
        k_s[t] = kw
        v_s[t] = z_ref[t, :, 512:768].T
        wf_s[t] = wf
        den_s[t] = den
        em_s[t] = _expand_heads(jnp.exp(-m_new), nb)
    o_s[...] = jnp.zeros(o_s.shape, F32)

    def per_key(hd, carry):
        r0 = pl.multiple_of(hd * D_D, D_D)
        e0 = pl.multiple_of((hd >> 6) * D_D, D_D)
        c = ct_s[pl.ds(r0, D_D), :]
        for t in range(steps):
            c = wf_s[t, pl.ds(hd, 1), :] * c + k_s[t, pl.ds(hd, 1), :] * v_s[t, pl.ds(e0, D_D), :]
            o_s[t, pl.ds(e0, D_D), :] += q_s[t, pl.ds(hd, 1), :] * c
        ct_s[pl.ds(r0, D_D), :] = c
        return carry

    lax.fori_loop(0, N_HEADS * D_D, per_key, 0)
    for t in range(steps):
        hh = o_s[t] / jnp.maximum(jnp.abs(den_s[t]), em_s[t])
        hn = _per_head(hh, lambda x, h: x * lax.rsqrt(jnp.mean(x * x, axis=0, keepdims=True) + 1e-6))
        y_ref[t] = (_sigmoid(z_ref[t, :, 768:1024].T) * (hn * ng_ref[...])).T.astype(BF16)
    _to_cols(ct_s, co_ref, n_state)
    no_ref[...] = n.T
    mo_ref[...] = jnp.concatenate([m, jnp.zeros((8 - N_HEADS, nb), F32)], axis=0)
    for j in range(CONV_D - 1):
        cbo_ref[:, j * 512:(j + 1) * 512] = hist[j].T


def _mlstm_sample(zs, cb, c, n, m_pad, cw_t, cbias_col, bif_col, ng_col):
    steps, nb, _ = zs.shape
    n_state = N_HEADS * D_D * D_D
    full = lambda shape: pl.BlockSpec(shape, lambda i: (0,) * len(shape))
    blk = pltpu.VMEM((steps, 256, nb), F32)
    return pl.pallas_call(
        functools.partial(_mlstm_sample_body, steps=steps, nb=nb),
        grid=(1,),
        in_specs=[pl.BlockSpec((steps, nb, ML_W), lambda i: (0, 0, ML_OFF // ML_W)),
                  pl.BlockSpec((steps, nb, MLIF_W), lambda i: (0, 0, MLIF_OFF // MLIF_W)),
                  _layer_spec(cb), _layer_spec(c), _layer_spec(n), _layer_spec(m_pad),
                  _layer_spec(cw_t), _layer_spec(cbias_col), _layer_spec(bif_col), _layer_spec(ng_col)],
        out_specs=[full((steps, nb, W_GROUP)), _layer_spec(c), _layer_spec(n), full((8, nb)), _layer_spec(cb)],
        out_shape=[jax.ShapeDtypeStruct((steps, nb, W_GROUP), BF16), _stacked_like(c), _stacked_like(n),
                   jax.ShapeDtypeStruct((8, nb), F32), _stacked_like(cb)],
        input_output_aliases={3: 1, 4: 2, 2: 4},
        scratch_shapes=[pltpu.VMEM((n_state, nb), F32), blk, blk, blk, blk, blk, blk, blk],
        compiler_params=_cp("arbitrary"),
        name="mlstm_sample",
    )(zs, zs, cb.arr, c.arr, n.arr, m_pad.arr, cw_t.arr, cbias_col.arr, bif_col.arr, ng_col.arr)


def _pad_cols(x, n):
    return jnp.pad(x, [(0, 0)] * (x.ndim - 1) + [(0, n - x.shape[-1])])


def _rows_at(x, row0, n_rows):
    return jnp.pad(x, ((0, 0), (row0, n_rows - row0 - x.shape[1]), (0, 0)))


def _stacked_params(w):
    gla, pool, rwkv, ml = jnp.split(w["w_in"], [784, 784 + 256, 784 + 256 + 832], axis=2)
    w_in_p = jnp.concatenate([_pad_cols(gla, GLA_W), _pad_cols(rwkv, RWKV_W), pool, ml[:, :, :1024],
                              _pad_cols(ml[:, :, 1024:], MLIF_W)], axis=2).astype(BF16)
    col = lambda v: v.reshape(DEPTH, -1, 1)
    row = lambda v: v.reshape(DEPTH, 1, -1)
    tr = lambda m: m.transpose(0, 2, 1)
    wa = _rows_at(w["gla_w_a2"], 0, 128)
    w2 = _rows_at(w["rwkv_w2"], 0, 128)
    a2 = _rows_at(w["rwkv_a2"], R_W, 128)
    g2 = _rows_at(w["rwkv_g2"], R_W + R_AA, 128)
    mu = _pad_cols(w["rwkv_mu"], RWKV_W)
    bif = _pad_cols(jnp.concatenate([w["mlstm_b_i"], w["mlstm_b_f"]], axis=1), 128)
    wbd = jnp.zeros((DEPTH, 256, 256), F32)
    for gi in range(4):
        wbd = wbd.at[:, gi * 64:(gi + 1) * 64, gi * 64:(gi + 1) * 64].set(w["pool_w"][:, gi])
    vecs = dict(w0=w["rwkv_w0"], a0=w["rwkv_a0"], k_k=w["rwkv_k_k"], k_a=w["rwkv_k_a"], r_k=w["rwkv_r_k"],
                ln_g=w["rwkv_ln_g"], ln_b=w["rwkv_ln_b"], mu=mu)
    rw = {k: row(v) for k, v in vecs.items()}
    rw.update({k + "_c": col(v) for k, v in vecs.items()})
    rw.update(w2=w2, a2=a2, g2=g2, w2_c=tr(w2), a2_c=tr(a2), g2_c=tr(g2))
    return dict(
        w_in=w_in_p, w_out=w["w_out"].astype(BF16), ln1_g=row(w["ln1_g"]), ln1_b=row(w["ln1_b"]),
        w_up=w["ffn_w_up"].astype(BF16), ffn_cw=w["ffn_conv_w"], ffn_cb=row(w["ffn_conv_b"]),
        w_down=w["ffn_w_down"].astype(BF16), ln2_g=row(w["ln2_g"]), ln2_b=row(w["ln2_b"]),
        gla_wa=wa, gla_wa_t=tr(wa), gla_ba=row(w["gla_b_a"]), gla_ba_c=col(w["gla_b_a"]),
        gla_ng=row(w["gla_norm_g"]), gla_ng_c=col(w["gla_norm_g"]),
        pool_w=wbd, pool_scale=row(w["pool_scale"]), rwkv=rw,
        ml_cw=w["mlstm_conv_w"], ml_cw_t=tr(w["mlstm_conv_w"]), ml_cb=row(w["mlstm_conv_b"]),
        ml_cb_c=col(w["mlstm_conv_b"]), ml_bif=row(bif), ml_bif_c=col(bif),
        ml_ng=row(w["mlstm_norm_g"]), ml_ng_c=col(w["mlstm_norm_g"]))


def _layer_view(stacked, l):
    return {k: (_layer_view(v, l) if isinstance(v, dict) else _Layer(v, l)) for k, v in stacked.items()}


RWKV_BLOCK = 512
GLA_BLOCK = 512
MLSTM_BLOCK = 256
POOL_BLOCK = 1024
DENSE_TILE = 1024
FFN_TILE = 512


def _prompt_layer(x, p):
    bsz, seq, d = x.shape
    z = _linear(x.reshape(bsz * seq, d), p["w_in"], DENSE_TILE).reshape(bsz, seq, ZC)
    zeros = lambda *s: jnp.zeros((bsz,) + s, F32)
    y_a, s_gla = _gla_prompt(z, zeros(N_HEADS, DK_A, DV_A), p["gla_wa"], p["gla_ba"], p["gla_ng"], GLA_BLOCK)
    y_b, s_pool = _pool_prompt(z, zeros(POOL_BUF, W_GROUP), p["pool_w"], p["pool_scale"], POOL_BLOCK, 0)
    y_c, s_rwkv, s_shift = _rwkv_prompt(z, zeros(1, RWKV_W), zeros(N_HEADS, N_C, N_C), p["rwkv"], RWKV_BLOCK)
    y_d, s_c, s_n, s_m, s_conv = _mlstm_prompt(z, zeros(CONV_D - 1, 512), zeros(N_HEADS, D_D, D_D), zeros(1, 256),
                                               zeros(1, 256), p["ml_cw"], p["ml_cb"], p["ml_bif"], p["ml_ng"],
                                               MLSTM_BLOCK)
    x2, s_ffn = _ffn_prompt([y_a, y_b, y_c, y_d], x, zeros(FFN_CONV - 1, 2 * D_FF), p, FFN_TILE)
    states = (s_gla, s_pool, s_rwkv, s_shift[:, :, :RWKV_COLS], s_c, s_n.reshape(bsz, N_HEADS, D_D),
              s_m[:, 0, ::D_D], s_conv, s_ffn)
    return x2, states


def _sample_layer(x, st, p, steps, nb):
    s_gla, s_pool, s_rwkv, s_shift, s_c, s_n, s_m, s_conv, s_ffn = st
    z = _linear(x, p["w_in"], steps * nb).reshape(steps, nb, ZC)
    y_a, n_gla = _gla_sample(z, s_gla, p["gla_wa_t"], p["gla_ba_c"], p["gla_ng_c"])
    y_b, n_pool = _pool_sample(z, s_pool, p["pool_w"], p["pool_scale"], PAST_LEN)
    y_c, n_rwkv, n_shift = _rwkv_sample(z, s_shift, s_rwkv, p["rwkv"])
    y_d, n_c, n_n, n_m, n_conv = _mlstm_sample(z, s_conv, s_c, s_n, s_m, p["ml_cw_t"], p["ml_cb_c"], p["ml_bif_c"],
                                               p["ml_ng_c"])
    flat = lambda y: y.reshape(steps * nb, W_GROUP)
    x2, n_ffn = _ffn_sample([flat(y_a), flat(y_b), flat(y_c), flat(y_d)], x, s_ffn, p, nb, steps)
    return x2, (n_gla, n_pool, n_rwkv, n_shift, n_c, n_n, n_m, n_conv, n_ffn)


def kernel(x_prompt, x_sample, state_gla, state_pool, state_rwkv, state_rwkv_shift, state_mlstm_c, state_mlstm_n, state_mlstm_m, state_mlstm_conv, state_ffn_conv, w_in, gla_w_a2, gla_b_a, gla_norm_g, pool_w, pool_scale, rwkv_mu, rwkv_w0, rwkv_w2, rwkv_a0, rwkv_a2, rwkv_g2, rwkv_k_k, rwkv_k_a, rwkv_r_k, rwkv_ln_g, rwkv_ln_b, mlstm_conv_w, mlstm_conv_b, mlstm_b_i, mlstm_b_f, mlstm_norm_g, w_out, ln1_g, ln1_b, ffn_w_up, ffn_conv_w, ffn_conv_b, ffn_w_down, ln2_g, ln2_b):
    w = dict(w_in=w_in, gla_w_a2=gla_w_a2, gla_b_a=gla_b_a, gla_norm_g=gla_norm_g, pool_w=pool_w,
             pool_scale=pool_scale, rwkv_mu=rwkv_mu, rwkv_w0=rwkv_w0, rwkv_w2=rwkv_w2, rwkv_a0=rwkv_a0,
             rwkv_a2=rwkv_a2, rwkv_g2=rwkv_g2, rwkv_k_k=rwkv_k_k, rwkv_k_a=rwkv_k_a, rwkv_r_k=rwkv_r_k,
             rwkv_ln_g=rwkv_ln_g, rwkv_ln_b=rwkv_ln_b, mlstm_conv_w=mlstm_conv_w, mlstm_conv_b=mlstm_conv_b,
             mlstm_b_i=mlstm_b_i, mlstm_b_f=mlstm_b_f, mlstm_norm_g=mlstm_norm_g, w_out=w_out, ln1_g=ln1_g,
             ln1_b=ln1_b, ffn_w_up=ffn_w_up, ffn_conv_w=ffn_conv_w, ffn_conv_b=ffn_conv_b, ffn_w_down=ffn_w_down,
             ln2_g=ln2_g, ln2_b=ln2_b)
    sample_states = (state_gla, state_pool, state_rwkv, state_rwkv_shift, state_mlstm_c, state_mlstm_n,
                     state_mlstm_m, state_mlstm_conv, state_ffn_conv)
    nb, steps, d = x_sample.shape
    flat_in = [s.reshape(DEPTH, nb, -1) for s in sample_states]
    flat_in[3] = _pad_cols(flat_in[3], RWKV_W)
    flat_in[6] = _pad_cols(flat_in[6], 128)
    stacked = _stacked_params(w)
    yp = x_prompt
    ys = x_sample.transpose(1, 0, 2).reshape(steps * nb, d)
    acc_p = [[] for _ in sample_states]
    m_new = []
    flat = list(flat_in)
    for l in range(DEPTH):
        p = _layer_view(stacked, l)
        yp, st_p = _prompt_layer(yp, p)
        ys, st_s = _sample_layer(ys, tuple(_Layer(s, l) for s in flat), p, steps, nb)
        for i in range(len(sample_states)):
            acc_p[i].append(st_p[i])
            if i == 6:
                m_new.append(st_s[i])
            else:
                flat[i] = st_s[i]
    ys = ys.reshape(steps, nb, d).transpose(1, 0, 2)
    out_s = list(flat)
    out_s[3] = out_s[3][:, :, :RWKV_COLS]
    out_s[6] = jnp.stack(m_new)[:, 0:N_HEADS, :].transpose(0, 2, 1)
    out = [yp, ys]
    for sp, ss, ref in zip(acc_p, out_s, sample_states):
        out.append(jnp.stack(sp))
        out.append(ss.reshape(ref.shape))
    return tuple(out)
```

```python
import functools
import math

import jax
import jax.numpy as jnp
from jax import lax
from jax.experimental import pallas as pl
from jax.experimental.pallas import tpu as pltpu

F32 = jnp.float32
BF16 = jnp.bfloat16
HI = lax.Precision.HIGHEST

D_MODEL = 1024
DEPTH = 4
PAST_LEN = 16384
W_GROUP = 256
N_HEADS = 4
DK_A = 32
DV_A = 64
R_GLA = 16
GLA_TAU = 16.0
POOL_WINDOWS = (2, 4, 8, 16)
POOL_BUF = 15
N_C = 64
R_W, R_AA, R_G = 16, 16, 32
RWKV_COLS = 832
RWKV_LN_EPS = 64e-5
D_D = 64
CONV_D = 4
D_FF = 2816
FFN_CONV = 3
ALPHA = (2 * DEPTH) ** 0.25
CHUNK = 64

GLA_W = 896
RWKV_W = 896
GLA_OFF, RWKV_OFF, POOL_OFF, ML_OFF, MLIF_OFF = 0, 896, 1792, 2048, 3072
ZC = 3200
ML_W = 1024
MLIF_W = 128

VMEM_LIMIT = 56 * 1024 * 1024


def _cp(*sem):
    return pltpu.CompilerParams(dimension_semantics=sem, vmem_limit_bytes=VMEM_LIMIT)


def _mm(a, b):
    return jnp.dot(a.astype(BF16), b.astype(BF16), preferred_element_type=F32)


def _mm_nt(a, b):
    return lax.dot_general(a.astype(BF16), b.astype(BF16), (((1,), (1,)), ((), ())), preferred_element_type=F32)


def _mm_tn(a, b):
    return lax.dot_general(a.astype(BF16), b.astype(BF16), (((0,), (0,)), ((), ())), preferred_element_type=F32)


def _mm_hi(a, b):
    return jnp.dot(a, b, preferred_element_type=F32, precision=HI)


def _split2(x):
    hi = x.astype(BF16)
    return hi, (x - hi.astype(F32)).astype(BF16)


def _mm_xc(x, c2):
    hi, mid = _split2(x)
    return jnp.dot(jnp.concatenate([hi, mid], axis=1), c2, preferred_element_type=F32)


def _mm_cx(c2, x):
    hi, mid = _split2(x)
    return jnp.dot(c2, jnp.concatenate([hi, mid], axis=0), preferred_element_type=F32)


def _twice(c, axis):
    c = c.astype(F32).astype(BF16)
    return jnp.concatenate([c, c], axis=axis)


def _iota(shape, dim):
    return lax.broadcasted_iota(jnp.int32, shape, dim)


def _sigmoid(x):
    return 1.0 / (1.0 + jnp.exp(-x))


def _silu(x):
    return x * _sigmoid(x)


def _log_sigmoid(x):
    return jnp.minimum(x, 0.0) - jnp.log(1.0 + jnp.exp(-jnp.abs(x)))


def _tile4(x):
    return jnp.concatenate([x, x, x, x], axis=0)


def _block_diag_mask(rows, cols, rshift, cshift):
    return (_iota((rows, cols), 0) >> rshift) == (_iota((rows, cols), 1) >> cshift)


def _layer_norm(h, g, b, eps):
    mu = jnp.mean(h, axis=-1, keepdims=True)
    xc = h - mu
    var = jnp.mean(xc * xc, axis=-1, keepdims=True)
    return xc * lax.rsqrt(var + eps) * g + b


class _Layer:
    def __init__(self, arr, l):
        self.arr, self.l, self.shape = arr, l, tuple(arr.shape[1:])


def _layer_spec(p, single_buffer=False):
    n, l = len(p.shape), p.l
    if single_buffer:
        return pl.BlockSpec((None,) + p.shape, lambda *g: (l,) + (0,) * n, pipeline_mode=pl.Buffered(1))
    return pl.BlockSpec((None,) + p.shape, lambda *g: (l,) + (0,) * n)


def _stacked_like(p):
    return jax.ShapeDtypeStruct(p.arr.shape, p.arr.dtype)


def _linear_body(x_ref, w_ref, o_ref):
    o_ref[...] = jnp.dot(x_ref[...].astype(BF16), w_ref[...], preferred_element_type=F32)


def _linear(x, w, tm):
    t, k = x.shape
    n = w.shape[1]
    return pl.pallas_call(
        _linear_body,
        grid=(t // tm,),
        in_specs=[pl.BlockSpec((tm, k), lambda i: (i, 0)), _layer_spec(w, single_buffer=True)],
        out_specs=pl.BlockSpec((tm, n), lambda i: (i, 0)),
        out_shape=jax.ShapeDtypeStruct((t, n), F32),
        compiler_params=_cp("arbitrary"),
        name="in_proj",
    )(x, w.arr)


def _outproj(ys, x, w_ref, g_ref, b_ref):
    y = jnp.concatenate(ys, axis=1).astype(BF16)
    mixed = jnp.dot(y, w_ref[...], preferred_element_type=F32)
    return _layer_norm(ALPHA * x + mixed, g_ref[...], b_ref[...], 1e-5)


FF_CHUNK = 256
N_FF_CHUNKS = D_FF // FF_CHUNK
UP_AHEAD = 2
DOWN_ROWS = 256


def _ffn_prompt_body(ya_ref, yb_ref, yc_ref, yd_ref, x_ref, buf_ref, wo_ref, g1_ref, b1_ref,
                     wup_ref, cw_ref, cb_ref, wdn_ref, g_ref, b_ref,
                     o_ref, st_ref, carry_ref, h_ref, x1_ref, *, tm, n_tiles):
    j = pl.program_id(1)

    @pl.when(j == 0)
    def _():
        carry_ref[...] = buf_ref[0]

    x = _outproj([ya_ref[0], yb_ref[0], yc_ref[0], yd_ref[0]], x_ref[0], wo_ref, g1_ref, b1_ref)
    x1_ref[...] = x
    xb = x.astype(BF16)
    row = _iota((8, FF_CHUNK), 0)
    col = lambda c, half: slice(half * D_FF + c * FF_CHUNK, half * D_FF + (c + 1) * FF_CHUNK)
    up = lambda c: [jnp.dot(xb, wup_ref[:, col(c, half)], preferred_element_type=F32) for half in range(2)]
    ahead = [up(c) for c in range(UP_AHEAD)]
    for c in range(N_FF_CHUNKS):
        if c + UP_AHEAD < N_FF_CHUNKS:
            ahead.append(up(c + UP_AHEAD))
        u_now = ahead.pop(0)
        acts = []
        for half in range(2):
            sl = col(c, half)
            u = u_now[half]
            two_back, one_back = carry_ref[0:1, sl], carry_ref[1:2, sl]
            w0, w1, w2 = cw_ref[0:1, sl], cw_ref[1:2, sl], cw_ref[2:3, sl]
            r1 = pltpu.roll(u, 1, 0)
            r2 = pltpu.roll(u, 2, 0)
            r1_head = jnp.where(row == 0, one_back, r1[0:8])
            r2_head = jnp.where(row == 0, two_back, jnp.where(row == 1, one_back, r2[0:8]))
            head = cb_ref[:, sl] + w0 * r2_head + w1 * r1_head + w2 * u[0:8]
            body = cb_ref[:, sl] + w0 * r2 + w1 * r1 + w2 * u
            carry_ref[:, sl] = u[tm - 2:tm]
            acts.append(jnp.concatenate([head, body[8:]], axis=0))
        h_ref[:, c * FF_CHUNK:(c + 1) * FF_CHUNK] = (_silu(acts[0]) * acts[1]).astype(BF16)
    for r0 in range(0, tm, DOWN_ROWS):
        rows = slice(r0, r0 + DOWN_ROWS)
        f = jnp.dot(h_ref[rows, :], wdn_ref[...], preferred_element_type=F32)
        o_ref[0, rows, :] = _layer_norm(ALPHA * x1_ref[rows, :] + f, g_ref[...], b_ref[...], 1e-5)

    @pl.when(j == n_tiles - 1)
    def _():
        st_ref[0] = carry_ref[...]


_DENSE_PARAMS = ("w_out", "ln1_g", "ln1_b", "w_up", "ffn_cw", "ffn_cb", "w_down", "ln2_g", "ln2_b")


def _ffn_prompt(ys, x3, buf, p, tm):
    bsz, seq, d = x3.shape
    n_tiles = seq // tm
    params = [p[k] for k in _DENSE_PARAMS]
    tok = lambda w: pl.BlockSpec((1, tm, w), lambda i, j: (i, j, 0))
    st = pl.BlockSpec((1, 2, 2 * D_FF), lambda i, j: (i, 0, 0))
    return pl.pallas_call(
        functools.partial(_ffn_prompt_body, tm=tm, n_tiles=n_tiles),
        grid=(bsz, n_tiles),
        in_specs=[tok(W_GROUP)] * 4 + [tok(d), st] + [_layer_spec(q, single_buffer=True) for q in params],
        out_specs=[tok(d), st],
        out_shape=[jax.ShapeDtypeStruct((bsz, seq, d), F32), jax.ShapeDtypeStruct((bsz, 2, 2 * D_FF), F32)],
        scratch_shapes=[pltpu.VMEM((2, 2 * D_FF), F32), pltpu.VMEM((tm, D_FF), BF16), pltpu.VMEM((tm, d), F32)],
        compiler_params=_cp("arbitrary", "arbitrary"),
        name="ffn_prompt",
    )(*ys, x3, buf, *[q.arr for q in params])


def _ffn_sample_body(ya_ref, yb_ref, yc_ref, yd_ref, x_ref, buf_ref, wo_ref, g1_ref, b1_ref,
                     wup_ref, cw_ref, cb_ref, wdn_ref, g_ref, b_ref,
                     o_ref, st_ref, h_ref, *, nb, steps):
    t = nb * steps
    x = _outproj([ya_ref[...], yb_ref[...], yc_ref[...], yd_ref[...]], x_ref[...], wo_ref, g1_ref, b1_ref)
    xb = x.astype(BF16)
    for c in range(N_FF_CHUNKS):
        acts = []
        for half in range(2):
            lo = half * D_FF + c * FF_CHUNK
            sl = slice(lo, lo + FF_CHUNK)
            u = jnp.dot(xb, wup_ref[:, sl], preferred_element_type=F32)
            b0 = buf_ref[:, lo:lo + FF_CHUNK]
            b1 = buf_ref[:, 2 * D_FF + lo:2 * D_FF + lo + FF_CHUNK]
            back1 = jnp.concatenate([b1, u[0:t - nb]], axis=0)
            back2 = jnp.concatenate([b0, b1, u[0:t - 2 * nb]], axis=0)
            conv = cb_ref[:, sl] + cw_ref[0:1, sl] * back2 + cw_ref[1:2, sl] * back1 + cw_ref[2:3, sl] * u
            st_ref[:, lo:lo + FF_CHUNK] = u[t - 2 * nb:t - nb]
            st_ref[:, 2 * D_FF + lo:2 * D_FF + lo + FF_CHUNK] = u[t - nb:t]
            acts.append(conv)
        h_ref[:, c * FF_CHUNK:(c + 1) * FF_CHUNK] = (_silu(acts[0]) * acts[1]).astype(BF16)
    f = jnp.dot(h_ref[...], wdn_ref[...], preferred_element_type=F32)
    o_ref[...] = _layer_norm(ALPHA * x + f, g_ref[...], b_ref[...], 1e-5)


def _ffn_sample(ys, x, buf, p, nb, steps):
    t, d = x.shape
    params = [p[k] for k in _DENSE_PARAMS]
    full = lambda shape: pl.BlockSpec(shape, lambda i: (0,) * len(shape))
    return pl.pallas_call(
        functools.partial(_ffn_sample_body, nb=nb, steps=steps),
        grid=(1,),
        in_specs=[full((t, W_GROUP))] * 4 + [full((t, d)), _layer_spec(buf)] + [_layer_spec(q) for q in params],
        out_specs=[full((t, d)), _layer_spec(buf)],
        out_shape=[jax.ShapeDtypeStruct((t, d), F32), _stacked_like(buf)],
        input_output_aliases={5: 1},
        scratch_shapes=[pltpu.VMEM((t, D_FF), BF16)],
        compiler_params=_cp("arbitrary"),
        name="ffn_sample",
    )(*ys, x, buf.arr, *[q.arr for q in params])


def _exp_masks():
    t = _iota((CHUNK, 256), 0)
    s = _iota((CHUNK, 256), 1) & (CHUNK - 1)
    return t >= s, t > s, t == s


def _bd256():
    return _block_diag_mask(256, 256, 6, 6)


def _tril_blocks(n):
    r = _iota((n, n), 0)
    c = _iota((n, n), 1)
    return ((r >= c) & ((r >> 6) == (c >> 6))).astype(F32)


def _mask_bf16(mask):
    return mask.astype(F32).astype(BF16)


def _bdx(x, mask01):
    return _tile4(x.astype(BF16)) * mask01


def _put_block_diag(dst_ref, blocks, rows, cols):
    dst_ref[...] = jnp.zeros(dst_ref.shape, F32)
    for h in range(N_HEADS):
        dst_ref[h * rows:(h + 1) * rows, h * cols:(h + 1) * cols] = blocks[h]


def _get_block_diag(src_ref, out_ref, rows, cols):
    for h in range(N_HEADS):
        out_ref[0, h] = src_ref[h * rows:(h + 1) * rows, h * cols:(h + 1) * cols]


def _gla_prompt_body(z_ref, s0_ref, wa_ref, ba_ref, ng_ref, y_ref, so_ref, sbd_ref, *, n_chunks, n_blocks):
    j = pl.program_id(1)

    @pl.when(j == 0)
    def _():
        _put_block_diag(sbd_ref, s0_ref[0], DK_A, DV_A)

    lb = n_chunks * CHUNK
    causal, _, _ = _exp_masks()
    bd_state = _block_diag_mask(128, 256, 5, 6)
    bd_k = _mask_bf16(_block_diag_mask(256, 128, 6, 5))
    bd_v = _mask_bf16(_bd256())
    pavg2 = _twice(bd_v.astype(F32) * (1.0 / DV_A), 0)

    lg = _log_sigmoid(_mm(z_ref[0, :, 768:896], wa_ref[...]) + ba_ref[...]) * (1.0 / GLA_TAU)
    bc_all = _mm_cx(_twice(_tril_blocks(lb), 1), lg)

    cs = range(n_chunks)
    sl = [slice(c * CHUNK, (c + 1) * CHUNK) for c in cs]
    q = [z_ref[0, sl[c], 0:128] * DK_A ** -0.5 for c in cs]
    bc = [bc_all[sl[c]] for c in cs]
    bl = [bc[c][CHUNK - 1:CHUNK] for c in cs]
    rho = [bc[c][CHUNK // 2 - 1:CHUNK // 2] for c in cs]
    att = [_mm_nt(q[c] * jnp.exp(bc[c] - rho[c]), _bdx(z_ref[0, sl[c], 128:256] * jnp.exp(rho[c] - bc[c]), bd_k))
           for c in cs]
    incr = [jnp.where(bd_state, _mm_tn(z_ref[0, sl[c], 128:256] * jnp.exp(bl[c] - bc[c]), z_ref[0, sl[c], 256:512]),
                      0.0) for c in cs]
    o_intra = [_mm(jnp.where(causal, att[c], 0.0), _bdx(z_ref[0, sl[c], 256:512], bd_v)) for c in cs]
    q_dec = [q[c] * jnp.exp(bc[c]) for c in cs]
    decay = []
    for c in cs:
        decay_col = jnp.broadcast_to(jnp.exp(bl[c]), (128, 128)).T
        decay.append(jnp.concatenate([decay_col, decay_col], axis=1))
    starts = [sbd_ref[...]]
    for c in range(n_chunks):
        starts.append(decay[c] * starts[c] + incr[c])
    sbd_ref[...] = starts[n_chunks]
    o = jnp.concatenate([_mm(q_dec[c], starts[c]) + o_intra[c] for c in range(n_chunks)], axis=0)
    ms = _mm_xc(o * o, pavg2)
    y_ref[0] = (o * lax.rsqrt(ms + 1e-6) * ng_ref[...] * _silu(z_ref[0, :, 512:768])).astype(BF16)

    @pl.when(j == n_blocks - 1)
    def _():
        _get_block_diag(sbd_ref, so_ref, DK_A, DV_A)


def _gla_prompt(z3, s0, wa, ba, ng, lb):
    bsz, seq, _ = z3.shape
    n_blocks = seq // lb
    st = pl.BlockSpec((1, N_HEADS, DK_A, DV_A), lambda i, j: (i, 0, 0, 0))
    return pl.pallas_call(
        functools.partial(_gla_prompt_body, n_chunks=lb // CHUNK, n_blocks=n_blocks),
        grid=(bsz, n_blocks),
        in_specs=[pl.BlockSpec((1, lb, GLA_W), lambda i, j: (i, j, GLA_OFF // GLA_W)), st,
                  _layer_spec(wa), _layer_spec(ba), _layer_spec(ng)],
        out_specs=[pl.BlockSpec((1, lb, W_GROUP), lambda i, j: (i, j, 0)), st],
        out_shape=[jax.ShapeDtypeStruct((bsz, seq, W_GROUP), BF16),
                   jax.ShapeDtypeStruct((bsz, N_HEADS, DK_A, DV_A), F32)],
        scratch_shapes=[pltpu.VMEM((128, 256), F32)],
        compiler_params=_cp("arbitrary", "arbitrary"),
        name="gla_prompt",
    )(z3, s0, wa.arr, ba.arr, ng.arr)


def _pool_prompt_body(z_ref, buf_ref, w_ref, sc_ref, y_ref, bo_ref, hist_ref, *, lb, n_blocks, start_pos):
    j = pl.program_id(1)

    @pl.when(j == 0)
    def _():
        hist_ref[0:1, :] = jnp.zeros((1, W_GROUP), F32)
        hist_ref[1:16, :] = buf_ref[0]

    zp = z_ref[0]
    e = jnp.concatenate([hist_ref[...], zp], axis=0)
    s2 = e + pltpu.roll(e, 1, 0)
    s4 = s2 + pltpu.roll(s2, 2, 0)
    s8 = s4 + pltpu.roll(s4, 4, 0)
    s16 = s8 + pltpu.roll(s8, 8, 0)
    lane = _iota((lb, W_GROUP), 1)
    win = jnp.where(lane < 64, s2[16:], jnp.where(lane < 128, s4[16:], jnp.where(lane < 192, s8[16:], s16[16:])))
    width = jnp.where(lane < 64, 2, jnp.where(lane < 128, 4, jnp.where(lane < 192, 8, 16)))
    pos = start_pos + j * lb + _iota((lb, W_GROUP), 0)
    cnt = jnp.minimum(pos + 1, width).astype(F32)
    y_ref[0] = (_mm(win / cnt - zp, w_ref[...]) * sc_ref[...]).astype(BF16)
    hist_ref[1:16, :] = e[lb + 1:lb + 16]

    @pl.when(j == n_blocks - 1)
    def _():
        bo_ref[0] = hist_ref[1:16, :]


def _pool_prompt(z3, buf, wbd, scale, lb, start_pos):
    bsz, seq, _ = z3.shape
    n_blocks = seq // lb
    st = pl.BlockSpec((1, POOL_BUF, W_GROUP), lambda i, j: (i, 0, 0))
    return pl.pallas_call(
        functools.partial(_pool_prompt_body, lb=lb, n_blocks=n_blocks, start_pos=start_pos),
        grid=(bsz, n_blocks),
        in_specs=[pl.BlockSpec((1, lb, W_GROUP), lambda i, j: (i, j, POOL_OFF // W_GROUP)), st,
                  _layer_spec(wbd), _layer_spec(scale)],
        out_specs=[pl.BlockSpec((1, lb, W_GROUP), lambda i, j: (i, j, 0)), st],
        out_shape=[jax.ShapeDtypeStruct((bsz, seq, W_GROUP), BF16),
                   jax.ShapeDtypeStruct((bsz, POOL_BUF, W_GROUP), F32)],
        scratch_shapes=[pltpu.VMEM((16, W_GROUP), F32)],
        compiler_params=_cp("arbitrary", "arbitrary"),
        name="pool_prompt",
    )(z3, buf, wbd.arr, scale.arr)


def _rwkv_prompt_body(z_ref, sh0_ref, s0_ref, mu_ref, w0_ref, w2_ref, a0_ref, a2_ref, g2_ref, kk_ref, ka_ref,
                      rk_ref, lg_ref, lb_ref, y_ref, so_ref, sho_ref,
                      nbd_ref, carry_ref, r_s, k_s, v_s, kk_s, b_s, lw_s, g_s, *, n_chunks, n_blocks, lb):
    j = pl.program_id(1)

    @pl.when(j == 0)
    def _():
        _put_block_diag(nbd_ref, s0_ref[0], N_C, N_C)
        carry_ref[...] = sh0_ref[0]

    bd = _bd256()
    bd_m = _mask_bf16(bd)
    ones2 = _twice(bd, 0)
    pavg2 = _twice(bd.astype(F32) * (1.0 / N_C), 0)

    f = z_ref[0]
    prev = jnp.concatenate([carry_ref[...], f[:-1]], axis=0)
    mixed = f + (prev - f) * mu_ref[...]
    carry_ref[...] = f[lb - 1:lb]
    r = mixed[:, 0:256]
    k = mixed[:, 256:512]
    v = mixed[:, 512:768]
    low = mixed[:, 768:896]
    lw_all = -_sigmoid(w0_ref[...] + _mm(jnp.tanh(low), w2_ref[...])) * math.exp(-0.5)
    lw_s[...] = lw_all
    a = _sigmoid(a0_ref[...] + _mm(low, a2_ref[...]))
    g_s[...] = _mm(_sigmoid(low), g2_ref[...])
    kk = k * kk_ref[...]
    kk = kk * lax.rsqrt(jnp.maximum(_mm_xc(kk * kk, ones2), 1e-24))
    r_s[...] = r
    v_s[...] = v
    kk_s[...] = kk
    k_s[...] = k * (1.0 + (a - 1.0) * ka_ref[...])
    b_s[...] = kk * a
    gc_all = _mm_cx(_twice(_tril_blocks(lb), 1), lw_all)

    lower, strict, eye_exp = _exp_masks()
    eye_f = eye_exp.astype(F32)

    cs = range(n_chunks)
    sl = [slice(c * CHUNK, (c + 1) * CHUNK) for c in cs]
    gc = [gc_all[s] for s in sl]
    kt = [kk_s[sl[c], :] * jnp.exp(gc[c] - lw_s[sl[c], :]) for c in cs]
    rt = [r_s[sl[c], :] * jnp.exp(gc[c]) for c in cs]
    a_kk, a_kb, a_rk, a_rb = [], [], [], []
    for c in cs:
        einv = jnp.exp(-gc[c])
        lhs = jnp.concatenate([kt[c], rt[c]], axis=0)
        rhs = jnp.concatenate([_bdx(k_s[sl[c], :] * einv, bd_m), _bdx(b_s[sl[c], :] * einv, bd_m)], axis=0)
        aa = _mm_nt(lhs, rhs)
        a_kk.append(jnp.where(strict, aa[0:CHUNK, 0:256], 0.0))
        a_kb.append(jnp.where(strict, aa[0:CHUNK, 256:512], 0.0))
        a_rk.append(jnp.where(lower, aa[CHUNK:, 0:256], 0.0))
        a_rb.append(jnp.where(lower, aa[CHUNK:, 256:512], 0.0))
    t = [eye_f - a_kb[c] for c in cs]
    p = [_mm(a_kb[c], _bdx(a_kb[c], bd_m)) for c in cs]
    for _ in range(4):
        m = [_mm(jnp.concatenate([p[c], t[c]], axis=0), _bdx(p[c], bd_m)) for c in cs]
        p = [m[c][0:CHUNK] for c in cs]
        t = [t[c] + m[c][CHUNK:] for c in cs]
    t = [t[c] + _mm(t[c], _bdx(p[c], bd_m)) for c in cs]
    av = [_mm(jnp.concatenate([a_kk[c], a_rk[c]], axis=0), _bdx(v_s[sl[c], :], bd_m)) for c in cs]
    tu = [_mm(t[c], jnp.concatenate([_bdx(av[c][0:CHUNK], bd_m), _bdx(kt[c], bd_m)], axis=1)) for c in cs]
    u0 = [tu[c][:, 0:256] for c in cs]
    tk = [tu[c][:, 256:512] for c in cs]
    ro = [_mm(a_rb[c], jnp.concatenate([_bdx(tk[c], bd_m), _bdx(u0[c], bd_m)], axis=1)) for c in cs]
    r_eff = [rt[c] - ro[c][:, 0:256] for c in cs]
    o0 = [av[c][CHUNK:] - ro[c][:, 256:512] for c in cs]
    a_st, b_st, decay = [], [], []
    for c in cs:
        gl = gc[c][CHUNK - 1:CHUNK]
        dk = jnp.exp(gl - gc[c])
        kd = k_s[sl[c], :] * dk
        bdk = b_s[sl[c], :] * dk
        a_st.append(jnp.where(bd, -_mm_tn(tk[c], bdk), 0.0))
        b_st.append(jnp.where(bd, _mm_tn(jnp.concatenate([v_s[sl[c], :], -u0[c]], axis=0),
                                         jnp.concatenate([kd, bdk], axis=0)), 0.0))
        decay.append(jnp.exp(gl))
    n = nbd_ref[...]
    o = []
    for c in cs:
        o.append(_mm_nt(r_eff[c], n) + o0[c])
        n = n * decay[c] + _mm(n, a_st[c]) + b_st[c]
    nbd_ref[...] = n
    o = jnp.concatenate(o, axis=0)
    mu = _mm_xc(o, pavg2)
    xc = o - mu
    var = _mm_xc(xc * xc, pavg2)
    on = xc * lax.rsqrt(var + RWKV_LN_EPS) * lg_ref[...] + lb_ref[...]
    bonus = _mm_xc(r_s[...] * k_s[...] * rk_ref[...], ones2) * v_s[...]
    y_ref[0] = ((on + bonus) * g_s[...]).astype(BF16)

    @pl.when(j == n_blocks - 1)
    def _():
        _get_block_diag(nbd_ref, so_ref, N_C, N_C)
        sho_ref[0] = carry_ref[...]


_RWKV_PARAMS = ("mu", "w0", "w2", "a0", "a2", "g2", "k_k", "k_a", "r_k", "ln_g", "ln_b")


def _rwkv_prompt(z3, sh0, s0, p, lb):
    bsz, seq, _ = z3.shape
    n_blocks = seq // lb
    st = pl.BlockSpec((1, N_HEADS, N_C, N_C), lambda i, j: (i, 0, 0, 0))
    sh = pl.BlockSpec((1, 1, RWKV_W), lambda i, j: (i, 0, 0))
    blk = pltpu.VMEM((lb, 256), F32)
    params = [p[k] for k in _RWKV_PARAMS]
    return pl.pallas_call(
        functools.partial(_rwkv_prompt_body, n_chunks=lb // CHUNK, n_blocks=n_blocks, lb=lb),
        grid=(bsz, n_blocks),
        in_specs=[pl.BlockSpec((1, lb, RWKV_W), lambda i, j: (i, j, RWKV_OFF // RWKV_W)), sh, st]
        + [_layer_spec(q) for q in params],
        out_specs=[pl.BlockSpec((1, lb, W_GROUP), lambda i, j: (i, j, 0)), st, sh],
        out_shape=[jax.ShapeDtypeStruct((bsz, seq, W_GROUP), BF16),
                   jax.ShapeDtypeStruct((bsz, N_HEADS, N_C, N_C), F32),
                   jax.ShapeDtypeStruct((bsz, 1, RWKV_W), F32)],
        scratch_shapes=[pltpu.VMEM((256, 256), F32), pltpu.VMEM((1, RWKV_W), F32),
                        blk, blk, blk, blk, blk, blk, blk],
        compiler_params=_cp("arbitrary", "arbitrary"),
        name="rwkv_prompt",
    )(z3, sh0, s0, *[q.arr for q in params])


def _mlstm_prompt_body(z_ref, zif_ref, cb0_ref, c0_ref, n0_ref, m0_ref, cw_ref, cbias_ref, bif_ref, ng_ref,
                       y_ref, co_ref, no_ref, mo_ref, cbo_ref,
                       cbd_ref, n_ref, m_ref, carry_ref, qk_s, *, n_chunks, n_blocks, lb):
    j = pl.program_id(1)

    @pl.when(j == 0)
    def _():
        _put_block_diag(cbd_ref, c0_ref[0], D_D, D_D)
        n_ref[...] = n0_ref[0]
        m_ref[...] = m0_ref[0]
        carry_ref[...] = cb0_ref[0]

    qk = z_ref[0, :, 0:512]
    conv = cbias_ref[...] + cw_ref[CONV_D - 1:CONV_D, :] * qk
    head = cbias_ref[...] + cw_ref[CONV_D - 1:CONV_D, :] * qk[0:8]
    row = _iota((8, 512), 0)
    for back in range(1, CONV_D):
        w = cw_ref[CONV_D - 1 - back:CONV_D - back, :]
        rolled = pltpu.roll(qk, back, 0)
        conv = conv + w * rolled
        fixed = rolled[0:8]
        for r in range(back):
            fixed = jnp.where(row == r, carry_ref[CONV_D - 1 - back + r:CONV_D - back + r, :], fixed)
        head = head + w * fixed
    qk_s[...] = _silu(jnp.concatenate([head, conv[8:]], axis=0))
    carry_ref[...] = qk[lb - (CONV_D - 1):lb]

    bd = _bd256()
    bd_m = _mask_bf16(bd)
    ones2 = _twice(bd, 0)
    pavg2 = _twice(bd.astype(F32) * (1.0 / D_D), 0)
    causal, _, eye_exp = _exp_masks()
    er = _iota((128, 512), 0)
    ec = _iota((128, 512), 1)
    expand2 = _twice(((ec < 256) & (er == (ec >> 6))) | ((ec >= 256) & (er == 4 + ((ec - 256) >> 6))), 0)
    is_f = (_iota((lb, 128), 1) >= 4) & (_iota((lb, 128), 1) < 8)
    lane = _iota((CHUNK, 256), 1)

    gates = zif_ref[0] + bif_ref[...]
    gates = jnp.where(is_f, _log_sigmoid(gates), gates)
    gates = jnp.where(is_f, _mm_cx(_twice(_tril_blocks(lb), 1), gates), gates)
    ge_all = _mm_xc(gates, expand2)

    cs = range(n_chunks)
    sl = [slice(c * CHUNK, (c + 1) * CHUNK) for c in cs]
    q = [qk_s[sl[c], 0:256] * D_D ** -0.5 for c in cs]
    k = [qk_s[sl[c], 256:512] for c in cs]
    iexp = [ge_all[sl[c], 0:256] for c in cs]
    fexp = [ge_all[sl[c], 256:512] for c in cs]
    qk_att = [_mm_nt(q[c], _bdx(k[c], bd_m)) for c in cs]
    dlog, m_t, m_prev = [], [], [m_ref[...]]
    for c in cs:
        d_row = jnp.sum(jnp.where(eye_exp, iexp[c] - fexp[c], 0.0), axis=0, keepdims=True)
        dl = jnp.where(causal, fexp[c] + d_row, -jnp.inf)
        mx = [jnp.max(dl[:, h * 64:(h + 1) * 64], axis=1, keepdims=True) for h in range(N_HEADS)]
        mx = jnp.where(lane < 64, mx[0], jnp.where(lane < 128, mx[1], jnp.where(lane < 192, mx[2], mx[3])))
        dlog.append(dl)
        m_t.append(jnp.maximum(fexp[c] + m_prev[c], mx))
        m_prev.append(m_t[c][CHUNK - 1:CHUNK])
    w0 = [jnp.exp(fexp[c] + m_prev[c] - m_t[c]) for c in cs]
    s = [jnp.exp(dlog[c] - m_t[c]) * qk_att[c] for c in cs]
    f_last = [fexp[c][CHUNK - 1:CHUNK] for c in cs]
    kw = [k[c] * jnp.exp(f_last[c] - fexp[c] + iexp[c] - m_prev[c + 1]) for c in cs]
    w0f = [jnp.exp(f_last[c] + m_prev[c] - m_prev[c + 1]) for c in cs]
    incr = [jnp.where(bd, _mm_tn(kw[c], z_ref[0, sl[c], 512:768]), 0.0) for c in cs]
    sv = [_mm(s[c], _bdx(z_ref[0, sl[c], 512:768], bd_m)) for c in cs]
    c_start, n_start = [cbd_ref[...]], [n_ref[...]]
    for c in cs:
        c_start.append(w0f[c] * c_start[c] + incr[c])
        n_start.append(w0f[c] * n_start[c] + jnp.sum(kw[c], axis=0, keepdims=True))
    cbd_ref[...] = c_start[n_chunks]
    n_ref[...] = n_start[n_chunks]
    m_ref[...] = m_prev[n_chunks]
    num = [w0[c] * _mm(q[c], c_start[c]) + sv[c] for c in cs]
    den = [_mm_xc(w0[c] * (q[c] * n_start[c]) + s[c], ones2) for c in cs]
    hh = jnp.concatenate([num[c] / jnp.maximum(jnp.abs(den[c]), jnp.exp(-m_t[c])) for c in cs], axis=0)
    ms = _mm_xc(hh * hh, pavg2)
    y_ref[0] = (_sigmoid(z_ref[0, :, 768:1024]) * (hh * lax.rsqrt(ms + 1e-6) * ng_ref[...])).astype(BF16)

    @pl.when(j == n_blocks - 1)
    def _():
        _get_block_diag(cbd_ref, co_ref, D_D, D_D)
        no_ref[0] = n_ref[...]
        mo_ref[0] = m_ref[...]
        cbo_ref[0] = carry_ref[...]


def _mlstm_prompt(z3, cb0, c0, n0, m0, cw, cbias, bif, ng, lb):
    bsz, seq, _ = z3.shape
    n_blocks = seq // lb
    cst = pl.BlockSpec((1, N_HEADS, D_D, D_D), lambda i, j: (i, 0, 0, 0))
    row = pl.BlockSpec((1, 1, 256), lambda i, j: (i, 0, 0))
    cbs = pl.BlockSpec((1, CONV_D - 1, 512), lambda i, j: (i, 0, 0))
    return pl.pallas_call(
        functools.partial(_mlstm_prompt_body, n_chunks=lb // CHUNK, n_blocks=n_blocks, lb=lb),
        grid=(bsz, n_blocks),
        in_specs=[pl.BlockSpec((1, lb, ML_W), lambda i, j: (i, j, ML_OFF // ML_W)),
                  pl.BlockSpec((1, lb, MLIF_W), lambda i, j: (i, j, MLIF_OFF // MLIF_W)),
                  cbs, cst, row, row, _layer_spec(cw), _layer_spec(cbias), _layer_spec(bif), _layer_spec(ng)],
        out_specs=[pl.BlockSpec((1, lb, W_GROUP), lambda i, j: (i, j, 0)), cst, row, row, cbs],
        out_shape=[jax.ShapeDtypeStruct((bsz, seq, W_GROUP), BF16),
                   jax.ShapeDtypeStruct((bsz, N_HEADS, D_D, D_D), F32),
                   jax.ShapeDtypeStruct((bsz, 1, 256), F32),
                   jax.ShapeDtypeStruct((bsz, 1, 256), F32),
                   jax.ShapeDtypeStruct((bsz, CONV_D - 1, 512), F32)],
        scratch_shapes=[pltpu.VMEM((256, 256), F32), pltpu.VMEM((1, 256), F32), pltpu.VMEM((1, 256), F32),
                        pltpu.VMEM((CONV_D - 1, 512), F32), pltpu.VMEM((lb, 512), F32)],
        compiler_params=_cp("arbitrary", "arbitrary"),
        name="mlstm_prompt",
    )(z3, z3, cb0, c0, n0, m0, cw.arr, cbias.arr, bif.arr, ng.arr)


T_COLS = 1024


def _to_rows(src_ref, dst_ref, n):
    for i in range(n // T_COLS):
        dst_ref[i * T_COLS:(i + 1) * T_COLS, :] = src_ref[:, i * T_COLS:(i + 1) * T_COLS].T


def _to_cols(src_ref, dst_ref, n):
    for i in range(n // T_COLS):
        dst_ref[:, i * T_COLS:(i + 1) * T_COLS] = src_ref[i * T_COLS:(i + 1) * T_COLS, :].T


def _per_head(x, fn):
    return jnp.concatenate([fn(x[h * 64:(h + 1) * 64], h) for h in range(N_HEADS)], axis=0)


def _expand_heads(x, nb):
    return jnp.concatenate([jnp.broadcast_to(x[h:h + 1], (64, nb)) for h in range(N_HEADS)], axis=0)


def _gla_sample_body(z_ref, s_ref, wa_ref, ba_ref, ng_ref, y_ref, so_ref,
                     st_s, dec_s, k_s, q_s, v_s, o_s, *, steps, nb):
    n_state = N_HEADS * DK_A * DV_A
    _to_rows(s_ref, st_s, n_state)
    for t in range(steps):
        zt = z_ref[t].T
        q_s[t] = zt[0:128] * DK_A ** -0.5
        k_s[t] = zt[128:256]
        v_s[t] = zt[256:512]
        lg = _log_sigmoid(_mm_hi(wa_ref[...], zt[768:896]) + ba_ref[...]) * (1.0 / GLA_TAU)
        dec_s[t] = jnp.exp(lg)
    o_s[...] = jnp.zeros(o_s.shape, F32)

    def per_key(hk, carry):
        r0 = pl.multiple_of(hk * DV_A, DV_A)
        v0 = pl.multiple_of((hk >> 5) * DV_A, DV_A)
        s = st_s[pl.ds(r0, DV_A), :]
        for t in range(steps):
            s = dec_s[t, pl.ds(hk, 1), :] * s + k_s[t, pl.ds(hk, 1), :] * v_s[t, pl.ds(v0, DV_A), :]
            o_s[t, pl.ds(v0, DV_A), :] += q_s[t, pl.ds(hk, 1), :] * s
        st_s[pl.ds(r0, DV_A), :] = s
        return carry

    lax.fori_loop(0, N_HEADS * DK_A, per_key, 0)
    for t in range(steps):
        g = z_ref[t, :, 512:768].T
        on = _per_head(o_s[t], lambda oh, h: oh * lax.rsqrt(jnp.mean(oh * oh, axis=0, keepdims=True) + 1e-6))
        y_ref[t] = (on * ng_ref[...] * _silu(g)).T.astype(BF16)
    _to_cols(st_s, so_ref, n_state)


def _gla_sample(zs, s, wa_t, ba_col, ng_col):
    steps, nb, _ = zs.shape
    n_state = N_HEADS * DK_A * DV_A
    full = lambda shape: pl.BlockSpec(shape, lambda i: (0,) * len(shape))
    return pl.pallas_call(
        functools.partial(_gla_sample_body, steps=steps, nb=nb),
        grid=(1,),
        in_specs=[pl.BlockSpec((steps, nb, GLA_W), lambda i: (0, 0, GLA_OFF // GLA_W)), _layer_spec(s),
                  _layer_spec(wa_t), _layer_spec(ba_col), _layer_spec(ng_col)],
        out_specs=[full((steps, nb, W_GROUP)), _layer_spec(s)],
        out_shape=[jax.ShapeDtypeStruct((steps, nb, W_GROUP), BF16), _stacked_like(s)],
        input_output_aliases={1: 1},
        scratch_shapes=[pltpu.VMEM((n_state, nb), F32), pltpu.VMEM((steps, 128, nb), F32),
                        pltpu.VMEM((steps, 128, nb), F32), pltpu.VMEM((steps, 128, nb), F32),
                        pltpu.VMEM((steps, 256, nb), F32), pltpu.VMEM((steps, 256, nb), F32)],
        compiler_params=_cp("arbitrary"),
        name="gla_sample",
    )(zs, s.arr, wa_t.arr, ba_col.arr, ng_col.arr)


def _pool_sample_body(z_ref, buf_ref, w_ref, sc_ref, y_ref, bo_ref, *, steps, nb, start_pos):
    ext = [buf_ref[:, j * W_GROUP:(j + 1) * W_GROUP] for j in range(POOL_BUF)] + [z_ref[t] for t in range(steps)]
    lane = _iota((nb, W_GROUP), 1)

    def by_group(vals):
        return jnp.where(lane < 64, vals[0], jnp.where(lane < 128, vals[1], jnp.where(lane < 192, vals[2], vals[3])))

    for t in range(steps):
        r = POOL_BUF + t
        acc = ext[r]
        sums = {}
        for back in range(1, max(POOL_WINDOWS)):
            acc = acc + ext[r - back]
            sums[back + 1] = acc
        win = by_group([sums[w] for w in POOL_WINDOWS])
        cnt = by_group([float(min(start_pos + t + 1, w)) for w in POOL_WINDOWS])
        y_ref[t] = (_mm(win / cnt - ext[r], w_ref[...]) * sc_ref[...]).astype(BF16)
    for j in range(POOL_BUF):
        bo_ref[:, j * W_GROUP:(j + 1) * W_GROUP] = ext[steps + j]


def _pool_sample(zs, buf, wbd, scale, start_pos):
    steps, nb, _ = zs.shape
    full = lambda shape: pl.BlockSpec(shape, lambda i: (0,) * len(shape))
    return pl.pallas_call(
        functools.partial(_pool_sample_body, steps=steps, nb=nb, start_pos=start_pos),
        grid=(1,),
        in_specs=[pl.BlockSpec((steps, nb, W_GROUP), lambda i: (0, 0, POOL_OFF // W_GROUP)),
                  _layer_spec(buf), _layer_spec(wbd), _layer_spec(scale)],
        out_specs=[full((steps, nb, W_GROUP)), _layer_spec(buf)],
        out_shape=[jax.ShapeDtypeStruct((steps, nb, W_GROUP), BF16), _stacked_like(buf)],
        input_output_aliases={1: 1},
        compiler_params=_cp("arbitrary"),
        name="pool_sample",
    )(zs, buf.arr, wbd.arr, scale.arr)


def _rwkv_sample_body(z_ref, sh_ref, s_ref, mu_ref, w0_ref, w2_ref, a0_ref, a2_ref, g2_ref, kk_ref, ka_ref,
                      rk_ref, lg_ref, lb_ref, y_ref, so_ref, sho_ref,
                      st_s, r_s, k_s, v_s, kk_s, b_s, w_s, g_s, o_s, *, steps, nb):
    n_state = N_HEADS * N_C * N_C
    _to_rows(s_ref, st_s, n_state)
    prev = sh_ref[...].T
    for t in range(steps):
        f = z_ref[t].T
        mixed = f + (prev - f) * mu_ref[...]
        prev = f
        k = mixed[256:512]
        low = mixed[768:896]
        w_s[t] = jnp.exp(-_sigmoid(w0_ref[...] + _mm_hi(w2_ref[...], jnp.tanh(low))) * math.exp(-0.5))
        a = _sigmoid(a0_ref[...] + _mm_hi(a2_ref[...], low))
        g_s[t] = _mm_hi(g2_ref[...], _sigmoid(low))
        kk = _per_head(k * kk_ref[...], lambda kh, h: kh * lax.rsqrt(
            jnp.maximum(jnp.sum(kh * kh, axis=0, keepdims=True), 1e-24)))
        r_s[t] = mixed[0:256]
        v_s[t] = mixed[512:768]
        kk_s[t] = kk
        k_s[t] = k * (1.0 + (a - 1.0) * ka_ref[...])
        b_s[t] = kk * a

    def per_value(hv, carry):
        r0 = pl.multiple_of(hv * N_C, N_C)
        hs = pl.ds(pl.multiple_of((hv >> 6) * N_C, N_C), N_C)
        s = st_s[pl.ds(r0, N_C), :]
        for t in range(steps):
            sa = -jnp.sum(s * kk_s[t, hs, :], axis=0, keepdims=True)
            s = s * w_s[t, hs, :] + sa * b_s[t, hs, :] + v_s[t, pl.ds(hv, 1), :] * k_s[t, hs, :]
            o_s[t, pl.ds(hv, 1), :] = jnp.sum(s * r_s[t, hs, :], axis=0, keepdims=True)
        st_s[pl.ds(r0, N_C), :] = s
        return carry

    lax.fori_loop(0, N_HEADS * N_C, per_value, 0, unroll=4)
    for t in range(steps):
        def norm(oh, h):
            mu = jnp.mean(oh, axis=0, keepdims=True)
            xc = oh - mu
            return xc * lax.rsqrt(jnp.mean(xc * xc, axis=0, keepdims=True) + RWKV_LN_EPS)
        on = _per_head(o_s[t], norm) * lg_ref[...] + lb_ref[...]
        rk = r_s[t] * k_s[t] * rk_ref[...]
        v = v_s[t]
        bonus = _per_head(rk, lambda x, h: jnp.sum(x, axis=0, keepdims=True) * v[h * 64:(h + 1) * 64])
        y_ref[t] = ((on + bonus) * g_s[t]).T.astype(BF16)
    _to_cols(st_s, so_ref, n_state)
    sho_ref[...] = z_ref[steps - 1]


def _rwkv_sample(zs, sh, s, p):
    steps, nb, _ = zs.shape
    n_state = N_HEADS * N_C * N_C
    full = lambda shape: pl.BlockSpec(shape, lambda i: (0,) * len(shape))
    blk = pltpu.VMEM((steps, 256, nb), F32)
    params = [p[k + "_c"] for k in _RWKV_PARAMS]
    return pl.pallas_call(
        functools.partial(_rwkv_sample_body, steps=steps, nb=nb),
        grid=(1,),
        in_specs=[pl.BlockSpec((steps, nb, RWKV_W), lambda i: (0, 0, RWKV_OFF // RWKV_W)), _layer_spec(sh),
                  _layer_spec(s)] + [_layer_spec(q) for q in params],
        out_specs=[full((steps, nb, W_GROUP)), _layer_spec(s), _layer_spec(sh)],
        out_shape=[jax.ShapeDtypeStruct((steps, nb, W_GROUP), BF16), _stacked_like(s), _stacked_like(sh)],
        input_output_aliases={2: 1, 1: 2},
        scratch_shapes=[pltpu.VMEM((n_state, nb), F32), blk, blk, blk, blk, blk, blk, blk, blk],
        compiler_params=_cp("arbitrary"),
        name="rwkv_sample",
    )(zs, sh.arr, s.arr, *[q.arr for q in params])


def _mlstm_sample_body(z_ref, zif_ref, cb_ref, c_ref, n_ref, m_ref, cw_ref, cbias_ref, bif_ref, ng_ref,
                       y_ref, co_ref, no_ref, mo_ref, cbo_ref,
                       ct_s, q_s, k_s, v_s, wf_s, den_s, em_s, o_s, *, steps, nb):
    n_state = N_HEADS * D_D * D_D
    _to_rows(c_ref, ct_s, n_state)
    n = n_ref[...].T
    m = m_ref[...].T[0:N_HEADS]
    hist = [cb_ref[:, j * 512:(j + 1) * 512].T for j in range(CONV_D - 1)]
    for t in range(steps):
        hist = hist + [z_ref[t, :, 0:512].T]
        conv = cbias_ref[...]
        for jj in range(CONV_D):
            conv = conv + cw_ref[:, jj:jj + 1] * hist[jj]
        hist = hist[1:]
        act = _silu(conv)
        q = act[0:256] * D_D ** -0.5
        k = act[256:512]
        gates = zif_ref[t].T + bif_ref[...]
        ig = gates[0:N_HEADS]
        lf = _log_sigmoid(gates[N_HEADS:2 * N_HEADS])
        m_new = jnp.maximum(lf + m, ig)
        wf = _expand_heads(jnp.exp(lf + m - m_new), nb)
        kw = _expand_heads(jnp.exp(ig - m_new), nb) * k
        m = m_new
        n = wf * n + kw
        den = _per_head(q * n, lambda x, h: jnp.broadcast_to(jnp.sum(x, axis=0, keepdims=True), (64, nb)))
        q_s[t] = q
        k_s[t] = kw
        v_s[t] = z_ref[t, :, 512:768].T
        wf_s[t] = wf
        den_s[t] = den
        em_s[t] = _expand_heads(jnp.exp(-m_new), nb)
    o_s[...] = jnp.zeros(o_s.shape, F32)

    def per_key(hd, carry):
        r0 = pl.multiple_of(hd * D_D, D_D)
        e0 = pl.multiple_of((hd >> 6) * D_D, D_D)
        c = ct_s[pl.ds(r0, D_D), :]
        for t in range(steps):
            c = wf_s[t, pl.ds(hd, 1), :] * c + k_s[t, pl.ds(hd, 1), :] * v_s[t, pl.ds(e0, D_D), :]
            o_s[t, pl.ds(e0, D_D), :] += q_s[t, pl.ds(hd, 1), :] * c
        ct_s[pl.ds(r0, D_D), :] = c
        return carry

    lax.fori_loop(0, N_HEADS * D_D, per_key, 0)
    for t in range(steps):
        hh = o_s[t] / jnp.maximum(jnp.abs(den_s[t]), em_s[t])
        hn = _per_head(hh, lambda x, h: x * lax.rsqrt(jnp.mean(x * x, axis=0, keepdims=True) + 1e-6))
        y_ref[t] = (_sigmoid(z_ref[t, :, 768:1024].T) * (hn * ng_ref[...])).T.astype(BF16)
    _to_cols(ct_s, co_ref, n_state)
    no_ref[...] = n.T
    mo_ref[...] = jnp.concatenate([m, jnp.zeros((8 - N_HEADS, nb), F32)], axis=0)
    for j in range(CONV_D - 1):
        cbo_ref[:, j * 512:(j + 1) * 512] = hist[j].T


def _mlstm_sample(zs, cb, c, n, m_pad, cw_t, cbias_col, bif_col, ng_col):
    steps, nb, _ = zs.shape
    n_state = N_HEADS * D_D * D_D
    full = lambda shape: pl.BlockSpec(shape, lambda i: (0,) * len(shape))
    blk = pltpu.VMEM((steps, 256, nb), F32)
    return pl.pallas_call(
        functools.partial(_mlstm_sample_body, steps=steps, nb=nb),
        grid=(1,),
        in_specs=[pl.BlockSpec((steps, nb, ML_W), lambda i: (0, 0, ML_OFF // ML_W)),
                  pl.BlockSpec((steps, nb, MLIF_W), lambda i: (0, 0, MLIF_OFF // MLIF_W)),
                  _layer_spec(cb), _layer_spec(c), _layer_spec(n), _layer_spec(m_pad),
                  _layer_spec(cw_t), _layer_spec(cbias_col), _layer_spec(bif_col), _layer_spec(ng_col)],
        out_specs=[full((steps, nb, W_GROUP)), _layer_spec(c), _layer_spec(n), full((8, nb)), _layer_spec(cb)],
        out_shape=[jax.ShapeDtypeStruct((steps, nb, W_GROUP), BF16), _stacked_like(c), _stacked_like(n),
                   jax.ShapeDtypeStruct((8, nb), F32), _stacked_like(cb)],
        input_output_aliases={3: 1, 4: 2, 2: 4},
        scratch_shapes=[pltpu.VMEM((n_state, nb), F32), blk, blk, blk, blk, blk, blk, blk],
        compiler_params=_cp("arbitrary"),
        name="mlstm_sample",
    )(zs, zs, cb.arr, c.arr, n.arr, m_pad.arr, cw_t.arr, cbias_col.arr, bif_col.arr, ng_col.arr)


def _pad_cols(x, n):
    return jnp.pad(x, [(0, 0)] * (x.ndim - 1) + [(0, n - x.shape[-1])])


def _rows_at(x, row0, n_rows):
    return jnp.pad(x, ((0, 0), (row0, n_rows - row0 - x.shape[1]), (0, 0)))


def _stacked_params(w):
    gla, pool, rwkv, ml = jnp.split(w["w_in"], [784, 784 + 256, 784 + 256 + 832], axis=2)
    w_in_p = jnp.concatenate([_pad_cols(gla, GLA_W), _pad_cols(rwkv, RWKV_W), pool, ml[:, :, :1024],
                              _pad_cols(ml[:, :, 1024:], MLIF_W)], axis=2).astype(BF16)
    col = lambda v: v.reshape(DEPTH, -1, 1)
    row = lambda v: v.reshape(DEPTH, 1, -1)
    tr = lambda m: m.transpose(0, 2, 1)
    wa = _rows_at(w["gla_w_a2"], 0, 128)
    w2 = _rows_at(w["rwkv_w2"], 0, 128)
    a2 = _rows_at(w["rwkv_a2"], R_W, 128)
    g2 = _rows_at(w["rwkv_g2"], R_W + R_AA, 128)
    mu = _pad_cols(w["rwkv_mu"], RWKV_W)
    bif = _pad_cols(jnp.concatenate([w["mlstm_b_i"], w["mlstm_b_f"]], axis=1), 128)
    wbd = jnp.zeros((DEPTH, 256, 256), F32)
    for gi in range(4):
        wbd = wbd.at[:, gi * 64:(gi + 1) * 64, gi * 64:(gi + 1) * 64].set(w["pool_w"][:, gi])
    vecs = dict(w0=w["rwkv_w0"], a0=w["rwkv_a0"], k_k=w["rwkv_k_k"], k_a=w["rwkv_k_a"], r_k=w["rwkv_r_k"],
                ln_g=w["rwkv_ln_g"], ln_b=w["rwkv_ln_b"], mu=mu)
    rw = {k: row(v) for k, v in vecs.items()}
    rw.update({k + "_c": col(v) for k, v in vecs.items()})
    rw.update(w2=w2, a2=a2, g2=g2, w2_c=tr(w2), a2_c=tr(a2), g2_c=tr(g2))
    return dict(
        w_in=w_in_p, w_out=w["w_out"].astype(BF16), ln1_g=row(w["ln1_g"]), ln1_b=row(w["ln1_b"]),
        w_up=w["ffn_w_up"].astype(BF16), ffn_cw=w["ffn_conv_w"], ffn_cb=row(w["ffn_conv_b"]),
        w_down=w["ffn_w_down"].astype(BF16), ln2_g=row(w["ln2_g"]), ln2_b=row(w["ln2_b"]),
        gla_wa=wa, gla_wa_t=tr(wa), gla_ba=row(w["gla_b_a"]), gla_ba_c=col(w["gla_b_a"]),
        gla_ng=row(w["gla_norm_g"]), gla_ng_c=col(w["gla_norm_g"]),
        pool_w=wbd, pool_scale=row(w["pool_scale"]), rwkv=rw,
        ml_cw=w["mlstm_conv_w"], ml_cw_t=tr(w["mlstm_conv_w"]), ml_cb=row(w["mlstm_conv_b"]),
        ml_cb_c=col(w["mlstm_conv_b"]), ml_bif=row(bif), ml_bif_c=col(bif),
        ml_ng=row(w["mlstm_norm_g"]), ml_ng_c=col(w["mlstm_norm_g"]))


def _layer_view(stacked, l):
    return {k: (_layer_view(v, l) if isinstance(v, dict) else _Layer(v, l)) for k, v in stacked.items()}


RWKV_BLOCK = 512
GLA_BLOCK = 512
MLSTM_BLOCK = 512
POOL_BLOCK = 1024
DENSE_TILE = 1024
FFN_TILE = 512


def _prompt_layer(x, p):
    bsz, seq, d = x.shape
    z = _linear(x.reshape(bsz * seq, d), p["w_in"], DENSE_TILE).reshape(bsz, seq, ZC)
    zeros = lambda *s: jnp.zeros((bsz,) + s, F32)
    y_a, s_gla = _gla_prompt(z, zeros(N_HEADS, DK_A, DV_A), p["gla_wa"], p["gla_ba"], p["gla_ng"], GLA_BLOCK)
    y_b, s_pool = _pool_prompt(z, zeros(POOL_BUF, W_GROUP), p["pool_w"], p["pool_scale"], POOL_BLOCK, 0)
    y_c, s_rwkv, s_shift = _rwkv_prompt(z, zeros(1, RWKV_W), zeros(N_HEADS, N_C, N_C), p["rwkv"], RWKV_BLOCK)
    y_d, s_c, s_n, s_m, s_conv = _mlstm_prompt(z, zeros(CONV_D - 1, 512), zeros(N_HEADS, D_D, D_D), zeros(1, 256),
                                               zeros(1, 256), p["ml_cw"], p["ml_cb"], p["ml_bif"], p["ml_ng"],
                                               MLSTM_BLOCK)
    x2, s_ffn = _ffn_prompt([y_a, y_b, y_c, y_d], x, zeros(FFN_CONV - 1, 2 * D_FF), p, FFN_TILE)
    states = (s_gla, s_pool, s_rwkv, s_shift[:, :, :RWKV_COLS], s_c, s_n.reshape(bsz, N_HEADS, D_D),
              s_m[:, 0, ::D_D], s_conv, s_ffn)
    return x2, states


def _sample_layer(x, st, p, steps, nb):
    s_gla, s_pool, s_rwkv, s_shift, s_c, s_n, s_m, s_conv, s_ffn = st
    z = _linear(x, p["w_in"], steps * nb).reshape(steps, nb, ZC)
    y_a, n_gla = _gla_sample(z, s_gla, p["gla_wa_t"], p["gla_ba_c"], p["gla_ng_c"])
    y_b, n_pool = _pool_sample(z, s_pool, p["pool_w"], p["pool_scale"], PAST_LEN)
    y_c, n_rwkv, n_shift = _rwkv_sample(z, s_shift, s_rwkv, p["rwkv"])
    y_d, n_c, n_n, n_m, n_conv = _mlstm_sample(z, s_conv, s_c, s_n, s_m, p["ml_cw_t"], p["ml_cb_c"], p["ml_bif_c"],
                                               p["ml_ng_c"])
    flat = lambda y: y.reshape(steps * nb, W_GROUP)
    x2, n_ffn = _ffn_sample([flat(y_a), flat(y_b), flat(y_c), flat(y_d)], x, s_ffn, p, nb, steps)
    return x2, (n_gla, n_pool, n_rwkv, n_shift, n_c, n_n, n_m, n_conv, n_ffn)


def kernel(x_prompt, x_sample, state_gla, state_pool, state_rwkv, state_rwkv_shift, state_mlstm_c, state_mlstm_n, state_mlstm_m, state_mlstm_conv, state_ffn_conv, w_in, gla_w_a2, gla_b_a, gla_norm_g, pool_w, pool_scale, rwkv_mu, rwkv_w0, rwkv_w2, rwkv_a0, rwkv_a2, rwkv_g2, rwkv_k_k, rwkv_k_a, rwkv_r_k, rwkv_ln_g, rwkv_ln_b, mlstm_conv_w, mlstm_conv_b, mlstm_b_i, mlstm_b_f, mlstm_norm_g, w_out, ln1_g, ln1_b, ffn_w_up, ffn_conv_w, ffn_conv_b, ffn_w_down, ln2_g, ln2_b):
    w = dict(w_in=w_in, gla_w_a2=gla_w_a2, gla_b_a=gla_b_a, gla_norm_g=gla_norm_g, pool_w=pool_w,
             pool_scale=pool_scale, rwkv_mu=rwkv_mu, rwkv_w0=rwkv_w0, rwkv_w2=rwkv_w2, rwkv_a0=rwkv_a0,
             rwkv_a2=rwkv_a2, rwkv_g2=rwkv_g2, rwkv_k_k=rwkv_k_k, rwkv_k_a=rwkv_k_a, rwkv_r_k=rwkv_r_k,
             rwkv_ln_g=rwkv_ln_g, rwkv_ln_b=rwkv_ln_b, mlstm_conv_w=mlstm_conv_w, mlstm_conv_b=mlstm_conv_b,
             mlstm_b_i=mlstm_b_i, mlstm_b_f=mlstm_b_f, mlstm_norm_g=mlstm_norm_g, w_out=w_out, ln1_g=ln1_g,
             ln1_b=ln1_b, ffn_w_up=ffn_w_up, ffn_conv_w=ffn_conv_w, ffn_conv_b=ffn_conv_b, ffn_w_down=ffn_w_down,
             ln2_g=ln2_g, ln2_b=ln2_b)
    sample_states = (state_gla, state_pool, state_rwkv, state_rwkv_shift, state_mlstm_c, state_mlstm_n,
                     state_mlstm_m, state_mlstm_conv, state_ffn_conv)
    nb, steps, d = x_sample.shape
    flat_in = [s.reshape(DEPTH, nb, -1) for s in sample_states]
    flat_in[3] = _pad_cols(flat_in[3], RWKV_W)
    flat_in[6] = _pad_cols(flat_in[6], 128)
    stacked = _stacked_params(w)
    yp = x_prompt
    ys = x_sample.transpose(1, 0, 2).reshape(steps * nb, d)
    acc_p = [[] for _ in sample_states]
    m_new = []
    flat = list(flat_in)
    for l in range(DEPTH):
        p = _layer_view(stacked, l)
        yp, st_p = _prompt_layer(yp, p)
        ys, st_s = _sample_layer(ys, tuple(_Layer(s, l) for s in flat), p, steps, nb)
        for i in range(len(sample_states)):
            acc_p[i].append(st_p[i])
            if i == 6:
                m_new.append(st_s[i])
            else:
                flat[i] = st_s[i]
    ys = ys.reshape(steps, nb, d).transpose(1, 0, 2)
    out_s = list(flat)
    out_s[3] = out_s[3][:, :, :RWKV_COLS]
    out_s[6] = jnp.stack(m_new)[:, 0:N_HEADS, :].transpose(0, 2, 1)
    out = [yp, ys]
    for sp, ss, ref in zip(acc_p, out_s, sample_states):
        out.append(jnp.stack(sp))
        out.append(ss.reshape(ref.shape))
    return tuple(out)
```

```python
import functools
import math

import jax
import jax.numpy as jnp
from jax import lax
from jax.experimental import pallas as pl
from jax.experimental.pallas import tpu as pltpu

F32 = jnp.float32
BF16 = jnp.bfloat16
HI = lax.Precision.HIGHEST

D_MODEL = 1024
DEPTH = 4
PAST_LEN = 16384
W_GROUP = 256
N_HEADS = 4
DK_A = 32
DV_A = 64
R_GLA = 16
GLA_TAU = 16.0
POOL_WINDOWS = (2, 4, 8, 16)
POOL_BUF = 15
N_C = 64
R_W, R_AA, R_G = 16, 16, 32
RWKV_COLS = 832
RWKV_LN_EPS = 64e-5
D_D = 64
CONV_D = 4
D_FF = 2816
FFN_CONV = 3
ALPHA = (2 * DEPTH) ** 0.25
CHUNK = 64

GLA_W = 896
RWKV_W = 896
GLA_OFF, RWKV_OFF, POOL_OFF, ML_OFF, MLIF_OFF = 0, 896, 1792, 2048, 3072
ZC = 3200
ML_W = 1024
MLIF_W = 128

VMEM_LIMIT = 56 * 1024 * 1024


def _cp(*sem):
    return pltpu.CompilerParams(dimension_semantics=sem, vmem_limit_bytes=VMEM_LIMIT)


def _mm(a, b):
    return jnp.dot(a.astype(BF16), b.astype(BF16), preferred_element_type=F32)


def _mm_nt(a, b):
    return lax.dot_general(a.astype(BF16), b.astype(BF16), (((1,), (1,)), ((), ())), preferred_element_type=F32)


def _mm_tn(a, b):
    return lax.dot_general(a.astype(BF16), b.astype(BF16), (((0,), (0,)), ((), ())), preferred_element_type=F32)


def _mm_hi(a, b):
    return jnp.dot(a, b, preferred_element_type=F32, precision=HI)


def _split2(x):
    hi = x.astype(BF16)
    return hi, (x - hi.astype(F32)).astype(BF16)


def _mm_xc(x, c2):
    hi, mid = _split2(x)
    return jnp.dot(jnp.concatenate([hi, mid], axis=1), c2, preferred_element_type=F32)


def _mm_cx(c2, x):
    hi, mid = _split2(x)
    return jnp.dot(c2, jnp.concatenate([hi, mid], axis=0), preferred_element_type=F32)


def _twice(c, axis):
    c = c.astype(F32).astype(BF16)
    return jnp.concatenate([c, c], axis=axis)


def _iota(shape, dim):
    return lax.broadcasted_iota(jnp.int32, shape, dim)


def _sigmoid(x):
    return 1.0 / (1.0 + jnp.exp(-x))


def _silu(x):
    return x * _sigmoid(x)


def _log_sigmoid(x):
    return jnp.minimum(x, 0.0) - jnp.log(1.0 + jnp.exp(-jnp.abs(x)))


def _tile4(x):
    return jnp.concatenate([x, x, x, x], axis=0)


def _block_diag_mask(rows, cols, rshift, cshift):
    return (_iota((rows, cols), 0) >> rshift) == (_iota((rows, cols), 1) >> cshift)


def _layer_norm(h, g, b, eps):
    mu = jnp.mean(h, axis=-1, keepdims=True)
    xc = h - mu
    var = jnp.mean(xc * xc, axis=-1, keepdims=True)
    return xc * lax.rsqrt(var + eps) * g + b


class _Layer:
    def __init__(self, arr, l):
        self.arr, self.l, self.shape = arr, l, tuple(arr.shape[1:])


def _layer_spec(p, single_buffer=False):
    n, l = len(p.shape), p.l
    if single_buffer:
        return pl.BlockSpec((None,) + p.shape, lambda *g: (l,) + (0,) * n, pipeline_mode=pl.Buffered(1))
    return pl.BlockSpec((None,) + p.shape, lambda *g: (l,) + (0,) * n)


def _stacked_like(p):
    return jax.ShapeDtypeStruct(p.arr.shape, p.arr.dtype)


def _linear_body(x_ref, w_ref, o_ref):
    o_ref[...] = jnp.dot(x_ref[...].astype(BF16), w_ref[...], preferred_element_type=F32)


def _linear(x, w, tm):
    t, k = x.shape
    n = w.shape[1]
    return pl.pallas_call(
        _linear_body,
        grid=(t // tm,),
        in_specs=[pl.BlockSpec((tm, k), lambda i: (i, 0)), _layer_spec(w, single_buffer=True)],
        out_specs=pl.BlockSpec((tm, n), lambda i: (i, 0)),
        out_shape=jax.ShapeDtypeStruct((t, n), F32),
        compiler_params=_cp("arbitrary"),
        name="in_proj",
    )(x, w.arr)


def _outproj(ys, x, w_ref, g_ref, b_ref):
    y = jnp.concatenate(ys, axis=1).astype(BF16)
    mixed = jnp.dot(y, w_ref[...], preferred_element_type=F32)
    return _layer_norm(ALPHA * x + mixed, g_ref[...], b_ref[...], 1e-5)


FF_CHUNK = 256
N_FF_CHUNKS = D_FF // FF_CHUNK
UP_AHEAD = 2
DOWN_ROWS = 256


def _ffn_prompt_body(ya_ref, yb_ref, yc_ref, yd_ref, x_ref, buf_ref, wo_ref, g1_ref, b1_ref,
                     wup_ref, cw_ref, cb_ref, wdn_ref, g_ref, b_ref,
                     o_ref, st_ref, carry_ref, h_ref, x1_ref, *, tm, n_tiles):
    j = pl.program_id(1)

    @pl.when(j == 0)
    def _():
        carry_ref[...] = buf_ref[0]

    x = _outproj([ya_ref[0], yb_ref[0], yc_ref[0], yd_ref[0]], x_ref[0], wo_ref, g1_ref, b1_ref)
    x1_ref[...] = x
    xb = x.astype(BF16)
    row = _iota((8, FF_CHUNK), 0)
    col = lambda c, half: slice(half * D_FF + c * FF_CHUNK, half * D_FF + (c + 1) * FF_CHUNK)
    up = lambda c: [jnp.dot(xb, wup_ref[:, col(c, half)], preferred_element_type=F32) for half in range(2)]
    ahead = [up(c) for c in range(UP_AHEAD)]
    for c in range(N_FF_CHUNKS):
        if c + UP_AHEAD < N_FF_CHUNKS:
            ahead.append(up(c + UP_AHEAD))
        u_now = ahead.pop(0)
        acts = []
        for half in range(2):
            sl = col(c, half)
            u = u_now[half]
            two_back, one_back = carry_ref[0:1, sl], carry_ref[1:2, sl]
            w0, w1, w2 = cw_ref[0:1, sl], cw_ref[1:2, sl], cw_ref[2:3, sl]
            r1 = pltpu.roll(u, 1, 0)
            r2 = pltpu.roll(u, 2, 0)
            r1_head = jnp.where(row == 0, one_back, r1[0:8])
            r2_head = jnp.where(row == 0, two_back, jnp.where(row == 1, one_back, r2[0:8]))
            head = cb_ref[:, sl] + w0 * r2_head + w1 * r1_head + w2 * u[0:8]
            body = cb_ref[:, sl] + w0 * r2 + w1 * r1 + w2 * u
            carry_ref[:, sl] = u[tm - 2:tm]
            acts.append(jnp.concatenate([head, body[8:]], axis=0))
        h_ref[:, c * FF_CHUNK:(c + 1) * FF_CHUNK] = (_silu(acts[0]) * acts[1]).astype(BF16)
    for r0 in range(0, tm, DOWN_ROWS):
        rows = slice(r0, r0 + DOWN_ROWS)
        f = jnp.dot(h_ref[rows, :], wdn_ref[...], preferred_element_type=F32)
        o_ref[0, rows, :] = _layer_norm(ALPHA * x1_ref[rows, :] + f, g_ref[...], b_ref[...], 1e-5)

    @pl.when(j == n_tiles - 1)
    def _():
        st_ref[0] = carry_ref[...]


_DENSE_PARAMS = ("w_out", "ln1_g", "ln1_b", "w_up", "ffn_cw", "ffn_cb", "w_down", "ln2_g", "ln2_b")


def _ffn_prompt(ys, x3, buf, p, tm):
    bsz, seq, d = x3.shape
    n_tiles = seq // tm
    params = [p[k] for k in _DENSE_PARAMS]
    tok = lambda w: pl.BlockSpec((1, tm, w), lambda i, j: (i, j, 0))
    st = pl.BlockSpec((1, 2, 2 * D_FF), lambda i, j: (i, 0, 0))
    return pl.pallas_call(
        functools.partial(_ffn_prompt_body, tm=tm, n_tiles=n_tiles),
        grid=(bsz, n_tiles),
        in_specs=[tok(W_GROUP)] * 4 + [tok(d), st] + [_layer_spec(q, single_buffer=True) for q in params],
        out_specs=[tok(d), st],
        out_shape=[jax.ShapeDtypeStruct((bsz, seq, d), F32), jax.ShapeDtypeStruct((bsz, 2, 2 * D_FF), F32)],
        scratch_shapes=[pltpu.VMEM((2, 2 * D_FF), F32), pltpu.VMEM((tm, D_FF), BF16), pltpu.VMEM((tm, d), F32)],
        compiler_params=_cp("arbitrary", "arbitrary"),
        name="ffn_prompt",
    )(*ys, x3, buf, *[q.arr for q in params])


def _ffn_sample_body(ya_ref, yb_ref, yc_ref, yd_ref, x_ref, buf_ref, wo_ref, g1_ref, b1_ref,
                     wup_ref, cw_ref, cb_ref, wdn_ref, g_ref, b_ref,
                     o_ref, st_ref, h_ref, *, nb, steps):
    t = nb * steps
    x = _outproj([ya_ref[...], yb_ref[...], yc_ref[...], yd_ref[...]], x_ref[...], wo_ref, g1_ref, b1_ref)
    xb = x.astype(BF16)
    for c in range(N_FF_CHUNKS):
        acts = []
        for half in range(2):
            lo = half * D_FF + c * FF_CHUNK
            sl = slice(lo, lo + FF_CHUNK)
            u = jnp.dot(xb, wup_ref[:, sl], preferred_element_type=F32)
            b0 = buf_ref[:, lo:lo + FF_CHUNK]
            b1 = buf_ref[:, 2 * D_FF + lo:2 * D_FF + lo + FF_CHUNK]
            back1 = jnp.concatenate([b1, u[0:t - nb]], axis=0)
            back2 = jnp.concatenate([b0, b1, u[0:t - 2 * nb]], axis=0)
            conv = cb_ref[:, sl] + cw_ref[0:1, sl] * back2 + cw_ref[1:2, sl] * back1 + cw_ref[2:3, sl] * u
            st_ref[:, lo:lo + FF_CHUNK] = u[t - 2 * nb:t - nb]
            st_ref[:, 2 * D_FF + lo:2 * D_FF + lo + FF_CHUNK] = u[t - nb:t]
            acts.append(conv)
        h_ref[:, c * FF_CHUNK:(c + 1) * FF_CHUNK] = (_silu(acts[0]) * acts[1]).astype(BF16)
    f = jnp.dot(h_ref[...], wdn_ref[...], preferred_element_type=F32)
    o_ref[...] = _layer_norm(ALPHA * x + f, g_ref[...], b_ref[...], 1e-5)


def _ffn_sample(ys, x, buf, p, nb, steps):
    t, d = x.shape
    params = [p[k] for k in _DENSE_PARAMS]
    full = lambda shape: pl.BlockSpec(shape, lambda i: (0,) * len(shape))
    return pl.pallas_call(
        functools.partial(_ffn_sample_body, nb=nb, steps=steps),
        grid=(1,),
        in_specs=[full((t, W_GROUP))] * 4 + [full((t, d)), _layer_spec(buf)] + [_layer_spec(q) for q in params],
        out_specs=[full((t, d)), _layer_spec(buf)],
        out_shape=[jax.ShapeDtypeStruct((t, d), F32), _stacked_like(buf)],
        input_output_aliases={5: 1},
        scratch_shapes=[pltpu.VMEM((t, D_FF), BF16)],
        compiler_params=_cp("arbitrary"),
        name="ffn_sample",
    )(*ys, x, buf.arr, *[q.arr for q in params])


def _exp_masks():
    t = _iota((CHUNK, 256), 0)
    s = _iota((CHUNK, 256), 1) & (CHUNK - 1)
    return t >= s, t > s, t == s


def _bd256():
    return _block_diag_mask(256, 256, 6, 6)


def _tril_blocks(n):
    r = _iota((n, n), 0)
    c = _iota((n, n), 1)
    return ((r >= c) & ((r >> 6) == (c >> 6))).astype(F32)


def _mask_bf16(mask):
    return mask.astype(F32).astype(BF16)


def _bdx(x, mask01):
    return _tile4(x.astype(BF16)) * mask01


def _put_block_diag(dst_ref, blocks, rows, cols):
    dst_ref[...] = jnp.zeros(dst_ref.shape, F32)
    for h in range(N_HEADS):
        dst_ref[h * rows:(h + 1) * rows, h * cols:(h + 1) * cols] = blocks[h]


def _get_block_diag(src_ref, out_ref, rows, cols):
    for h in range(N_HEADS):
        out_ref[0, h] = src_ref[h * rows:(h + 1) * rows, h * cols:(h + 1) * cols]


def _gla_prompt_body(z_ref, s0_ref, wa_ref, ba_ref, ng_ref, y_ref, so_ref, sbd_ref, *, n_chunks, n_blocks):
    j = pl.program_id(1)

    @pl.when(j == 0)
    def _():
        _put_block_diag(sbd_ref, s0_ref[0], DK_A, DV_A)

    lb = n_chunks * CHUNK
    causal, _, _ = _exp_masks()
    bd_state = _block_diag_mask(128, 256, 5, 6)
    bd_k = _mask_bf16(_block_diag_mask(256, 128, 6, 5))
    bd_v = _mask_bf16(_bd256())
    pavg2 = _twice(bd_v.astype(F32) * (1.0 / DV_A), 0)

    lg = _log_sigmoid(_mm(z_ref[0, :, 768:896], wa_ref[...]) + ba_ref[...]) * (1.0 / GLA_TAU)
    bc_all = _mm_cx(_twice(_tril_blocks(lb), 1), lg)

    cs = range(n_chunks)
    sl = [slice(c * CHUNK, (c + 1) * CHUNK) for c in cs]
    q = [z_ref[0, sl[c], 0:128] * DK_A ** -0.5 for c in cs]
    bc = [bc_all[sl[c]] for c in cs]
    bl = [bc[c][CHUNK - 1:CHUNK] for c in cs]
    rho = [bc[c][CHUNK // 2 - 1:CHUNK // 2] for c in cs]
    att = [_mm_nt(q[c] * jnp.exp(bc[c] - rho[c]), _bdx(z_ref[0, sl[c], 128:256] * jnp.exp(rho[c] - bc[c]), bd_k))
           for c in cs]
    incr = [jnp.where(bd_state, _mm_tn(z_ref[0, sl[c], 128:256] * jnp.exp(bl[c] - bc[c]), z_ref[0, sl[c], 256:512]),
                      0.0) for c in cs]
    o_intra = [_mm(jnp.where(causal, att[c], 0.0), _bdx(z_ref[0, sl[c], 256:512], bd_v)) for c in cs]
    q_dec = [q[c] * jnp.exp(bc[c]) for c in cs]
    decay = []
    for c in cs:
        decay_col = jnp.broadcast_to(jnp.exp(bl[c]), (128, 128)).T
        decay.append(jnp.concatenate([decay_col, decay_col], axis=1))
    starts = [sbd_ref[...]]
    for c in range(n_chunks):
        starts.append(decay[c] * starts[c] + incr[c])
    sbd_ref[...] = starts[n_chunks]
    o = jnp.concatenate([_mm(q_dec[c], starts[c]) + o_intra[c] for c in range(n_chunks)], axis=0)
    ms = _mm_xc(o * o, pavg2)
    y_ref[0] = (o * lax.rsqrt(ms + 1e-6) * ng_ref[...] * _silu(z_ref[0, :, 512:768])).astype(BF16)

    @pl.when(j == n_blocks - 1)
    def _():
        _get_block_diag(sbd_ref, so_ref, DK_A, DV_A)


def _gla_prompt(z3, s0, wa, ba, ng, lb):
    bsz, seq, _ = z3.shape
    n_blocks = seq // lb
    st = pl.BlockSpec((1, N_HEADS, DK_A, DV_A), lambda i, j: (i, 0, 0, 0))
    return pl.pallas_call(
        functools.partial(_gla_prompt_body, n_chunks=lb // CHUNK, n_blocks=n_blocks),
        grid=(bsz, n_blocks),
        in_specs=[pl.BlockSpec((1, lb, GLA_W), lambda i, j: (i, j, GLA_OFF // GLA_W)), st,
                  _layer_spec(wa), _layer_spec(ba), _layer_spec(ng)],
        out_specs=[pl.BlockSpec((1, lb, W_GROUP), lambda i, j: (i, j, 0)), st],
        out_shape=[jax.ShapeDtypeStruct((bsz, seq, W_GROUP), BF16),
                   jax.ShapeDtypeStruct((bsz, N_HEADS, DK_A, DV_A), F32)],
        scratch_shapes=[pltpu.VMEM((128, 256), F32)],
        compiler_params=_cp("arbitrary", "arbitrary"),
        name="gla_prompt",
    )(z3, s0, wa.arr, ba.arr, ng.arr)


def _pool_prompt_body(z_ref, buf_ref, w_ref, sc_ref, y_ref, bo_ref, hist_ref, *, lb, n_blocks, start_pos):
    j = pl.program_id(1)

    @pl.when(j == 0)
    def _():
        hist_ref[0:1, :] = jnp.zeros((1, W_GROUP), F32)
        hist_ref[1:16, :] = buf_ref[0]

    zp = z_ref[0]
    e = jnp.concatenate([hist_ref[...], zp], axis=0)
    s2 = e + pltpu.roll(e, 1, 0)
    s4 = s2 + pltpu.roll(s2, 2, 0)
    s8 = s4 + pltpu.roll(s4, 4, 0)
    s16 = s8 + pltpu.roll(s8, 8, 0)
    lane = _iota((lb, W_GROUP), 1)
    win = jnp.where(lane < 64, s2[16:], jnp.where(lane < 128, s4[16:], jnp.where(lane < 192, s8[16:], s16[16:])))
    by_lane = lambda ln, vals: jnp.where(ln < 64, vals[0], jnp.where(ln < 128, vals[1],
                                                                      jnp.where(ln < 192, vals[2], vals[3])))
    pos = start_pos + j * lb + _iota((16, W_GROUP), 0)
    cnt_head = jnp.minimum(pos + 1, by_lane(_iota((16, W_GROUP), 1), POOL_WINDOWS)).astype(F32)
    inv_width = by_lane(_iota((1, W_GROUP), 1), [1.0 / w for w in POOL_WINDOWS])
    mean = jnp.concatenate([win[0:16] / cnt_head, win[16:] * inv_width], axis=0)
    y_ref[0] = (_mm(mean - zp, w_ref[...]) * sc_ref[...]).astype(BF16)
    hist_ref[1:16, :] = e[lb + 1:lb + 16]

    @pl.when(j == n_blocks - 1)
    def _():
        bo_ref[0] = hist_ref[1:16, :]


def _pool_prompt(z3, buf, wbd, scale, lb, start_pos):
    bsz, seq, _ = z3.shape
    n_blocks = seq // lb
    st = pl.BlockSpec((1, POOL_BUF, W_GROUP), lambda i, j: (i, 0, 0))
    return pl.pallas_call(
        functools.partial(_pool_prompt_body, lb=lb, n_blocks=n_blocks, start_pos=start_pos),
        grid=(bsz, n_blocks),
        in_specs=[pl.BlockSpec((1, lb, W_GROUP), lambda i, j: (i, j, POOL_OFF // W_GROUP)), st,
                  _layer_spec(wbd), _layer_spec(scale)],
        out_specs=[pl.BlockSpec((1, lb, W_GROUP), lambda i, j: (i, j, 0)), st],
        out_shape=[jax.ShapeDtypeStruct((bsz, seq, W_GROUP), BF16),
                   jax.ShapeDtypeStruct((bsz, POOL_BUF, W_GROUP), F32)],
        scratch_shapes=[pltpu.VMEM((16, W_GROUP), F32)],
        compiler_params=_cp("arbitrary", "arbitrary"),
        name="pool_prompt",
    )(z3, buf, wbd.arr, scale.arr)


def _rwkv_prompt_body(z_ref, sh0_ref, s0_ref, mu_ref, w0_ref, w2_ref, a0_ref, a2_ref, g2_ref, kk_ref, ka_ref,
                      rk_ref, lg_ref, lb_ref, y_ref, so_ref, sho_ref,
                      nbd_ref, carry_ref, r_s, k_s, v_s, kk_s, b_s, lw_s, g_s, *, n_chunks, n_blocks, lb):
    j = pl.program_id(1)

    @pl.when(j == 0)
    def _():
        _put_block_diag(nbd_ref, s0_ref[0], N_C, N_C)
        carry_ref[...] = sh0_ref[0]

    bd = _bd256()
    bd_m = _mask_bf16(bd)
    ones2 = _twice(bd, 0)
    pavg2 = _twice(bd.astype(F32) * (1.0 / N_C), 0)

    f = z_ref[0]
    prev = jnp.concatenate([carry_ref[...], f[:-1]], axis=0)
    mixed = f + (prev - f) * mu_ref[...]
    carry_ref[...] = f[lb - 1:lb]
    r = mixed[:, 0:256]
    k = mixed[:, 256:512]
    v = mixed[:, 512:768]
    low = mixed[:, 768:896]
    lw_all = -_sigmoid(w0_ref[...] + _mm(jnp.tanh(low), w2_ref[...])) * math.exp(-0.5)
    lw_s[...] = lw_all
    a = _sigmoid(a0_ref[...] + _mm(low, a2_ref[...]))
    g_s[...] = _mm(_sigmoid(low), g2_ref[...])
    kk = k * kk_ref[...]
    kk = kk * lax.rsqrt(jnp.maximum(_mm_xc(kk * kk, ones2), 1e-24))
    r_s[...] = r
    v_s[...] = v
    kk_s[...] = kk
    k_s[...] = k * (1.0 + (a - 1.0) * ka_ref[...])
    b_s[...] = kk * a
    gc_all = _mm_cx(_twice(_tril_blocks(lb), 1), lw_all)

    lower, strict, eye_exp = _exp_masks()
    eye_f = eye_exp.astype(F32)

    cs = range(n_chunks)
    sl = [slice(c * CHUNK, (c + 1) * CHUNK) for c in cs]
    gc = [gc_all[s] for s in sl]
    kt = [kk_s[sl[c], :] * jnp.exp(gc[c] - lw_s[sl[c], :]) for c in cs]
    rt = [r_s[sl[c], :] * jnp.exp(gc[c]) for c in cs]
    a_kk, a_kb, a_rk, a_rb = [], [], [], []
    for c in cs:
        einv = jnp.exp(-gc[c])
        lhs = jnp.concatenate([kt[c], rt[c]], axis=0)
        rhs = jnp.concatenate([_bdx(k_s[sl[c], :] * einv, bd_m), _bdx(b_s[sl[c], :] * einv, bd_m)], axis=0)
        aa = _mm_nt(lhs, rhs)
        a_kk.append(jnp.where(strict, aa[0:CHUNK, 0:256], 0.0))
        a_kb.append(jnp.where(strict, aa[0:CHUNK, 256:512], 0.0))
        a_rk.append(jnp.where(lower, aa[CHUNK:, 0:256], 0.0))
        a_rb.append(jnp.where(lower, aa[CHUNK:, 256:512], 0.0))
    t = [eye_f - a_kb[c] for c in cs]
    p = [_mm(a_kb[c], _bdx(a_kb[c], bd_m)) for c in cs]
    for _ in range(4):
        m = [_mm(jnp.concatenate([p[c], t[c]], axis=0), _bdx(p[c], bd_m)) for c in cs]
        p = [m[c][0:CHUNK] for c in cs]
        t = [t[c] + m[c][CHUNK:] for c in cs]
    t = [t[c] + _mm(t[c], _bdx(p[c], bd_m)) for c in cs]
    av = [_mm(jnp.concatenate([a_kk[c], a_rk[c]], axis=0), _bdx(v_s[sl[c], :], bd_m)) for c in cs]
    tu = [_mm(t[c], jnp.concatenate([_bdx(av[c][0:CHUNK], bd_m), _bdx(kt[c], bd_m)], axis=1)) for c in cs]
    u0 = [tu[c][:, 0:256] for c in cs]
    tk = [tu[c][:, 256:512] for c in cs]
    ro = [_mm(a_rb[c], jnp.concatenate([_bdx(tk[c], bd_m), _bdx(u0[c], bd_m)], axis=1)) for c in cs]
    r_eff = [rt[c] - ro[c][:, 0:256] for c in cs]
    o0 = [av[c][CHUNK:] - ro[c][:, 256:512] for c in cs]
    gl = [gc[c][CHUNK - 1:CHUNK] for c in cs]
    dk = [jnp.exp(gl[c] - gc[c]) for c in cs]
    bdk = [b_s[sl[c], :] * dk[c] for c in cs]
    a_st = [jnp.where(bd, -_mm_tn(tk[c], bdk[c]), 0.0) for c in cs]
    b_st = [jnp.where(bd, _mm_tn(jnp.concatenate([v_s[sl[c], :], -u0[c]], axis=0),
                                 jnp.concatenate([k_s[sl[c], :] * dk[c], bdk[c]], axis=0)), 0.0)
            for c in cs]
    n = nbd_ref[...]
    o = []
    for c in cs:
        o.append(_mm_nt(r_eff[c], n) + o0[c])
        n = n * jnp.exp(gl[c]) + _mm(n, a_st[c]) + b_st[c]
    nbd_ref[...] = n
    o = jnp.concatenate(o, axis=0)
    mu = _mm_xc(o, pavg2)
    xc = o - mu
    var = _mm_xc(xc * xc, pavg2)
    on = xc * lax.rsqrt(var + RWKV_LN_EPS) * lg_ref[...] + lb_ref[...]
    bonus = _mm_xc(r_s[...] * k_s[...] * rk_ref[...], ones2) * v_s[...]
    y_ref[0] = ((on + bonus) * g_s[...]).astype(BF16)

    @pl.when(j == n_blocks - 1)
    def _():
        _get_block_diag(nbd_ref, so_ref, N_C, N_C)
        sho_ref[0] = carry_ref[...]


_RWKV_PARAMS = ("mu", "w0", "w2", "a0", "a2", "g2", "k_k", "k_a", "r_k", "ln_g", "ln_b")


def _rwkv_prompt(z3, sh0, s0, p, lb):
    bsz, seq, _ = z3.shape
    n_blocks = seq // lb
    st = pl.BlockSpec((1, N_HEADS, N_C, N_C), lambda i, j: (i, 0, 0, 0))
    sh = pl.BlockSpec((1, 1, RWKV_W), lambda i, j: (i, 0, 0))
    blk = pltpu.VMEM((lb, 256), F32)
    params = [p[k] for k in _RWKV_PARAMS]
    return pl.pallas_call(
        functools.partial(_rwkv_prompt_body, n_chunks=lb // CHUNK, n_blocks=n_blocks, lb=lb),
        grid=(bsz, n_blocks),
        in_specs=[pl.BlockSpec((1, lb, RWKV_W), lambda i, j: (i, j, RWKV_OFF // RWKV_W)), sh, st]
        + [_layer_spec(q) for q in params],
        out_specs=[pl.BlockSpec((1, lb, W_GROUP), lambda i, j: (i, j, 0)), st, sh],
        out_shape=[jax.ShapeDtypeStruct((bsz, seq, W_GROUP), BF16),
                   jax.ShapeDtypeStruct((bsz, N_HEADS, N_C, N_C), F32),
                   jax.ShapeDtypeStruct((bsz, 1, RWKV_W), F32)],
        scratch_shapes=[pltpu.VMEM((256, 256), F32), pltpu.VMEM((1, RWKV_W), F32),
                        blk, blk, blk, blk, blk, blk, blk],
        compiler_params=_cp("arbitrary", "arbitrary"),
        name="rwkv_prompt",
    )(z3, sh0, s0, *[q.arr for q in params])


def _mlstm_prompt_body(z_ref, zif_ref, cb0_ref, c0_ref, n0_ref, m0_ref, cw_ref, cbias_ref, bif_ref, ng_ref,
                       y_ref, co_ref, no_ref, mo_ref, cbo_ref,
                       cbd_ref, n_ref, m_ref, carry_ref, qk_s, *, n_chunks, n_blocks, lb):
    j = pl.program_id(1)

    @pl.when(j == 0)
    def _():
        _put_block_diag(cbd_ref, c0_ref[0], D_D, D_D)
        n_ref[...] = n0_ref[0]
        m_ref[...] = m0_ref[0]
        carry_ref[...] = cb0_ref[0]

    qk = z_ref[0, :, 0:512]
    conv = cbias_ref[...] + cw_ref[CONV_D - 1:CONV_D, :] * qk
    head = cbias_ref[...] + cw_ref[CONV_D - 1:CONV_D, :] * qk[0:8]
    row = _iota((8, 512), 0)
    for back in range(1, CONV_D):
        w = cw_ref[CONV_D - 1 - back:CONV_D - back, :]
        rolled = pltpu.roll(qk, back, 0)
        conv = conv + w * rolled
        fixed = rolled[0:8]
        for r in range(back):
            fixed = jnp.where(row == r, carry_ref[CONV_D - 1 - back + r:CONV_D - back + r, :], fixed)
        head = head + w * fixed
    qk_s[...] = _silu(jnp.concatenate([head, conv[8:]], axis=0))
    carry_ref[...] = qk[lb - (CONV_D - 1):lb]

    bd = _bd256()
    bd_m = _mask_bf16(bd)
    ones2 = _twice(bd, 0)
    pavg2 = _twice(bd.astype(F32) * (1.0 / D_D), 0)
    causal, _, eye_exp = _exp_masks()
    er = _iota((128, 512), 0)
    ec = _iota((128, 512), 1)
    expand2 = _twice(((ec < 256) & (er == (ec >> 6))) | ((ec >= 256) & (er == 4 + ((ec - 256) >> 6))), 0)
    is_f = (_iota((lb, 128), 1) >= 4) & (_iota((lb, 128), 1) < 8)
    lane = _iota((CHUNK, 256), 1)

    gates = zif_ref[0] + bif_ref[...]
    gates = jnp.where(is_f, _log_sigmoid(gates), gates)
    gates = jnp.where(is_f, _mm_cx(_twice(_tril_blocks(lb), 1), gates), gates)
    ge_all = _mm_xc(gates, expand2)

    cs = range(n_chunks)
    sl = [slice(c * CHUNK, (c + 1) * CHUNK) for c in cs]
    q = [qk_s[sl[c], 0:256] * D_D ** -0.5 for c in cs]
    k = [qk_s[sl[c], 256:512] for c in cs]
    iexp = [ge_all[sl[c], 0:256] for c in cs]
    fexp = [ge_all[sl[c], 256:512] for c in cs]
    qk_att = [_mm_nt(q[c], _bdx(k[c], bd_m)) for c in cs]
    dlog, m_t, m_prev = [], [], [m_ref[...]]
    for c in cs:
        d_row = jnp.sum(jnp.where(eye_exp, iexp[c] - fexp[c], 0.0), axis=0, keepdims=True)
        dl = jnp.where(causal, fexp[c] + d_row, -jnp.inf)
        mx = [jnp.max(dl[:, h * 64:(h + 1) * 64], axis=1, keepdims=True) for h in range(N_HEADS)]
        mx = jnp.where(lane < 64, mx[0], jnp.where(lane < 128, mx[1], jnp.where(lane < 192, mx[2], mx[3])))
        dlog.append(dl)
        m_t.append(jnp.maximum(fexp[c] + m_prev[c], mx))
        m_prev.append(m_t[c][CHUNK - 1:CHUNK])
    w0 = [jnp.exp(fexp[c] + m_prev[c] - m_t[c]) for c in cs]
    s = [jnp.exp(dlog[c] - m_t[c]) * qk_att[c] for c in cs]
    f_last = [fexp[c][CHUNK - 1:CHUNK] for c in cs]
    kw = [k[c] * jnp.exp(f_last[c] - fexp[c] + iexp[c] - m_prev[c + 1]) for c in cs]
    w0f = [jnp.exp(f_last[c] + m_prev[c] - m_prev[c + 1]) for c in cs]
    incr = [jnp.where(bd, _mm_tn(kw[c], z_ref[0, sl[c], 512:768]), 0.0) for c in cs]
    sv = [_mm(s[c], _bdx(z_ref[0, sl[c], 512:768], bd_m)) for c in cs]
    c_start, n_start = [cbd_ref[...]], [n_ref[...]]
    for c in cs:
        c_start.append(w0f[c] * c_start[c] + incr[c])
        n_start.append(w0f[c] * n_start[c] + jnp.sum(kw[c], axis=0, keepdims=True))
    cbd_ref[...] = c_start[n_chunks]
    n_ref[...] = n_start[n_chunks]
    m_ref[...] = m_prev[n_chunks]
    num = [w0[c] * _mm(q[c], c_start[c]) + sv[c] for c in cs]
    den = [_mm_xc(w0[c] * (q[c] * n_start[c]) + s[c], ones2) for c in cs]
    hh = jnp.concatenate([num[c] / jnp.maximum(jnp.abs(den[c]), jnp.exp(-m_t[c])) for c in cs], axis=0)
    ms = _mm_xc(hh * hh, pavg2)
    y_ref[0] = (_sigmoid(z_ref[0, :, 768:1024]) * (hh * lax.rsqrt(ms + 1e-6) * ng_ref[...])).astype(BF16)

    @pl.when(j == n_blocks - 1)
    def _():
        _get_block_diag(cbd_ref, co_ref, D_D, D_D)
        no_ref[0] = n_ref[...]
        mo_ref[0] = m_ref[...]
        cbo_ref[0] = carry_ref[...]


def _mlstm_prompt(z3, cb0, c0, n0, m0, cw, cbias, bif, ng, lb):
    bsz, seq, _ = z3.shape
    n_blocks = seq // lb
    cst = pl.BlockSpec((1, N_HEADS, D_D, D_D), lambda i, j: (i, 0, 0, 0))
    row = pl.BlockSpec((1, 1, 256), lambda i, j: (i, 0, 0))
    cbs = pl.BlockSpec((1, CONV_D - 1, 512), lambda i, j: (i, 0, 0))
    return pl.pallas_call(
        functools.partial(_mlstm_prompt_body, n_chunks=lb // CHUNK, n_blocks=n_blocks, lb=lb),
        grid=(bsz, n_blocks),
        in_specs=[pl.BlockSpec((1, lb, ML_W), lambda i, j: (i, j, ML_OFF // ML_W)),
                  pl.BlockSpec((1, lb, MLIF_W), lambda i, j: (i, j, MLIF_OFF // MLIF_W)),
                  cbs, cst, row, row, _layer_spec(cw), _layer_spec(cbias), _layer_spec(bif), _layer_spec(ng)],
        out_specs=[pl.BlockSpec((1, lb, W_GROUP), lambda i, j: (i, j, 0)), cst, row, row, cbs],
        out_shape=[jax.ShapeDtypeStruct((bsz, seq, W_GROUP), BF16),
                   jax.ShapeDtypeStruct((bsz, N_HEADS, D_D, D_D), F32),
                   jax.ShapeDtypeStruct((bsz, 1, 256), F32),
                   jax.ShapeDtypeStruct((bsz, 1, 256), F32),
                   jax.ShapeDtypeStruct((bsz, CONV_D - 1, 512), F32)],
        scratch_shapes=[pltpu.VMEM((256, 256), F32), pltpu.VMEM((1, 256), F32), pltpu.VMEM((1, 256), F32),
                        pltpu.VMEM((CONV_D - 1, 512), F32), pltpu.VMEM((lb, 512), F32)],
        compiler_params=_cp("arbitrary", "arbitrary"),
        name="mlstm_prompt",
    )(z3, z3, cb0, c0, n0, m0, cw.arr, cbias.arr, bif.arr, ng.arr)


T_COLS = 1024


def _to_rows(src_ref, dst_ref, n):
    for i in range(n // T_COLS):
        dst_ref[i * T_COLS:(i + 1) * T_COLS, :] = src_ref[:, i * T_COLS:(i + 1) * T_COLS].T


def _to_cols(src_ref, dst_ref, n):
    for i in range(n // T_COLS):
        dst_ref[:, i * T_COLS:(i + 1) * T_COLS] = src_ref[i * T_COLS:(i + 1) * T_COLS, :].T


def _per_head(x, fn):
    return jnp.concatenate([fn(x[h * 64:(h + 1) * 64], h) for h in range(N_HEADS)], axis=0)


def _expand_heads(x, nb):
    return jnp.concatenate([jnp.broadcast_to(x[h:h + 1], (64, nb)) for h in range(N_HEADS)], axis=0)


def _gla_sample_body(z_ref, s_ref, wa_ref, ba_ref, ng_ref, y_ref, so_ref,
                     st_s, dec_s, k_s, q_s, v_s, o_s, *, steps, nb):
    n_state = N_HEADS * DK_A * DV_A
    _to_rows(s_ref, st_s, n_state)
    for t in range(steps):
        zt = z_ref[t].T
        q_s[t] = zt[0:128] * DK_A ** -0.5
        k_s[t] = zt[128:256]
        v_s[t] = zt[256:512]
        lg = _log_sigmoid(_mm_hi(wa_ref[...], zt[768:896]) + ba_ref[...]) * (1.0 / GLA_TAU)
        dec_s[t] = jnp.exp(lg)
    o_s[...] = jnp.zeros(o_s.shape, F32)

    def per_key(hk, carry):
        r0 = pl.multiple_of(hk * DV_A, DV_A)
        v0 = pl.multiple_of((hk >> 5) * DV_A, DV_A)
        s = st_s[pl.ds(r0, DV_A), :]
        for t in range(steps):
            s = dec_s[t, pl.ds(hk, 1), :] * s + k_s[t, pl.ds(hk, 1), :] * v_s[t, pl.ds(v0, DV_A), :]
            o_s[t, pl.ds(v0, DV_A), :] += q_s[t, pl.ds(hk, 1), :] * s
        st_s[pl.ds(r0, DV_A), :] = s
        return carry

    lax.fori_loop(0, N_HEADS * DK_A, per_key, 0, unroll=2)
    for t in range(steps):
        g = z_ref[t, :, 512:768].T
        on = _per_head(o_s[t], lambda oh, h: oh * lax.rsqrt(jnp.mean(oh * oh, axis=0, keepdims=True) + 1e-6))
        y_ref[t] = (on * ng_ref[...] * _silu(g)).T.astype(BF16)
    _to_cols(st_s, so_ref, n_state)


def _gla_sample(zs, s, wa_t, ba_col, ng_col):
    steps, nb, _ = zs.shape
    n_state = N_HEADS * DK_A * DV_A
    full = lambda shape: pl.BlockSpec(shape, lambda i: (0,) * len(shape))
    return pl.pallas_call(
        functools.partial(_gla_sample_body, steps=steps, nb=nb),
        grid=(1,),
        in_specs=[pl.BlockSpec((steps, nb, GLA_W), lambda i: (0, 0, GLA_OFF // GLA_W)), _layer_spec(s),
                  _layer_spec(wa_t), _layer_spec(ba_col), _layer_spec(ng_col)],
        out_specs=[full((steps, nb, W_GROUP)), _layer_spec(s)],
        out_shape=[jax.ShapeDtypeStruct((steps, nb, W_GROUP), BF16), _stacked_like(s)],
        input_output_aliases={1: 1},
        scratch_shapes=[pltpu.VMEM((n_state, nb), F32), pltpu.VMEM((steps, 128, nb), F32),
                        pltpu.VMEM((steps, 128, nb), F32), pltpu.VMEM((steps, 128, nb), F32),
                        pltpu.VMEM((steps, 256, nb), F32), pltpu.VMEM((steps, 256, nb), F32)],
        compiler_params=_cp("arbitrary"),
        name="gla_sample",
    )(zs, s.arr, wa_t.arr, ba_col.arr, ng_col.arr)


def _pool_sample_body(z_ref, buf_ref, w_ref, sc_ref, y_ref, bo_ref, *, steps, nb, start_pos):
    ext = [buf_ref[:, j * W_GROUP:(j + 1) * W_GROUP] for j in range(POOL_BUF)] + [z_ref[t] for t in range(steps)]
    lane = _iota((nb, W_GROUP), 1)

    def by_group(vals):
        return jnp.where(lane < 64, vals[0], jnp.where(lane < 128, vals[1], jnp.where(lane < 192, vals[2], vals[3])))

    for t in range(steps):
        r = POOL_BUF + t
        acc = ext[r]
        sums = {}
        for back in range(1, max(POOL_WINDOWS)):
            acc = acc + ext[r - back]
            sums[back + 1] = acc
        win = by_group([sums[w] for w in POOL_WINDOWS])
        cnt = by_group([float(min(start_pos + t + 1, w)) for w in POOL_WINDOWS])
        y_ref[t] = (_mm(win / cnt - ext[r], w_ref[...]) * sc_ref[...]).astype(BF16)
    for j in range(POOL_BUF):
        bo_ref[:, j * W_GROUP:(j + 1) * W_GROUP] = ext[steps + j]


def _pool_sample(zs, buf, wbd, scale, start_pos):
    steps, nb, _ = zs.shape
    full = lambda shape: pl.BlockSpec(shape, lambda i: (0,) * len(shape))
    return pl.pallas_call(
        functools.partial(_pool_sample_body, steps=steps, nb=nb, start_pos=start_pos),
        grid=(1,),
        in_specs=[pl.BlockSpec((steps, nb, W_GROUP), lambda i: (0, 0, POOL_OFF // W_GROUP)),
                  _layer_spec(buf), _layer_spec(wbd), _layer_spec(scale)],
        out_specs=[full((steps, nb, W_GROUP)), _layer_spec(buf)],
        out_shape=[jax.ShapeDtypeStruct((steps, nb, W_GROUP), BF16), _stacked_like(buf)],
        input_output_aliases={1: 1},
        compiler_params=_cp("arbitrary"),
        name="pool_sample",
    )(zs, buf.arr, wbd.arr, scale.arr)


def _rwkv_sample_body(z_ref, sh_ref, s_ref, mu_ref, w0_ref, w2_ref, a0_ref, a2_ref, g2_ref, kk_ref, ka_ref,
                      rk_ref, lg_ref, lb_ref, y_ref, so_ref, sho_ref,
                      st_s, r_s, k_s, v_s, kk_s, b_s, w_s, g_s, o_s, *, steps, nb):
    n_state = N_HEADS * N_C * N_C
    _to_rows(s_ref, st_s, n_state)
    prev = sh_ref[...].T
    for t in range(steps):
        f = z_ref[t].T
        mixed = f + (prev - f) * mu_ref[...]
        prev = f
        k = mixed[256:512]
        low = mixed[768:896]
        w_s[t] = jnp.exp(-_sigmoid(w0_ref[...] + _mm_hi(w2_ref[...], jnp.tanh(low))) * math.exp(-0.5))
        a = _sigmoid(a0_ref[...] + _mm_hi(a2_ref[...], low))
        g_s[t] = _mm_hi(g2_ref[...], _sigmoid(low))
        kk = _per_head(k * kk_ref[...], lambda kh, h: kh * lax.rsqrt(
            jnp.maximum(jnp.sum(kh * kh, axis=0, keepdims=True), 1e-24)))
        r_s[t] = mixed[0:256]
        v_s[t] = mixed[512:768]
        kk_s[t] = kk
        k_s[t] = k * (1.0 + (a - 1.0) * ka_ref[...])
        b_s[t] = kk * a

    def per_value(hv, carry):
        r0 = pl.multiple_of(hv * N_C, N_C)
        hs = pl.ds(pl.multiple_of((hv >> 6) * N_C, N_C), N_C)
        s = st_s[pl.ds(r0, N_C), :]
        for t in range(steps):
            sa = -jnp.sum(s * kk_s[t, hs, :], axis=0, keepdims=True)
            s = s * w_s[t, hs, :] + sa * b_s[t, hs, :] + v_s[t, pl.ds(hv, 1), :] * k_s[t, hs, :]
            o_s[t, pl.ds(hv, 1), :] = jnp.sum(s * r_s[t, hs, :], axis=0, keepdims=True)
        st_s[pl.ds(r0, N_C), :] = s
        return carry

    lax.fori_loop(0, N_HEADS * N_C, per_value, 0, unroll=4)
    for t in range(steps):
        def norm(oh, h):
            mu = jnp.mean(oh, axis=0, keepdims=True)
            xc = oh - mu
            return xc * lax.rsqrt(jnp.mean(xc * xc, axis=0, keepdims=True) + RWKV_LN_EPS)
        on = _per_head(o_s[t], norm) * lg_ref[...] + lb_ref[...]
        rk = r_s[t] * k_s[t] * rk_ref[...]
        v = v_s[t]
        bonus = _per_head(rk, lambda x, h: jnp.sum(x, axis=0, keepdims=True) * v[h * 64:(h + 1) * 64])
        y_ref[t] = ((on + bonus) * g_s[t]).T.astype(BF16)
    _to_cols(st_s, so_ref, n_state)
    sho_ref[...] = z_ref[steps - 1]


def _rwkv_sample(zs, sh, s, p):
    steps, nb, _ = zs.shape
    n_state = N_HEADS * N_C * N_C
    full = lambda shape: pl.BlockSpec(shape, lambda i: (0,) * len(shape))
    blk = pltpu.VMEM((steps, 256, nb), F32)
    params = [p[k + "_c"] for k in _RWKV_PARAMS]
    return pl.pallas_call(
        functools.partial(_rwkv_sample_body, steps=steps, nb=nb),
        grid=(1,),
        in_specs=[pl.BlockSpec((steps, nb, RWKV_W), lambda i: (0, 0, RWKV_OFF // RWKV_W)), _layer_spec(sh),
                  _layer_spec(s)] + [_layer_spec(q) for q in params],
        out_specs=[full((steps, nb, W_GROUP)), _layer_spec(s), _layer_spec(sh)],
        out_shape=[jax.ShapeDtypeStruct((steps, nb, W_GROUP), BF16), _stacked_like(s), _stacked_like(sh)],
        input_output_aliases={2: 1, 1: 2},
        scratch_shapes=[pltpu.VMEM((n_state, nb), F32), blk, blk, blk, blk, blk, blk, blk, blk],
        compiler_params=_cp("arbitrary"),
        name="rwkv_sample",
    )(zs, sh.arr, s.arr, *[q.arr for q in params])


def _mlstm_sample_body(z_ref, zif_ref, cb_ref, c_ref, n_ref, m_ref, cw_ref, cbias_ref, bif_ref, ng_ref,
                       y_ref, co_ref, no_ref, mo_ref, cbo_ref,
                       ct_s, q_s, k_s, v_s, wf_s, den_s, em_s, o_s, *, steps, nb):
    n_state = N_HEADS * D_D * D_D
    _to_rows(c_ref, ct_s, n_state)
    n = n_ref[...].T
    m = m_ref[...].T[0:N_HEADS]
    hist = [cb_ref[:, j * 512:(j + 1) * 512].T for j in range(CONV_D - 1)]
    for t in range(steps):
        hist = hist + [z_ref[t, :, 0:512].T]
        conv = cbias_ref[...]
        for jj in range(CONV_D):
            conv = conv + cw_ref[:, jj:jj + 1] * hist[jj]
        hist = hist[1:]
        act = _silu(conv)
        q = act[0:256] * D_D ** -0.5
        k = act[256:512]
        gates = zif_ref[t].T + bif_ref[...]
        ig = gates[0:N_HEADS]
        lf = _log_sigmoid(gates[N_HEADS:2 * N_HEADS])
        m_new = jnp.maximum(lf + m, ig)
        wf = _expand_heads(jnp.exp(lf + m - m_new), nb)
        kw = _expand_heads(jnp.exp(ig - m_new), nb) * k
        m = m_new
        n = wf * n + kw
        den = _per_head(q * n, lambda x, h: jnp.broadcast_to(jnp.sum(x, axis=0, keepdims=True), (64, nb)))
        q_s[t] = q
        k_s[t] = kw
        v_s[t] = z_ref[t, :, 512:768].T
        wf_s[t] = wf
        den_s[t] = den
        em_s[t] = _expand_heads(jnp.exp(-m_new), nb)
    o_s[...] = jnp.zeros(o_s.shape, F32)

    def per_key(hd, carry):
        r0 = pl.multiple_of(hd * D_D, D_D)
        e0 = pl.multiple_of((hd >> 6) * D_D, D_D)
        c = ct_s[pl.ds(r0, D_D), :]
        for t in range(steps):
            c = wf_s[t, pl.ds(hd, 1), :] * c + k_s[t, pl.ds(hd, 1), :] * v_s[t, pl.ds(e0, D_D), :]
            o_s[t, pl.ds(e0, D_D), :] += q_s[t, pl.ds(hd, 1), :] * c
        ct_s[pl.ds(r0, D_D), :] = c
        return carry

    lax.fori_loop(0, N_HEADS * D_D, per_key, 0, unroll=2)
    for t in range(steps):
        hh = o_s[t] / jnp.maximum(jnp.abs(den_s[t]), em_s[t])
        hn = _per_head(hh, lambda x, h: x * lax.rsqrt(jnp.mean(x * x, axis=0, keepdims=True) + 1e-6))
        y_ref[t] = (_sigmoid(z_ref[t, :, 768:1024].T) * (hn * ng_ref[...])).T.astype(BF16)
    _to_cols(ct_s, co_ref, n_state)
    no_ref[...] = n.T
    mo_ref[...] = jnp.concatenate([m, jnp.zeros((8 - N_HEADS, nb), F32)], axis=0)
    for j in range(CONV_D - 1):
        cbo_ref[:, j * 512:(j + 1) * 512] = hist[j].T


def _mlstm_sample(zs, cb, c, n, m_pad, cw_t, cbias_col, bif_col, ng_col):
    steps, nb, _ = zs.shape
    n_state = N_HEADS * D_D * D_D
    full = lambda shape: pl.BlockSpec(shape, lambda i: (0,) * len(shape))
    blk = pltpu.VMEM((steps, 256, nb), F32)
    return pl.pallas_call(
        functools.partial(_mlstm_sample_body, steps=steps, nb=nb),
        grid=(1,),
        in_specs=[pl.BlockSpec((steps, nb, ML_W), lambda i: (0, 0, ML_OFF // ML_W)),
                  pl.BlockSpec((steps, nb, MLIF_W), lambda i: (0, 0, MLIF_OFF // MLIF_W)),
                  _layer_spec(cb), _layer_spec(c), _layer_spec(n), _layer_spec(m_pad),
                  _layer_spec(cw_t), _layer_spec(cbias_col), _layer_spec(bif_col), _layer_spec(ng_col)],
        out_specs=[full((steps, nb, W_GROUP)), _layer_spec(c), _layer_spec(n), full((8, nb)), _layer_spec(cb)],
        out_shape=[jax.ShapeDtypeStruct((steps, nb, W_GROUP), BF16), _stacked_like(c), _stacked_like(n),
                   jax.ShapeDtypeStruct((8, nb), F32), _stacked_like(cb)],
        input_output_aliases={3: 1, 4: 2, 2: 4},
        scratch_shapes=[pltpu.VMEM((n_state, nb), F32), blk, blk, blk, blk, blk, blk, blk],
        compiler_params=_cp("arbitrary"),
        name="mlstm_sample",
    )(zs, zs, cb.arr, c.arr, n.arr, m_pad.arr, cw_t.arr, cbias_col.arr, bif_col.arr, ng_col.arr)


def _pad_cols(x, n):
    return jnp.pad(x, [(0, 0)] * (x.ndim - 1) + [(0, n - x.shape[-1])])


def _rows_at(x, row0, n_rows):
    return jnp.pad(x, ((0, 0), (row0, n_rows - row0 - x.shape[1]), (0, 0)))


def _stacked_params(w):
    gla, pool, rwkv, ml = jnp.split(w["w_in"], [784, 784 + 256, 784 + 256 + 832], axis=2)
    w_in_p = jnp.concatenate([_pad_cols(gla, GLA_W), _pad_cols(rwkv, RWKV_W), pool, ml[:, :, :1024],
                              _pad_cols(ml[:, :, 1024:], MLIF_W)], axis=2).astype(BF16)
    col = lambda v: v.reshape(DEPTH, -1, 1)
    row = lambda v: v.reshape(DEPTH, 1, -1)
    tr = lambda m: m.transpose(0, 2, 1)
    wa = _rows_at(w["gla_w_a2"], 0, 128)
    w2 = _rows_at(w["rwkv_w2"], 0, 128)
    a2 = _rows_at(w["rwkv_a2"], R_W, 128)
    g2 = _rows_at(w["rwkv_g2"], R_W + R_AA, 128)
    mu = _pad_cols(w["rwkv_mu"], RWKV_W)
    bif = _pad_cols(jnp.concatenate([w["mlstm_b_i"], w["mlstm_b_f"]], axis=1), 128)
    wbd = jnp.zeros((DEPTH, 256, 256), F32)
    for gi in range(4):
        wbd = wbd.at[:, gi * 64:(gi + 1) * 64, gi * 64:(gi + 1) * 64].set(w["pool_w"][:, gi])
    vecs = dict(w0=w["rwkv_w0"], a0=w["rwkv_a0"], k_k=w["rwkv_k_k"], k_a=w["rwkv_k_a"], r_k=w["rwkv_r_k"],
                ln_g=w["rwkv_ln_g"], ln_b=w["rwkv_ln_b"], mu=mu)
    rw = {k: row(v) for k, v in vecs.items()}
    rw.update({k + "_c": col(v) for k, v in vecs.items()})
    rw.update(w2=w2, a2=a2, g2=g2, w2_c=tr(w2), a2_c=tr(a2), g2_c=tr(g2))
    return dict(
        w_in=w_in_p, w_out=w["w_out"].astype(BF16), ln1_g=row(w["ln1_g"]), ln1_b=row(w["ln1_b"]),
        w_up=w["ffn_w_up"].astype(BF16), ffn_cw=w["ffn_conv_w"], ffn_cb=row(w["ffn_conv_b"]),
        w_down=w["ffn_w_down"].astype(BF16), ln2_g=row(w["ln2_g"]), ln2_b=row(w["ln2_b"]),
        gla_wa=wa, gla_wa_t=tr(wa), gla_ba=row(w["gla_b_a"]), gla_ba_c=col(w["gla_b_a"]),
        gla_ng=row(w["gla_norm_g"]), gla_ng_c=col(w["gla_norm_g"]),
        pool_w=wbd, pool_scale=row(w["pool_scale"]), rwkv=rw,
        ml_cw=w["mlstm_conv_w"], ml_cw_t=tr(w["mlstm_conv_w"]), ml_cb=row(w["mlstm_conv_b"]),
        ml_cb_c=col(w["mlstm_conv_b"]), ml_bif=row(bif), ml_bif_c=col(bif),
        ml_ng=row(w["mlstm_norm_g"]), ml_ng_c=col(w["mlstm_norm_g"]))


def _layer_view(stacked, l):
    return {k: (_layer_view(v, l) if isinstance(v, dict) else _Layer(v, l)) for k, v in stacked.items()}


RWKV_BLOCK = 512
GLA_BLOCK = 512
MLSTM_BLOCK = 512
POOL_BLOCK = 1024
DENSE_TILE = 1024
FFN_TILE = 512


def _prompt_layer(x, p):
    bsz, seq, d = x.shape
    z = _linear(x.reshape(bsz * seq, d), p["w_in"], DENSE_TILE).reshape(bsz, seq, ZC)
    zeros = lambda *s: jnp.zeros((bsz,) + s, F32)
    y_a, s_gla = _gla_prompt(z, zeros(N_HEADS, DK_A, DV_A), p["gla_wa"], p["gla_ba"], p["gla_ng"], GLA_BLOCK)
    y_b, s_pool = _pool_prompt(z, zeros(POOL_BUF, W_GROUP), p["pool_w"], p["pool_scale"], POOL_BLOCK, 0)
    y_c, s_rwkv, s_shift = _rwkv_prompt(z, zeros(1, RWKV_W), zeros(N_HEADS, N_C, N_C), p["rwkv"], RWKV_BLOCK)
    y_d, s_c, s_n, s_m, s_conv = _mlstm_prompt(z, zeros(CONV_D - 1, 512), zeros(N_HEADS, D_D, D_D), zeros(1, 256),
                                               zeros(1, 256), p["ml_cw"], p["ml_cb"], p["ml_bif"], p["ml_ng"],
                                               MLSTM_BLOCK)
    x2, s_ffn = _ffn_prompt([y_a, y_b, y_c, y_d], x, zeros(FFN_CONV - 1, 2 * D_FF), p, FFN_TILE)
    states = (s_gla, s_pool, s_rwkv, s_shift[:, :, :RWKV_COLS], s_c, s_n.reshape(bsz, N_HEADS, D_D),
              s_m[:, 0, ::D_D], s_conv, s_ffn)
    return x2, states


def _sample_layer(x, st, p, steps, nb):
    s_gla, s_pool, s_rwkv, s_shift, s_c, s_n, s_m, s_conv, s_ffn = st
    z = _linear(x, p["w_in"], steps * nb).reshape(steps, nb, ZC)
    y_a, n_gla = _gla_sample(z, s_gla, p["gla_wa_t"], p["gla_ba_c"], p["gla_ng_c"])
    y_b, n_pool = _pool_sample(z, s_pool, p["pool_w"], p["pool_scale"], PAST_LEN)
    y_c, n_rwkv, n_shift = _rwkv_sample(z, s_shift, s_rwkv, p["rwkv"])
    y_d, n_c, n_n, n_m, n_conv = _mlstm_sample(z, s_conv, s_c, s_n, s_m, p["ml_cw_t"], p["ml_cb_c"], p["ml_bif_c"],
                                               p["ml_ng_c"])
    flat = lambda y: y.reshape(steps * nb, W_GROUP)
    x2, n_ffn = _ffn_sample([flat(y_a), flat(y_b), flat(y_c), flat(y_d)], x, s_ffn, p, nb, steps)
    return x2, (n_gla, n_pool, n_rwkv, n_shift, n_c, n_n, n_m, n_conv, n_ffn)


def kernel(x_prompt, x_sample, state_gla, state_pool, state_rwkv, state_rwkv_shift, state_mlstm_c, state_mlstm_n, state_mlstm_m, state_mlstm_conv, state_ffn_conv, w_in, gla_w_a2, gla_b_a, gla_norm_g, pool_w, pool_scale, rwkv_mu, rwkv_w0, rwkv_w2, rwkv_a0, rwkv_a2, rwkv_g2, rwkv_k_k, rwkv_k_a, rwkv_r_k, rwkv_ln_g, rwkv_ln_b, mlstm_conv_w, mlstm_conv_b, mlstm_b_i, mlstm_b_f, mlstm_norm_g, w_out, ln1_g, ln1_b, ffn_w_up, ffn_conv_w, ffn_conv_b, ffn_w_down, ln2_g, ln2_b):
    w = dict(w_in=w_in, gla_w_a2=gla_w_a2, gla_b_a=gla_b_a, gla_norm_g=gla_norm_g, pool_w=pool_w,
             pool_scale=pool_scale, rwkv_mu=rwkv_mu, rwkv_w0=rwkv_w0, rwkv_w2=rwkv_w2, rwkv_a0=rwkv_a0,
             rwkv_a2=rwkv_a2, rwkv_g2=rwkv_g2, rwkv_k_k=rwkv_k_k, rwkv_k_a=rwkv_k_a, rwkv_r_k=rwkv_r_k,
             rwkv_ln_g=rwkv_ln_g, rwkv_ln_b=rwkv_ln_b, mlstm_conv_w=mlstm_conv_w, mlstm_conv_b=mlstm_conv_b,
             mlstm_b_i=mlstm_b_i, mlstm_b_f=mlstm_b_f, mlstm_norm_g=mlstm_norm_g, w_out=w_out, ln1_g=ln1_g,
             ln1_b=ln1_b, ffn_w_up=ffn_w_up, ffn_conv_w=ffn_conv_w, ffn_conv_b=ffn_conv_b, ffn_w_down=ffn_w_down,
             ln2_g=ln2_g, ln2_b=ln2_b)
    sample_states = (state_gla, state_pool, state_rwkv, state_rwkv_shift, state_mlstm_c, state_mlstm_n,
                     state_mlstm_m, state_mlstm_conv, state_ffn_conv)
    nb, steps, d = x_sample.shape
    flat_in = [s.reshape(DEPTH, nb, -1) for s in sample_states]
    flat_in[3] = _pad_cols(flat_in[3], RWKV_W)
    flat_in[6] = _pad_cols(flat_in[6], 128)
    stacked = _stacked_params(w)
    yp = x_prompt
    ys = x_sample.transpose(1, 0, 2).reshape(steps * nb, d)
    acc_p = [[] for _ in sample_states]
    m_new = []
    flat = list(flat_in)
    for l in range(DEPTH):
        p = _layer_view(stacked, l)
        yp, st_p = _prompt_layer(yp, p)
        ys, st_s = _sample_layer(ys, tuple(_Layer(s, l) for s in flat), p, steps, nb)
        for i in range(len(sample_states)):
            acc_p[i].append(st_p[i])
            if i == 6:
                m_new.append(st_s[i])
            else:
                flat[i] = st_s[i]
    ys = ys.reshape(steps, nb, d).transpose(1, 0, 2)
    out_s = list(flat)
    out_s[3] = out_s[3][:, :, :RWKV_COLS]
    out_s[6] = jnp.stack(m_new)[:, 0:N_HEADS, :].transpose(0, 2, 1)
    out = [yp, ys]
    for sp, ss, ref in zip(acc_p, out_s, sample_states):
        out.append(jnp.stack(sp))
        out.append(ss.reshape(ref.shape))
    return tuple(out)
```

```python
import functools
import math

import jax
import jax.numpy as jnp
from jax import lax
from jax.experimental import pallas as pl
from jax.experimental.pallas import tpu as pltpu

F32 = jnp.float32
BF16 = jnp.bfloat16
HI = lax.Precision.HIGHEST

D_MODEL = 1024
DEPTH = 4
PAST_LEN = 16384
W_GROUP = 256
N_HEADS = 4
DK_A = 32
DV_A = 64
R_GLA = 16
GLA_TAU = 16.0
POOL_WINDOWS = (2, 4, 8, 16)
POOL_BUF = 15
N_C = 64
R_W, R_AA, R_G = 16, 16, 32
RWKV_COLS = 832
RWKV_LN_EPS = 64e-5
D_D = 64
CONV_D = 4
D_FF = 2816
FFN_CONV = 3
ALPHA = (2 * DEPTH) ** 0.25
CHUNK = 64

GLA_W = 896
RWKV_W = 896
GLA_OFF, RWKV_OFF, POOL_OFF, ML_OFF, MLIF_OFF = 0, 896, 1792, 2048, 3072
ZC = 3200
ML_W = 1024
MLIF_W = 128

VMEM_LIMIT = 56 * 1024 * 1024


def _cp(*sem):
    return pltpu.CompilerParams(dimension_semantics=sem, vmem_limit_bytes=VMEM_LIMIT)


def _mm(a, b):
    return jnp.dot(a.astype(BF16), b.astype(BF16), preferred_element_type=F32)


def _mm_nt(a, b):
    return lax.dot_general(a.astype(BF16), b.astype(BF16), (((1,), (1,)), ((), ())), preferred_element_type=F32)


def _mm_tn(a, b):
    return lax.dot_general(a.astype(BF16), b.astype(BF16), (((0,), (0,)), ((), ())), preferred_element_type=F32)


def _mm_hi(a, b):
    return jnp.dot(a, b, preferred_element_type=F32, precision=HI)


def _split2(x):
    hi = x.astype(BF16)
    return hi, (x - hi.astype(F32)).astype(BF16)


def _mm_xc(x, c2):
    hi, mid = _split2(x)
    return jnp.dot(jnp.concatenate([hi, mid], axis=1), c2, preferred_element_type=F32)


def _mm_cx(c2, x):
    hi, mid = _split2(x)
    return jnp.dot(c2, jnp.concatenate([hi, mid], axis=0), preferred_element_type=F32)


def _twice(c, axis):
    c = c.astype(F32).astype(BF16)
    return jnp.concatenate([c, c], axis=axis)


def _iota(shape, dim):
    return lax.broadcasted_iota(jnp.int32, shape, dim)


def _sigmoid(x):
    return 1.0 / (1.0 + jnp.exp(-x))


def _silu(x):
    return x * _sigmoid(x)


def _log_sigmoid(x):
    return jnp.minimum(x, 0.0) - jnp.log(1.0 + jnp.exp(-jnp.abs(x)))


def _tile4(x):
    return jnp.concatenate([x, x, x, x], axis=0)


def _block_diag_mask(rows, cols, rshift, cshift):
    return (_iota((rows, cols), 0) >> rshift) == (_iota((rows, cols), 1) >> cshift)


def _layer_norm(h, g, b, eps):
    mu = jnp.mean(h, axis=-1, keepdims=True)
    xc = h - mu
    var = jnp.mean(xc * xc, axis=-1, keepdims=True)
    return xc * lax.rsqrt(var + eps) * g + b


class _Layer:
    def __init__(self, arr, l):
        self.arr, self.l, self.shape = arr, l, tuple(arr.shape[1:])


def _layer_spec(p, single_buffer=False):
    n, l = len(p.shape), p.l
    if single_buffer:
        return pl.BlockSpec((None,) + p.shape, lambda *g: (l,) + (0,) * n, pipeline_mode=pl.Buffered(1))
    return pl.BlockSpec((None,) + p.shape, lambda *g: (l,) + (0,) * n)


def _stacked_like(p):
    return jax.ShapeDtypeStruct(p.arr.shape, p.arr.dtype)


def _linear_body(x_ref, w_ref, o_ref):
    o_ref[...] = jnp.dot(x_ref[...].astype(BF16), w_ref[...], preferred_element_type=F32)


def _linear(x, w, tm):
    t, k = x.shape
    n = w.shape[1]
    return pl.pallas_call(
        _linear_body,
        grid=(t // tm,),
        in_specs=[pl.BlockSpec((tm, k), lambda i: (i, 0)), _layer_spec(w, single_buffer=True)],
        out_specs=pl.BlockSpec((tm, n), lambda i: (i, 0)),
        out_shape=jax.ShapeDtypeStruct((t, n), F32),
        compiler_params=_cp("arbitrary"),
        name="in_proj",
    )(x, w.arr)


def _outproj(ys, x, w_ref, g_ref, b_ref):
    y = jnp.concatenate(ys, axis=1).astype(BF16)
    mixed = jnp.dot(y, w_ref[...], preferred_element_type=F32)
    return _layer_norm(ALPHA * x + mixed, g_ref[...], b_ref[...], 1e-5)


FF_CHUNK = 256
N_FF_CHUNKS = D_FF // FF_CHUNK
UP_AHEAD = 2
DOWN_ROWS = 256


def _ffn_prompt_body(ya_ref, yb_ref, yc_ref, yd_ref, x_ref, buf_ref, wo_ref, g1_ref, b1_ref,
                     wup_ref, cw_ref, cb_ref, wdn_ref, g_ref, b_ref,
                     o_ref, st_ref, carry_ref, h_ref, x1_ref, *, tm, n_tiles):
    j = pl.program_id(1)

    @pl.when(j == 0)
    def _():
        carry_ref[...] = buf_ref[0]

    x = _outproj([ya_ref[0], yb_ref[0], yc_ref[0], yd_ref[0]], x_ref[0], wo_ref, g1_ref, b1_ref)
    x1_ref[...] = x
    xb = x.astype(BF16)
    row = _iota((8, FF_CHUNK), 0)
    col = lambda c, half: slice(half * D_FF + c * FF_CHUNK, half * D_FF + (c + 1) * FF_CHUNK)
    up = lambda c: [jnp.dot(xb, wup_ref[:, col(c, half)], preferred_element_type=F32) for half in range(2)]
    ahead = [up(c) for c in range(UP_AHEAD)]
    for c in range(N_FF_CHUNKS):
        if c + UP_AHEAD < N_FF_CHUNKS:
            ahead.append(up(c + UP_AHEAD))
        u_now = ahead.pop(0)
        acts = []
        for half in range(2):
            sl = col(c, half)
            u = u_now[half]
            two_back, one_back = carry_ref[0:1, sl], carry_ref[1:2, sl]
            w0, w1, w2 = cw_ref[0:1, sl], cw_ref[1:2, sl], cw_ref[2:3, sl]
            r1 = pltpu.roll(u, 1, 0)
            r2 = pltpu.roll(u, 2, 0)
            r1_head = jnp.where(row == 0, one_back, r1[0:8])
            r2_head = jnp.where(row == 0, two_back, jnp.where(row == 1, one_back, r2[0:8]))
            head = cb_ref[:, sl] + w0 * r2_head + w1 * r1_head + w2 * u[0:8]
            body = cb_ref[:, sl] + w0 * r2 + w1 * r1 + w2 * u
            carry_ref[:, sl] = u[tm - 2:tm]
            acts.append(jnp.concatenate([head, body[8:]], axis=0))
        h_ref[:, c * FF_CHUNK:(c + 1) * FF_CHUNK] = (_silu(acts[0]) * acts[1]).astype(BF16)
    for r0 in range(0, tm, DOWN_ROWS):
        rows = slice(r0, r0 + DOWN_ROWS)
        f = jnp.dot(h_ref[rows, :], wdn_ref[...], preferred_element_type=F32)
        o_ref[0, rows, :] = _layer_norm(ALPHA * x1_ref[rows, :] + f, g_ref[...], b_ref[...], 1e-5)

    @pl.when(j == n_tiles - 1)
    def _():
        st_ref[0] = carry_ref[...]


_DENSE_PARAMS = ("w_out", "ln1_g", "ln1_b", "w_up", "ffn_cw", "ffn_cb", "w_down", "ln2_g", "ln2_b")


def _ffn_prompt(ys, x3, buf, p, tm):
    bsz, seq, d = x3.shape
    n_tiles = seq // tm
    params = [p[k] for k in _DENSE_PARAMS]
    tok = lambda w: pl.BlockSpec((1, tm, w), lambda i, j: (i, j, 0))
    st = pl.BlockSpec((1, 2, 2 * D_FF), lambda i, j: (i, 0, 0))
    return pl.pallas_call(
        functools.partial(_ffn_prompt_body, tm=tm, n_tiles=n_tiles),
        grid=(bsz, n_tiles),
        in_specs=[tok(W_GROUP)] * 4 + [tok(d), st] + [_layer_spec(q, single_buffer=True) for q in params],
        out_specs=[tok(d), st],
        out_shape=[jax.ShapeDtypeStruct((bsz, seq, d), F32), jax.ShapeDtypeStruct((bsz, 2, 2 * D_FF), F32)],
        scratch_shapes=[pltpu.VMEM((2, 2 * D_FF), F32), pltpu.VMEM((tm, D_FF), BF16), pltpu.VMEM((tm, d), F32)],
        compiler_params=_cp("arbitrary", "arbitrary"),
        name="ffn_prompt",
    )(*ys, x3, buf, *[q.arr for q in params])


def _ffn_sample_body(ya_ref, yb_ref, yc_ref, yd_ref, x_ref, buf_ref, wo_ref, g1_ref, b1_ref,
                     wup_ref, cw_ref, cb_ref, wdn_ref, g_ref, b_ref,
                     o_ref, st_ref, h_ref, *, nb, steps):
    t = nb * steps
    x = _outproj([ya_ref[...], yb_ref[...], yc_ref[...], yd_ref[...]], x_ref[...], wo_ref, g1_ref, b1_ref)
    xb = x.astype(BF16)
    for c in range(N_FF_CHUNKS):
        acts = []
        for half in range(2):
            lo = half * D_FF + c * FF_CHUNK
            sl = slice(lo, lo + FF_CHUNK)
            u = jnp.dot(xb, wup_ref[:, sl], preferred_element_type=F32)
            b0 = buf_ref[:, 0, sl]
            b1 = buf_ref[:, 1, sl]
            back1 = jnp.concatenate([b1, u[0:t - nb]], axis=0)
            back2 = jnp.concatenate([b0, b1, u[0:t - 2 * nb]], axis=0)
            conv = cb_ref[:, sl] + cw_ref[0:1, sl] * back2 + cw_ref[1:2, sl] * back1 + cw_ref[2:3, sl] * u
            st_ref[:, 0, sl] = u[t - 2 * nb:t - nb]
            st_ref[:, 1, sl] = u[t - nb:t]
            acts.append(conv)
        h_ref[:, c * FF_CHUNK:(c + 1) * FF_CHUNK] = (_silu(acts[0]) * acts[1]).astype(BF16)
    f = jnp.dot(h_ref[...], wdn_ref[...], preferred_element_type=F32)
    o_ref[...] = _layer_norm(ALPHA * x + f, g_ref[...], b_ref[...], 1e-5)


def _ffn_sample(ys, x, buf, p, nb, steps):
    t, d = x.shape
    params = [p[k] for k in _DENSE_PARAMS]
    full = lambda shape: pl.BlockSpec(shape, lambda i: (0,) * len(shape))
    return pl.pallas_call(
        functools.partial(_ffn_sample_body, nb=nb, steps=steps),
        grid=(1,),
        in_specs=[full((t, W_GROUP))] * 4 + [full((t, d)), _layer_spec(buf)] + [_layer_spec(q) for q in params],
        out_specs=[full((t, d)), _layer_spec(buf)],
        out_shape=[jax.ShapeDtypeStruct((t, d), F32), _stacked_like(buf)],
        input_output_aliases={5: 1},
        scratch_shapes=[pltpu.VMEM((t, D_FF), BF16)],
        compiler_params=_cp("arbitrary"),
        name="ffn_sample",
    )(*ys, x, buf.arr, *[q.arr for q in params])


def _exp_masks():
    t = _iota((CHUNK, 256), 0)
    s = _iota((CHUNK, 256), 1) & (CHUNK - 1)
    return t >= s, t > s, t == s


def _bd256():
    return _block_diag_mask(256, 256, 6, 6)


def _tril_blocks(n):
    r = _iota((n, n), 0)
    c = _iota((n, n), 1)
    return ((r >= c) & ((r >> 6) == (c >> 6))).astype(F32)


def _mask_bf16(mask):
    return mask.astype(F32).astype(BF16)


def _bdx(x, mask01):
    return _tile4(x.astype(BF16)) * mask01


def _put_block_diag(dst_ref, blocks, rows, cols):
    dst_ref[...] = jnp.zeros(dst_ref.shape, F32)
    for h in range(N_HEADS):
        dst_ref[h * rows:(h + 1) * rows, h * cols:(h + 1) * cols] = blocks[h]


def _get_block_diag(src_ref, out_ref, rows, cols):
    for h in range(N_HEADS):
        out_ref[0, h] = src_ref[h * rows:(h + 1) * rows, h * cols:(h + 1) * cols]


def _gla_prompt_body(z_ref, s0_ref, wa_ref, ba_ref, ng_ref, y_ref, so_ref, sbd_ref, *, n_chunks, n_blocks):
    j = pl.program_id(1)

    @pl.when(j == 0)
    def _():
        _put_block_diag(sbd_ref, s0_ref[0], DK_A, DV_A)

    lb = n_chunks * CHUNK
    causal, _, _ = _exp_masks()
    bd_state = _block_diag_mask(128, 256, 5, 6)
    bd_k = _mask_bf16(_block_diag_mask(256, 128, 6, 5))
    bd_v = _mask_bf16(_bd256())
    pavg2 = _twice(bd_v.astype(F32) * (1.0 / DV_A), 0)

    lg = _log_sigmoid(_mm(z_ref[0, :, 768:896], wa_ref[...]) + ba_ref[...]) * (1.0 / GLA_TAU)
    bc_all = _mm_cx(_twice(_tril_blocks(lb), 1), lg)

    cs = range(n_chunks)
    sl = [slice(c * CHUNK, (c + 1) * CHUNK) for c in cs]
    q = [z_ref[0, sl[c], 0:128] * DK_A ** -0.5 for c in cs]
    bc = [bc_all[sl[c]] for c in cs]
    bl = [bc[c][CHUNK - 1:CHUNK] for c in cs]
    rho = [bc[c][CHUNK // 2 - 1:CHUNK // 2] for c in cs]
    att = [_mm_nt(q[c] * jnp.exp(bc[c] - rho[c]), _bdx(z_ref[0, sl[c], 128:256] * jnp.exp(rho[c] - bc[c]), bd_k))
           for c in cs]
    incr = [jnp.where(bd_state, _mm_tn(z_ref[0, sl[c], 128:256] * jnp.exp(bl[c] - bc[c]), z_ref[0, sl[c], 256:512]),
                      0.0) for c in cs]
    o_intra = [_mm(jnp.where(causal, att[c], 0.0), _bdx(z_ref[0, sl[c], 256:512], bd_v)) for c in cs]
    q_dec = [q[c] * jnp.exp(bc[c]) for c in cs]
    decay = []
    for c in cs:
        decay_col = jnp.broadcast_to(jnp.exp(bl[c]), (128, 128)).T
        decay.append(jnp.concatenate([decay_col, decay_col], axis=1))
    starts = [sbd_ref[...]]
    for c in range(n_chunks):
        starts.append(decay[c] * starts[c] + incr[c])
    sbd_ref[...] = starts[n_chunks]
    o = jnp.concatenate([_mm(q_dec[c], starts[c]) + o_intra[c] for c in range(n_chunks)], axis=0)
    ms = _mm_xc(o * o, pavg2)
    y_ref[0] = (o * lax.rsqrt(ms + 1e-6) * ng_ref[...] * _silu(z_ref[0, :, 512:768])).astype(BF16)

    @pl.when(j == n_blocks - 1)
    def _():
        _get_block_diag(sbd_ref, so_ref, DK_A, DV_A)


def _gla_prompt(z3, s0, wa, ba, ng, lb):
    bsz, seq, _ = z3.shape
    n_blocks = seq // lb
    st = pl.BlockSpec((1, N_HEADS, DK_A, DV_A), lambda i, j: (i, 0, 0, 0))
    return pl.pallas_call(
        functools.partial(_gla_prompt_body, n_chunks=lb // CHUNK, n_blocks=n_blocks),
        grid=(bsz, n_blocks),
        in_specs=[pl.BlockSpec((1, lb, GLA_W), lambda i, j: (i, j, GLA_OFF // GLA_W)), st,
                  _layer_spec(wa), _layer_spec(ba), _layer_spec(ng)],
        out_specs=[pl.BlockSpec((1, lb, W_GROUP), lambda i, j: (i, j, 0)), st],
        out_shape=[jax.ShapeDtypeStruct((bsz, seq, W_GROUP), BF16),
                   jax.ShapeDtypeStruct((bsz, N_HEADS, DK_A, DV_A), F32)],
        scratch_shapes=[pltpu.VMEM((128, 256), F32)],
        compiler_params=_cp("arbitrary", "arbitrary"),
        name="gla_prompt",
    )(z3, s0, wa.arr, ba.arr, ng.arr)


def _pool_prompt_body(z_ref, buf_ref, w_ref, sc_ref, y_ref, bo_ref, hist_ref, *, lb, n_blocks, start_pos):
    j = pl.program_id(1)

    @pl.when(j == 0)
    def _():
        hist_ref[0:1, :] = jnp.zeros((1, W_GROUP), F32)
        hist_ref[1:16, :] = buf_ref[0]

    zp = z_ref[0]
    e = jnp.concatenate([hist_ref[...], zp], axis=0)
    s2 = e + pltpu.roll(e, 1, 0)
    s4 = s2 + pltpu.roll(s2, 2, 0)
    s8 = s4 + pltpu.roll(s4, 4, 0)
    s16 = s8 + pltpu.roll(s8, 8, 0)
    lane = _iota((lb, W_GROUP), 1)
    win = jnp.where(lane < 64, s2[16:], jnp.where(lane < 128, s4[16:], jnp.where(lane < 192, s8[16:], s16[16:])))
    by_lane = lambda ln, vals: jnp.where(ln < 64, vals[0], jnp.where(ln < 128, vals[1],
                                                                      jnp.where(ln < 192, vals[2], vals[3])))
    pos = start_pos + j * lb + _iota((16, W_GROUP), 0)
    cnt_head = jnp.minimum(pos + 1, by_lane(_iota((16, W_GROUP), 1), POOL_WINDOWS)).astype(F32)
    inv_width = by_lane(_iota((1, W_GROUP), 1), [1.0 / w for w in POOL_WINDOWS])
    mean = jnp.concatenate([win[0:16] / cnt_head, win[16:] * inv_width], axis=0)
    y_ref[0] = (_mm(mean - zp, w_ref[...]) * sc_ref[...]).astype(BF16)
    hist_ref[1:16, :] = e[lb + 1:lb + 16]

    @pl.when(j == n_blocks - 1)
    def _():
        bo_ref[0] = hist_ref[1:16, :]


def _pool_prompt(z3, buf, wbd, scale, lb, start_pos):
    bsz, seq, _ = z3.shape
    n_blocks = seq // lb
    st = pl.BlockSpec((1, POOL_BUF, W_GROUP), lambda i, j: (i, 0, 0))
    return pl.pallas_call(
        functools.partial(_pool_prompt_body, lb=lb, n_blocks=n_blocks, start_pos=start_pos),
        grid=(bsz, n_blocks),
        in_specs=[pl.BlockSpec((1, lb, W_GROUP), lambda i, j: (i, j, POOL_OFF // W_GROUP)), st,
                  _layer_spec(wbd), _layer_spec(scale)],
        out_specs=[pl.BlockSpec((1, lb, W_GROUP), lambda i, j: (i, j, 0)), st],
        out_shape=[jax.ShapeDtypeStruct((bsz, seq, W_GROUP), BF16),
                   jax.ShapeDtypeStruct((bsz, POOL_BUF, W_GROUP), F32)],
        scratch_shapes=[pltpu.VMEM((16, W_GROUP), F32)],
        compiler_params=_cp("arbitrary", "arbitrary"),
        name="pool_prompt",
    )(z3, buf, wbd.arr, scale.arr)


def _rwkv_prompt_body(z_ref, sh0_ref, s0_ref, mu_ref, w0_ref, w2_ref, a0_ref, a2_ref, g2_ref, kk_ref, ka_ref,
                      rk_ref, lg_ref, lb_ref, y_ref, so_ref, sho_ref,
                      nbd_ref, carry_ref, r_s, k_s, v_s, kk_s, b_s, lw_s, g_s, *, n_chunks, n_blocks, lb):
    j = pl.program_id(1)

    @pl.when(j == 0)
    def _():
        _put_block_diag(nbd_ref, s0_ref[0], N_C, N_C)
        carry_ref[...] = sh0_ref[0]

    bd = _bd256()
    bd_m = _mask_bf16(bd)
    ones2 = _twice(bd, 0)
    pavg2 = _twice(bd.astype(F32) * (1.0 / N_C), 0)

    f = z_ref[0]
    prev = jnp.concatenate([carry_ref[...], f[:-1]], axis=0)
    mixed = f + (prev - f) * mu_ref[...]
    carry_ref[...] = f[lb - 1:lb]
    r = mixed[:, 0:256]
    k = mixed[:, 256:512]
    v = mixed[:, 512:768]
    low = mixed[:, 768:896]
    lw_all = -_sigmoid(w0_ref[...] + _mm(jnp.tanh(low), w2_ref[...])) * math.exp(-0.5)
    lw_s[...] = lw_all
    a = _sigmoid(a0_ref[...] + _mm(low, a2_ref[...]))
    g_s[...] = _mm(_sigmoid(low), g2_ref[...])
    kk = k * kk_ref[...]
    kk = kk * lax.rsqrt(jnp.maximum(_mm_xc(kk * kk, ones2), 1e-24))
    r_s[...] = r
    v_s[...] = v
    kk_s[...] = kk
    k_s[...] = k * (1.0 + (a - 1.0) * ka_ref[...])
    b_s[...] = kk * a
    gc_all = _mm_cx(_twice(_tril_blocks(lb), 1), lw_all)

    lower, strict, eye_exp = _exp_masks()
    eye_f = eye_exp.astype(F32)

    cs = range(n_chunks)
    sl = [slice(c * CHUNK, (c + 1) * CHUNK) for c in cs]
    gc = [gc_all[s] for s in sl]
    kt = [kk_s[sl[c], :] * jnp.exp(gc[c] - lw_s[sl[c], :]) for c in cs]
    rt = [r_s[sl[c], :] * jnp.exp(gc[c]) for c in cs]
    a_kk, a_kb, a_rk, a_rb = [], [], [], []
    for c in cs:
        einv = jnp.exp(-gc[c])
        lhs = jnp.concatenate([kt[c], rt[c]], axis=0)
        rhs = jnp.concatenate([_bdx(k_s[sl[c], :] * einv, bd_m), _bdx(b_s[sl[c], :] * einv, bd_m)], axis=0)
        aa = _mm_nt(lhs, rhs)
        a_kk.append(jnp.where(strict, aa[0:CHUNK, 0:256], 0.0))
        a_kb.append(jnp.where(strict, aa[0:CHUNK, 256:512], 0.0))
        a_rk.append(jnp.where(lower, aa[CHUNK:, 0:256], 0.0))
        a_rb.append(jnp.where(lower, aa[CHUNK:, 256:512], 0.0))
    t = [eye_f - a_kb[c] for c in cs]
    p = [_mm(a_kb[c], _bdx(a_kb[c], bd_m)) for c in cs]
    for _ in range(4):
        m = [_mm(jnp.concatenate([p[c], t[c]], axis=0), _bdx(p[c], bd_m)) for c in cs]
        p = [m[c][0:CHUNK] for c in cs]
        t = [t[c] + m[c][CHUNK:] for c in cs]
    t = [t[c] + _mm(t[c], _bdx(p[c], bd_m)) for c in cs]
    av = [_mm(jnp.concatenate([a_kk[c], a_rk[c]], axis=0), _bdx(v_s[sl[c], :], bd_m)) for c in cs]
    tu = [_mm(t[c], jnp.concatenate([_bdx(av[c][0:CHUNK], bd_m), _bdx(kt[c], bd_m)], axis=1)) for c in cs]
    u0 = [tu[c][:, 0:256] for c in cs]
    tk = [tu[c][:, 256:512] for c in cs]
    ro = [_mm(a_rb[c], jnp.concatenate([_bdx(tk[c], bd_m), _bdx(u0[c], bd_m)], axis=1)) for c in cs]
    r_eff = [rt[c] - ro[c][:, 0:256] for c in cs]
    o0 = [av[c][CHUNK:] - ro[c][:, 256:512] for c in cs]
    gl = [gc[c][CHUNK - 1:CHUNK] for c in cs]
    dk = [jnp.exp(gl[c] - gc[c]) for c in cs]
    bdk = [b_s[sl[c], :] * dk[c] for c in cs]
    a_st = [jnp.where(bd, -_mm_tn(tk[c], bdk[c]), 0.0) for c in cs]
    b_st = [jnp.where(bd, _mm_tn(jnp.concatenate([v_s[sl[c], :], -u0[c]], axis=0),
                                 jnp.concatenate([k_s[sl[c], :] * dk[c], bdk[c]], axis=0)), 0.0)
            for c in cs]
    n = nbd_ref[...]
    o = []
    for c in cs:
        o.append(_mm_nt(r_eff[c], n) + o0[c])
        n = n * jnp.exp(gl[c]) + _mm(n, a_st[c]) + b_st[c]
    nbd_ref[...] = n
    o = jnp.concatenate(o, axis=0)
    mu = _mm_xc(o, pavg2)
    xc = o - mu
    var = _mm_xc(xc * xc, pavg2)
    on = xc * lax.rsqrt(var + RWKV_LN_EPS) * lg_ref[...] + lb_ref[...]
    bonus = _mm_xc(r_s[...] * k_s[...] * rk_ref[...], ones2) * v_s[...]
    y_ref[0] = ((on + bonus) * g_s[...]).astype(BF16)

    @pl.when(j == n_blocks - 1)
    def _():
        _get_block_diag(nbd_ref, so_ref, N_C, N_C)
        sho_ref[0] = carry_ref[...]


_RWKV_PARAMS = ("mu", "w0", "w2", "a0", "a2", "g2", "k_k", "k_a", "r_k", "ln_g", "ln_b")


def _rwkv_prompt(z3, sh0, s0, p, lb):
    bsz, seq, _ = z3.shape
    n_blocks = seq // lb
    st = pl.BlockSpec((1, N_HEADS, N_C, N_C), lambda i, j: (i, 0, 0, 0))
    sh = pl.BlockSpec((1, 1, RWKV_W), lambda i, j: (i, 0, 0))
    blk = pltpu.VMEM((lb, 256), F32)
    params = [p[k] for k in _RWKV_PARAMS]
    return pl.pallas_call(
        functools.partial(_rwkv_prompt_body, n_chunks=lb // CHUNK, n_blocks=n_blocks, lb=lb),
        grid=(bsz, n_blocks),
        in_specs=[pl.BlockSpec((1, lb, RWKV_W), lambda i, j: (i, j, RWKV_OFF // RWKV_W)), sh, st]
        + [_layer_spec(q) for q in params],
        out_specs=[pl.BlockSpec((1, lb, W_GROUP), lambda i, j: (i, j, 0)), st, sh],
        out_shape=[jax.ShapeDtypeStruct((bsz, seq, W_GROUP), BF16),
                   jax.ShapeDtypeStruct((bsz, N_HEADS, N_C, N_C), F32),
                   jax.ShapeDtypeStruct((bsz, 1, RWKV_W), F32)],
        scratch_shapes=[pltpu.VMEM((256, 256), F32), pltpu.VMEM((1, RWKV_W), F32),
                        blk, blk, blk, blk, blk, blk, blk],
        compiler_params=_cp("arbitrary", "arbitrary"),
        name="rwkv_prompt",
    )(z3, sh0, s0, *[q.arr for q in params])


def _mlstm_prompt_body(z_ref, zif_ref, cb0_ref, c0_ref, n0_ref, m0_ref, cw_ref, cbias_ref, bif_ref, ng_ref,
                       y_ref, co_ref, no_ref, mo_ref, cbo_ref,
                       cbd_ref, n_ref, m_ref, carry_ref, qk_s, *, n_chunks, n_blocks, lb):
    j = pl.program_id(1)

    @pl.when(j == 0)
    def _():
        _put_block_diag(cbd_ref, c0_ref[0], D_D, D_D)
        n_ref[...] = n0_ref[0]
        m_ref[...] = m0_ref[0]
        carry_ref[...] = cb0_ref[0]

    qk = z_ref[0, :, 0:512]
    conv = cbias_ref[...] + cw_ref[CONV_D - 1:CONV_D, :] * qk
    head = cbias_ref[...] + cw_ref[CONV_D - 1:CONV_D, :] * qk[0:8]
    row = _iota((8, 512), 0)
    for back in range(1, CONV_D):
        w = cw_ref[CONV_D - 1 - back:CONV_D - back, :]
        rolled = pltpu.roll(qk, back, 0)
        conv = conv + w * rolled
        fixed = rolled[0:8]
        for r in range(back):
            fixed = jnp.where(row == r, carry_ref[CONV_D - 1 - back + r:CONV_D - back + r, :], fixed)
        head = head + w * fixed
    qk_s[...] = _silu(jnp.concatenate([head, conv[8:]], axis=0))
    carry_ref[...] = qk[lb - (CONV_D - 1):lb]

    bd = _bd256()
    bd_m = _mask_bf16(bd)
    ones2 = _twice(bd, 0)
    pavg2 = _twice(bd.astype(F32) * (1.0 / D_D), 0)
    causal, _, eye_exp = _exp_masks()
    er = _iota((128, 512), 0)
    ec = _iota((128, 512), 1)
    expand2 = _twice(((ec < 256) & (er == (ec >> 6))) | ((ec >= 256) & (er == 4 + ((ec - 256) >> 6))), 0)
    is_f = (_iota((lb, 128), 1) >= 4) & (_iota((lb, 128), 1) < 8)
    lane = _iota((CHUNK, 256), 1)

    gates = zif_ref[0] + bif_ref[...]
    gates = jnp.where(is_f, _log_sigmoid(gates), gates)
    gates = jnp.where(is_f, _mm_cx(_twice(_tril_blocks(lb), 1), gates), gates)
    ge_all = _mm_xc(gates, expand2)

    cs = range(n_chunks)
    sl = [slice(c * CHUNK, (c + 1) * CHUNK) for c in cs]
    q = [qk_s[sl[c], 0:256] * D_D ** -0.5 for c in cs]
    k = [qk_s[sl[c], 256:512] for c in cs]
    iexp = [ge_all[sl[c], 0:256] for c in cs]
    fexp = [ge_all[sl[c], 256:512] for c in cs]
    qk_att = [_mm_nt(q[c], _bdx(k[c], bd_m)) for c in cs]
    dlog, m_t, m_prev = [], [], [m_ref[...]]
    for c in cs:
        d_row = jnp.sum(jnp.where(eye_exp, iexp[c] - fexp[c], 0.0), axis=0, keepdims=True)
        dl = jnp.where(causal, fexp[c] + d_row, -jnp.inf)
        mx = [jnp.max(dl[:, h * 64:(h + 1) * 64], axis=1, keepdims=True) for h in range(N_HEADS)]
        mx = jnp.where(lane < 64, mx[0], jnp.where(lane < 128, mx[1], jnp.where(lane < 192, mx[2], mx[3])))
        dlog.append(dl)
        m_t.append(jnp.maximum(fexp[c] + m_prev[c], mx))
        m_prev.append(m_t[c][CHUNK - 1:CHUNK])
    w0 = [jnp.exp(fexp[c] + m_prev[c] - m_t[c]) for c in cs]
    s = [jnp.exp(dlog[c] - m_t[c]) * qk_att[c] for c in cs]
    f_last = [fexp[c][CHUNK - 1:CHUNK] for c in cs]
    kw = [k[c] * jnp.exp(f_last[c] - fexp[c] + iexp[c] - m_prev[c + 1]) for c in cs]
    w0f = [jnp.exp(f_last[c] + m_prev[c] - m_prev[c + 1]) for c in cs]
    incr = [jnp.where(bd, _mm_tn(kw[c], z_ref[0, sl[c], 512:768]), 0.0) for c in cs]
    sv = [_mm(s[c], _bdx(z_ref[0, sl[c], 512:768], bd_m)) for c in cs]
    c_start, n_start = [cbd_ref[...]], [n_ref[...]]
    for c in cs:
        c_start.append(w0f[c] * c_start[c] + incr[c])
        n_start.append(w0f[c] * n_start[c] + jnp.sum(kw[c], axis=0, keepdims=True))
    cbd_ref[...] = c_start[n_chunks]
    n_ref[...] = n_start[n_chunks]
    m_ref[...] = m_prev[n_chunks]
    num = [w0[c] * _mm(q[c], c_start[c]) + sv[c] for c in cs]
    den = [_mm_xc(w0[c] * (q[c] * n_start[c]) + s[c], ones2) for c in cs]
    hh = jnp.concatenate([num[c] / jnp.maximum(jnp.abs(den[c]), jnp.exp(-m_t[c])) for c in cs], axis=0)
    ms = _mm_xc(hh * hh, pavg2)
    y_ref[0] = (_sigmoid(z_ref[0, :, 768:1024]) * (hh * lax.rsqrt(ms + 1e-6) * ng_ref[...])).astype(BF16)

    @pl.when(j == n_blocks - 1)
    def _():
        _get_block_diag(cbd_ref, co_ref, D_D, D_D)
        no_ref[0] = n_ref[...]
        mo_ref[0] = m_ref[...]
        cbo_ref[0] = carry_ref[...]


def _mlstm_prompt(z3, cb0, c0, n0, m0, cw, cbias, bif, ng, lb):
    bsz, seq, _ = z3.shape
    n_blocks = seq // lb
    cst = pl.BlockSpec((1, N_HEADS, D_D, D_D), lambda i, j: (i, 0, 0, 0))
    row = pl.BlockSpec((1, 1, 256), lambda i, j: (i, 0, 0))
    cbs = pl.BlockSpec((1, CONV_D - 1, 512), lambda i, j: (i, 0, 0))
    return pl.pallas_call(
        functools.partial(_mlstm_prompt_body, n_chunks=lb // CHUNK, n_blocks=n_blocks, lb=lb),
        grid=(bsz, n_blocks),
        in_specs=[pl.BlockSpec((1, lb, ML_W), lambda i, j: (i, j, ML_OFF // ML_W)),
                  pl.BlockSpec((1, lb, MLIF_W), lambda i, j: (i, j, MLIF_OFF // MLIF_W)),
                  cbs, cst, row, row, _layer_spec(cw), _layer_spec(cbias), _layer_spec(bif), _layer_spec(ng)],
        out_specs=[pl.BlockSpec((1, lb, W_GROUP), lambda i, j: (i, j, 0)), cst, row, row, cbs],
        out_shape=[jax.ShapeDtypeStruct((bsz, seq, W_GROUP), BF16),
                   jax.ShapeDtypeStruct((bsz, N_HEADS, D_D, D_D), F32),
                   jax.ShapeDtypeStruct((bsz, 1, 256), F32),
                   jax.ShapeDtypeStruct((bsz, 1, 256), F32),
                   jax.ShapeDtypeStruct((bsz, CONV_D - 1, 512), F32)],
        scratch_shapes=[pltpu.VMEM((256, 256), F32), pltpu.VMEM((1, 256), F32), pltpu.VMEM((1, 256), F32),
                        pltpu.VMEM((CONV_D - 1, 512), F32), pltpu.VMEM((lb, 512), F32)],
        compiler_params=_cp("arbitrary", "arbitrary"),
        name="mlstm_prompt",
    )(z3, z3, cb0, c0, n0, m0, cw.arr, cbias.arr, bif.arr, ng.arr)


T_COLS = 1024


def _to_rows(src_ref, dst_ref, n):
    for i in range(n // T_COLS):
        dst_ref[i * T_COLS:(i + 1) * T_COLS, :] = src_ref[:, i * T_COLS:(i + 1) * T_COLS].T


def _to_cols(src_ref, dst_ref, n):
    for i in range(n // T_COLS):
        dst_ref[:, i * T_COLS:(i + 1) * T_COLS] = src_ref[i * T_COLS:(i + 1) * T_COLS, :].T


def _per_head(x, fn):
    return jnp.concatenate([fn(x[h * 64:(h + 1) * 64], h) for h in range(N_HEADS)], axis=0)


def _expand_heads(x, nb):
    return jnp.concatenate([jnp.broadcast_to(x[h:h + 1], (64, nb)) for h in range(N_HEADS)], axis=0)


def _gla_sample_body(z_ref, s_ref, wa_ref, ba_ref, ng_ref, y_ref, so_ref,
                     st_s, dec_s, k_s, q_s, v_s, o_s, *, steps, nb):
    n_state = N_HEADS * DK_A * DV_A
    _to_rows(s_ref, st_s, n_state)
    for t in range(steps):
        zt = z_ref[t].T
        q_s[t] = zt[0:128] * DK_A ** -0.5
        k_s[t] = zt[128:256]
        v_s[t] = zt[256:512]
        lg = _log_sigmoid(_mm_hi(wa_ref[...], zt[768:896]) + ba_ref[...]) * (1.0 / GLA_TAU)
        dec_s[t] = jnp.exp(lg)
    o_s[...] = jnp.zeros(o_s.shape, F32)

    def per_key(hk, carry):
        r0 = pl.multiple_of(hk * DV_A, DV_A)
        v0 = pl.multiple_of((hk >> 5) * DV_A, DV_A)
        s = st_s[pl.ds(r0, DV_A), :]
        for t in range(steps):
            s = dec_s[t, pl.ds(hk, 1), :] * s + k_s[t, pl.ds(hk, 1), :] * v_s[t, pl.ds(v0, DV_A), :]
            o_s[t, pl.ds(v0, DV_A), :] += q_s[t, pl.ds(hk, 1), :] * s
        st_s[pl.ds(r0, DV_A), :] = s
        return carry

    lax.fori_loop(0, N_HEADS * DK_A, per_key, 0, unroll=2)
    for t in range(steps):
        g = z_ref[t, :, 512:768].T
        on = _per_head(o_s[t], lambda oh, h: oh * lax.rsqrt(jnp.mean(oh * oh, axis=0, keepdims=True) + 1e-6))
        y_ref[t] = (on * ng_ref[...] * _silu(g)).T.astype(BF16)
    _to_cols(st_s, so_ref, n_state)


def _gla_sample(zs, s, wa_t, ba_col, ng_col):
    steps, nb, _ = zs.shape
    n_state = N_HEADS * DK_A * DV_A
    full = lambda shape: pl.BlockSpec(shape, lambda i: (0,) * len(shape))
    return pl.pallas_call(
        functools.partial(_gla_sample_body, steps=steps, nb=nb),
        grid=(1,),
        in_specs=[pl.BlockSpec((steps, nb, GLA_W), lambda i: (0, 0, GLA_OFF // GLA_W)), _layer_spec(s),
                  _layer_spec(wa_t), _layer_spec(ba_col), _layer_spec(ng_col)],
        out_specs=[full((steps, nb, W_GROUP)), _layer_spec(s)],
        out_shape=[jax.ShapeDtypeStruct((steps, nb, W_GROUP), BF16), _stacked_like(s)],
        input_output_aliases={1: 1},
        scratch_shapes=[pltpu.VMEM((n_state, nb), F32), pltpu.VMEM((steps, 128, nb), F32),
                        pltpu.VMEM((steps, 128, nb), F32), pltpu.VMEM((steps, 128, nb), F32),
                        pltpu.VMEM((steps, 256, nb), F32), pltpu.VMEM((steps, 256, nb), F32)],
        compiler_params=_cp("arbitrary"),
        name="gla_sample",
    )(zs, s.arr, wa_t.arr, ba_col.arr, ng_col.arr)


def _pool_sample_body(z_ref, buf_ref, w_ref, sc_ref, y_ref, bo_ref, *, steps, nb, start_pos):
    ext = [buf_ref[:, j * W_GROUP:(j + 1) * W_GROUP] for j in range(POOL_BUF)] + [z_ref[t] for t in range(steps)]
    lane = _iota((nb, W_GROUP), 1)

    def by_group(vals):
        return jnp.where(lane < 64, vals[0], jnp.where(lane < 128, vals[1], jnp.where(lane < 192, vals[2], vals[3])))

    for t in range(steps):
        r = POOL_BUF + t
        acc = ext[r]
        sums = {}
        for back in range(1, max(POOL_WINDOWS)):
            acc = acc + ext[r - back]
            sums[back + 1] = acc
        win = by_group([sums[w] for w in POOL_WINDOWS])
        cnt = by_group([float(min(start_pos + t + 1, w)) for w in POOL_WINDOWS])
        y_ref[t] = (_mm(win / cnt - ext[r], w_ref[...]) * sc_ref[...]).astype(BF16)
    for j in range(POOL_BUF):
        bo_ref[:, j * W_GROUP:(j + 1) * W_GROUP] = ext[steps + j]


def _pool_sample(zs, buf, wbd, scale, start_pos):
    steps, nb, _ = zs.shape
    full = lambda shape: pl.BlockSpec(shape, lambda i: (0,) * len(shape))
    return pl.pallas_call(
        functools.partial(_pool_sample_body, steps=steps, nb=nb, start_pos=start_pos),
        grid=(1,),
        in_specs=[pl.BlockSpec((steps, nb, W_GROUP), lambda i: (0, 0, POOL_OFF // W_GROUP)),
                  _layer_spec(buf), _layer_spec(wbd), _layer_spec(scale)],
        out_specs=[full((steps, nb, W_GROUP)), _layer_spec(buf)],
        out_shape=[jax.ShapeDtypeStruct((steps, nb, W_GROUP), BF16), _stacked_like(buf)],
        input_output_aliases={1: 1},
        compiler_params=_cp("arbitrary"),
        name="pool_sample",
    )(zs, buf.arr, wbd.arr, scale.arr)


def _rwkv_sample_body(z_ref, sh_ref, s_ref, mu_ref, w0_ref, w2_ref, a0_ref, a2_ref, g2_ref, kk_ref, ka_ref,
                      rk_ref, lg_ref, lb_ref, y_ref, so_ref, sho_ref,
                      st_s, r_s, k_s, v_s, kk_s, b_s, w_s, g_s, o_s, *, steps, nb):
    n_state = N_HEADS * N_C * N_C
    _to_rows(s_ref, st_s, n_state)
    prev = sh_ref[...].T
    for t in range(steps):
        f = z_ref[t].T
        mixed = f + (prev - f) * mu_ref[...]
        prev = f
        k = mixed[256:512]
        low = mixed[768:896]
        w_s[t] = jnp.exp(-_sigmoid(w0_ref[...] + _mm_hi(w2_ref[...], jnp.tanh(low))) * math.exp(-0.5))
        a = _sigmoid(a0_ref[...] + _mm_hi(a2_ref[...], low))
        g_s[t] = _mm_hi(g2_ref[...], _sigmoid(low))
        kk = _per_head(k * kk_ref[...], lambda kh, h: kh * lax.rsqrt(
            jnp.maximum(jnp.sum(kh * kh, axis=0, keepdims=True), 1e-24)))
        r_s[t] = mixed[0:256]
        v_s[t] = mixed[512:768]
        kk_s[t] = kk
        k_s[t] = k * (1.0 + (a - 1.0) * ka_ref[...])
        b_s[t] = kk * a

    def per_value(hv, carry):
        r0 = pl.multiple_of(hv * N_C, N_C)
        hs = pl.ds(pl.multiple_of((hv >> 6) * N_C, N_C), N_C)
        s = st_s[pl.ds(r0, N_C), :]
        for t in range(steps):
            sa = -jnp.sum(s * kk_s[t, hs, :], axis=0, keepdims=True)
            s = s * w_s[t, hs, :] + sa * b_s[t, hs, :] + v_s[t, pl.ds(hv, 1), :] * k_s[t, hs, :]
            o_s[t, pl.ds(hv, 1), :] = jnp.sum(s * r_s[t, hs, :], axis=0, keepdims=True)
        st_s[pl.ds(r0, N_C), :] = s
        return carry

    lax.fori_loop(0, N_HEADS * N_C, per_value, 0, unroll=4)
    for t in range(steps):
        def norm(oh, h):
            mu = jnp.mean(oh, axis=0, keepdims=True)
            xc = oh - mu
            return xc * lax.rsqrt(jnp.mean(xc * xc, axis=0, keepdims=True) + RWKV_LN_EPS)
        on = _per_head(o_s[t], norm) * lg_ref[...] + lb_ref[...]
        rk = r_s[t] * k_s[t] * rk_ref[...]
        v = v_s[t]
        bonus = _per_head(rk, lambda x, h: jnp.sum(x, axis=0, keepdims=True) * v[h * 64:(h + 1) * 64])
        y_ref[t] = ((on + bonus) * g_s[t]).T.astype(BF16)
    _to_cols(st_s, so_ref, n_state)
    sho_ref[...] = z_ref[steps - 1]


def _rwkv_sample(zs, sh, s, p):
    steps, nb, _ = zs.shape
    n_state = N_HEADS * N_C * N_C
    full = lambda shape: pl.BlockSpec(shape, lambda i: (0,) * len(shape))
    blk = pltpu.VMEM((steps, 256, nb), F32)
    params = [p[k + "_c"] for k in _RWKV_PARAMS]
    return pl.pallas_call(
        functools.partial(_rwkv_sample_body, steps=steps, nb=nb),
        grid=(1,),
        in_specs=[pl.BlockSpec((steps, nb, RWKV_W), lambda i: (0, 0, RWKV_OFF // RWKV_W)), _layer_spec(sh),
                  _layer_spec(s)] + [_layer_spec(q) for q in params],
        out_specs=[full((steps, nb, W_GROUP)), _layer_spec(s), _layer_spec(sh)],
        out_shape=[jax.ShapeDtypeStruct((steps, nb, W_GROUP), BF16), _stacked_like(s), _stacked_like(sh)],
        input_output_aliases={2: 1, 1: 2},
        scratch_shapes=[pltpu.VMEM((n_state, nb), F32), blk, blk, blk, blk, blk, blk, blk, blk],
        compiler_params=_cp("arbitrary"),
        name="rwkv_sample",
    )(zs, sh.arr, s.arr, *[q.arr for q in params])


def _mlstm_sample_body(z_ref, zif_ref, cb_ref, c_ref, n_ref, m_ref, cw_ref, cbias_ref, bif_ref, ng_ref,
                       y_ref, co_ref, no_ref, mo_ref, cbo_ref,
                       ct_s, q_s, k_s, v_s, wf_s, den_s, em_s, o_s, *, steps, nb):
    n_state = N_HEADS * D_D * D_D
    _to_rows(c_ref, ct_s, n_state)
    n = n_ref[...].T
    m = m_ref[...].T[0:N_HEADS]
    hist = [cb_ref[:, j * 512:(j + 1) * 512].T for j in range(CONV_D - 1)]
    for t in range(steps):
        hist = hist + [z_ref[t, :, 0:512].T]
        conv = cbias_ref[...]
        for jj in range(CONV_D):
            conv = conv + cw_ref[:, jj:jj + 1] * hist[jj]
        hist = hist[1:]
        act = _silu(conv)
        q = act[0:256] * D_D ** -0.5
        k = act[256:512]
        gates = zif_ref[t].T + bif_ref[...]
        ig = gates[0:N_HEADS]
        lf = _log_sigmoid(gates[N_HEADS:2 * N_HEADS])
        m_new = jnp.maximum(lf + m, ig)
        wf = _expand_heads(jnp.exp(lf + m - m_new), nb)
        kw = _expand_heads(jnp.exp(ig - m_new), nb) * k
        m = m_new
        n = wf * n + kw
        den = _per_head(q * n, lambda x, h: jnp.broadcast_to(jnp.sum(x, axis=0, keepdims=True), (64, nb)))
        q_s[t] = q
        k_s[t] = kw
        v_s[t] = z_ref[t, :, 512:768].T
        wf_s[t] = wf
        den_s[t] = den
        em_s[t] = _expand_heads(jnp.exp(-m_new), nb)
    o_s[...] = jnp.zeros(o_s.shape, F32)

    def per_key(hd, carry):
        r0 = pl.multiple_of(hd * D_D, D_D)
        e0 = pl.multiple_of((hd >> 6) * D_D, D_D)
        c = ct_s[pl.ds(r0, D_D), :]
        for t in range(steps):
            c = wf_s[t, pl.ds(hd, 1), :] * c + k_s[t, pl.ds(hd, 1), :] * v_s[t, pl.ds(e0, D_D), :]
            o_s[t, pl.ds(e0, D_D), :] += q_s[t, pl.ds(hd, 1), :] * c
        ct_s[pl.ds(r0, D_D), :] = c
        return carry

    lax.fori_loop(0, N_HEADS * D_D, per_key, 0, unroll=2)
    for t in range(steps):
        hh = o_s[t] / jnp.maximum(jnp.abs(den_s[t]), em_s[t])
        hn = _per_head(hh, lambda x, h: x * lax.rsqrt(jnp.mean(x * x, axis=0, keepdims=True) + 1e-6))
        y_ref[t] = (_sigmoid(z_ref[t, :, 768:1024].T) * (hn * ng_ref[...])).T.astype(BF16)
    _to_cols(ct_s, co_ref, n_state)
    no_ref[...] = n.T
    mo_ref[...] = jnp.concatenate([m, jnp.zeros((8 - N_HEADS, nb), F32)], axis=0)
    for j in range(CONV_D - 1):
        cbo_ref[:, j * 512:(j + 1) * 512] = hist[j].T


def _mlstm_sample(zs, cb, c, n, m_pad, cw_t, cbias_col, bif_col, ng_col):
    steps, nb, _ = zs.shape
    n_state = N_HEADS * D_D * D_D
    full = lambda shape: pl.BlockSpec(shape, lambda i: (0,) * len(shape))
    blk = pltpu.VMEM((steps, 256, nb), F32)
    return pl.pallas_call(
        functools.partial(_mlstm_sample_body, steps=steps, nb=nb),
        grid=(1,),
        in_specs=[pl.BlockSpec((steps, nb, ML_W), lambda i: (0, 0, ML_OFF // ML_W)),
                  pl.BlockSpec((steps, nb, MLIF_W), lambda i: (0, 0, MLIF_OFF // MLIF_W)),
                  _layer_spec(cb), _layer_spec(c), _layer_spec(n), _layer_spec(m_pad),
                  _layer_spec(cw_t), _layer_spec(cbias_col), _layer_spec(bif_col), _layer_spec(ng_col)],
        out_specs=[full((steps, nb, W_GROUP)), _layer_spec(c), _layer_spec(n), full((8, nb)), _layer_spec(cb)],
        out_shape=[jax.ShapeDtypeStruct((steps, nb, W_GROUP), BF16), _stacked_like(c), _stacked_like(n),
                   jax.ShapeDtypeStruct((8, nb), F32), _stacked_like(cb)],
        input_output_aliases={3: 1, 4: 2, 2: 4},
        scratch_shapes=[pltpu.VMEM((n_state, nb), F32), blk, blk, blk, blk, blk, blk, blk],
        compiler_params=_cp("arbitrary"),
        name="mlstm_sample",
    )(zs, zs, cb.arr, c.arr, n.arr, m_pad.arr, cw_t.arr, cbias_col.arr, bif_col.arr, ng_col.arr)


def _pad_cols(x, n):
    return jnp.pad(x, [(0, 0)] * (x.ndim - 1) + [(0, n - x.shape[-1])])


def _rows_at(x, row0, n_rows):
    return jnp.pad(x, ((0, 0), (row0, n_rows - row0 - x.shape[1]), (0, 0)))


def _stacked_params(w):
    gla, pool, rwkv, ml = jnp.split(w["w_in"], [784, 784 + 256, 784 + 256 + 832], axis=2)
    w_in_p = jnp.concatenate([_pad_cols(gla, GLA_W), _pad_cols(rwkv, RWKV_W), pool, ml[:, :, :1024],
                              _pad_cols(ml[:, :, 1024:], MLIF_W)], axis=2).astype(BF16)
    col = lambda v: v.reshape(DEPTH, -1, 1)
    row = lambda v: v.reshape(DEPTH, 1, -1)
    tr = lambda m: m.transpose(0, 2, 1)
    wa = _rows_at(w["gla_w_a2"], 0, 128)
    w2 = _rows_at(w["rwkv_w2"], 0, 128)
    a2 = _rows_at(w["rwkv_a2"], R_W, 128)
    g2 = _rows_at(w["rwkv_g2"], R_W + R_AA, 128)
    mu = _pad_cols(w["rwkv_mu"], RWKV_W)
    bif = _pad_cols(jnp.concatenate([w["mlstm_b_i"], w["mlstm_b_f"]], axis=1), 128)
    wbd = jnp.zeros((DEPTH, 256, 256), F32)
    for gi in range(4):
        wbd = wbd.at[:, gi * 64:(gi + 1) * 64, gi * 64:(gi + 1) * 64].set(w["pool_w"][:, gi])
    vecs = dict(w0=w["rwkv_w0"], a0=w["rwkv_a0"], k_k=w["rwkv_k_k"], k_a=w["rwkv_k_a"], r_k=w["rwkv_r_k"],
                ln_g=w["rwkv_ln_g"], ln_b=w["rwkv_ln_b"], mu=mu)
    rw = {k: row(v) for k, v in vecs.items()}
    rw.update({k + "_c": col(v) for k, v in vecs.items()})
    rw.update(w2=w2, a2=a2, g2=g2, w2_c=tr(w2), a2_c=tr(a2), g2_c=tr(g2))
    return dict(
        w_in=w_in_p, w_out=w["w_out"].astype(BF16), ln1_g=row(w["ln1_g"]), ln1_b=row(w["ln1_b"]),
        w_up=w["ffn_w_up"].astype(BF16), ffn_cw=w["ffn_conv_w"], ffn_cb=row(w["ffn_conv_b"]),
        w_down=w["ffn_w_down"].astype(BF16), ln2_g=row(w["ln2_g"]), ln2_b=row(w["ln2_b"]),
        gla_wa=wa, gla_wa_t=tr(wa), gla_ba=row(w["gla_b_a"]), gla_ba_c=col(w["gla_b_a"]),
        gla_ng=row(w["gla_norm_g"]), gla_ng_c=col(w["gla_norm_g"]),
        pool_w=wbd, pool_scale=row(w["pool_scale"]), rwkv=rw,
        ml_cw=w["mlstm_conv_w"], ml_cw_t=tr(w["mlstm_conv_w"]), ml_cb=row(w["mlstm_conv_b"]),
        ml_cb_c=col(w["mlstm_conv_b"]), ml_bif=row(bif), ml_bif_c=col(bif),
        ml_ng=row(w["mlstm_norm_g"]), ml_ng_c=col(w["mlstm_norm_g"]))


def _layer_view(stacked, l):
    return {k: (_layer_view(v, l) if isinstance(v, dict) else _Layer(v, l)) for k, v in stacked.items()}


RWKV_BLOCK = 512
GLA_BLOCK = 512
MLSTM_BLOCK = 512
POOL_BLOCK = 1024
DENSE_TILE = 1024
FFN_TILE = 512


def _prompt_layer(x, p):
    bsz, seq, d = x.shape
    z = _linear(x.reshape(bsz * seq, d), p["w_in"], DENSE_TILE).reshape(bsz, seq, ZC)
    zeros = lambda *s: jnp.zeros((bsz,) + s, F32)
    y_a, s_gla = _gla_prompt(z, zeros(N_HEADS, DK_A, DV_A), p["gla_wa"], p["gla_ba"], p["gla_ng"], GLA_BLOCK)
    y_b, s_pool = _pool_prompt(z, zeros(POOL_BUF, W_GROUP), p["pool_w"], p["pool_scale"], POOL_BLOCK, 0)
    y_c, s_rwkv, s_shift = _rwkv_prompt(z, zeros(1, RWKV_W), zeros(N_HEADS, N_C, N_C), p["rwkv"], RWKV_BLOCK)
    y_d, s_c, s_n, s_m, s_conv = _mlstm_prompt(z, zeros(CONV_D - 1, 512), zeros(N_HEADS, D_D, D_D), zeros(1, 256),
                                               zeros(1, 256), p["ml_cw"], p["ml_cb"], p["ml_bif"], p["ml_ng"],
                                               MLSTM_BLOCK)
    x2, s_ffn = _ffn_prompt([y_a, y_b, y_c, y_d], x, zeros(FFN_CONV - 1, 2 * D_FF), p, FFN_TILE)
    states = (s_gla, s_pool, s_rwkv, s_shift[:, :, :RWKV_COLS], s_c, s_n.reshape(bsz, N_HEADS, D_D),
              s_m[:, 0, ::D_D], s_conv, s_ffn)
    return x2, states


def _sample_layer(x, st, p, steps, nb):
    s_gla, s_pool, s_rwkv, s_shift, s_c, s_n, s_m, s_conv, s_ffn = st
    z = _linear(x, p["w_in"], steps * nb).reshape(steps, nb, ZC)
    y_a, n_gla = _gla_sample(z, s_gla, p["gla_wa_t"], p["gla_ba_c"], p["gla_ng_c"])
    y_b, n_pool = _pool_sample(z, s_pool, p["pool_w"], p["pool_scale"], PAST_LEN)
    y_c, n_rwkv, n_shift = _rwkv_sample(z, s_shift, s_rwkv, p["rwkv"])
    y_d, n_c, n_n, n_m, n_conv = _mlstm_sample(z, s_conv, s_c, s_n, s_m, p["ml_cw_t"], p["ml_cb_c"], p["ml_bif_c"],
                                               p["ml_ng_c"])
    flat = lambda y: y.reshape(steps * nb, W_GROUP)
    x2, n_ffn = _ffn_sample([flat(y_a), flat(y_b), flat(y_c), flat(y_d)], x, s_ffn, p, nb, steps)
    return x2, (n_gla, n_pool, n_rwkv, n_shift, n_c, n_n, n_m, n_conv, n_ffn)


def kernel(x_prompt, x_sample, state_gla, state_pool, state_rwkv, state_rwkv_shift, state_mlstm_c, state_mlstm_n, state_mlstm_m, state_mlstm_conv, state_ffn_conv, w_in, gla_w_a2, gla_b_a, gla_norm_g, pool_w, pool_scale, rwkv_mu, rwkv_w0, rwkv_w2, rwkv_a0, rwkv_a2, rwkv_g2, rwkv_k_k, rwkv_k_a, rwkv_r_k, rwkv_ln_g, rwkv_ln_b, mlstm_conv_w, mlstm_conv_b, mlstm_b_i, mlstm_b_f, mlstm_norm_g, w_out, ln1_g, ln1_b, ffn_w_up, ffn_conv_w, ffn_conv_b, ffn_w_down, ln2_g, ln2_b):
    w = dict(w_in=w_in, gla_w_a2=gla_w_a2, gla_b_a=gla_b_a, gla_norm_g=gla_norm_g, pool_w=pool_w,
             pool_scale=pool_scale, rwkv_mu=rwkv_mu, rwkv_w0=rwkv_w0, rwkv_w2=rwkv_w2, rwkv_a0=rwkv_a0,
             rwkv_a2=rwkv_a2, rwkv_g2=rwkv_g2, rwkv_k_k=rwkv_k_k, rwkv_k_a=rwkv_k_a, rwkv_r_k=rwkv_r_k,
             rwkv_ln_g=rwkv_ln_g, rwkv_ln_b=rwkv_ln_b, mlstm_conv_w=mlstm_conv_w, mlstm_conv_b=mlstm_conv_b,
             mlstm_b_i=mlstm_b_i, mlstm_b_f=mlstm_b_f, mlstm_norm_g=mlstm_norm_g, w_out=w_out, ln1_g=ln1_g,
             ln1_b=ln1_b, ffn_w_up=ffn_w_up, ffn_conv_w=ffn_conv_w, ffn_conv_b=ffn_conv_b, ffn_w_down=ffn_w_down,
             ln2_g=ln2_g, ln2_b=ln2_b)
    sample_states = (state_gla, state_pool, state_rwkv, state_rwkv_shift, state_mlstm_c, state_mlstm_n,
                     state_mlstm_m, state_mlstm_conv, state_ffn_conv)
    nb, steps, d = x_sample.shape
    flat_in = [s.reshape(DEPTH, nb, -1) for s in sample_states]
    flat_in[8] = state_ffn_conv
    flat_in[3] = _pad_cols(flat_in[3], RWKV_W)
    flat_in[6] = _pad_cols(flat_in[6], 128)
    stacked = _stacked_params(w)
    yp = x_prompt
    ys = x_sample.transpose(1, 0, 2).reshape(steps * nb, d)
    acc_p = [[] for _ in sample_states]
    m_new = []
    flat = list(flat_in)
    for l in range(DEPTH):
        p = _layer_view(stacked, l)
        yp, st_p = _prompt_layer(yp, p)
        ys, st_s = _sample_layer(ys, tuple(_Layer(s, l) for s in flat), p, steps, nb)
        for i in range(len(sample_states)):
            acc_p[i].append(st_p[i])
            if i == 6:
                m_new.append(st_s[i])
            else:
                flat[i] = st_s[i]
    ys = ys.reshape(steps, nb, d).transpose(1, 0, 2)
    out_s = list(flat)
    out_s[3] = out_s[3][:, :, :RWKV_COLS]
    out_s[6] = jnp.stack(m_new)[:, 0:N_HEADS, :].transpose(0, 2, 1)
    out = [yp, ys]
    for sp, ss, ref in zip(acc_p, out_s, sample_states):
        out.append(jnp.stack(sp))
        out.append(ss.reshape(ref.shape))
    return tuple(out)
```

```python
import functools
import math

import jax
import jax.numpy as jnp
from jax import lax
from jax.experimental import pallas as pl
from jax.experimental.pallas import tpu as pltpu

F32 = jnp.float32
BF16 = jnp.bfloat16
HI = lax.Precision.HIGHEST

D_MODEL = 1024
DEPTH = 4
PAST_LEN = 16384
W_GROUP = 256
N_HEADS = 4
DK_A = 32
DV_A = 64
R_GLA = 16
GLA_TAU = 16.0
POOL_WINDOWS = (2, 4, 8, 16)
POOL_BUF = 15
N_C = 64
R_W, R_AA, R_G = 16, 16, 32
RWKV_COLS = 832
RWKV_LN_EPS = 64e-5
D_D = 64
CONV_D = 4
D_FF = 2816
FFN_CONV = 3
ALPHA = (2 * DEPTH) ** 0.25
CHUNK = 64

GLA_W = 896
RWKV_W = 896
GLA_OFF, RWKV_OFF, POOL_OFF, ML_OFF, MLIF_OFF = 0, 896, 1792, 2048, 3072
ZC = 3200
ML_W = 1024
MLIF_W = 128

VMEM_LIMIT = 56 * 1024 * 1024


def _cp(*sem):
    return pltpu.CompilerParams(dimension_semantics=sem, vmem_limit_bytes=VMEM_LIMIT)


def _mm(a, b):
    return jnp.dot(a.astype(BF16), b.astype(BF16), preferred_element_type=F32)


def _mm_nt(a, b):
    return lax.dot_general(a.astype(BF16), b.astype(BF16), (((1,), (1,)), ((), ())), preferred_element_type=F32)


def _mm_tn(a, b):
    return lax.dot_general(a.astype(BF16), b.astype(BF16), (((0,), (0,)), ((), ())), preferred_element_type=F32)


def _mm_hi(a, b):
    return jnp.dot(a, b, preferred_element_type=F32, precision=HI)


def _split2(x):
    hi = x.astype(BF16)
    return hi, (x - hi.astype(F32)).astype(BF16)


def _mm_xc(x, c2):
    hi, mid = _split2(x)
    return jnp.dot(jnp.concatenate([hi, mid], axis=1), c2, preferred_element_type=F32)


def _mm_cx(c2, x):
    hi, mid = _split2(x)
    return jnp.dot(c2, jnp.concatenate([hi, mid], axis=0), preferred_element_type=F32)


def _twice(c, axis):
    c = c.astype(F32).astype(BF16)
    return jnp.concatenate([c, c], axis=axis)


def _iota(shape, dim):
    return lax.broadcasted_iota(jnp.int32, shape, dim)


def _sigmoid(x):
    return 1.0 / (1.0 + jnp.exp(-x))


def _silu(x):
    return x * _sigmoid(x)


def _log_sigmoid(x):
    return jnp.minimum(x, 0.0) - jnp.log(1.0 + jnp.exp(-jnp.abs(x)))


def _tile4(x):
    return jnp.concatenate([x, x, x, x], axis=0)


def _block_diag_mask(rows, cols, rshift, cshift):
    return (_iota((rows, cols), 0) >> rshift) == (_iota((rows, cols), 1) >> cshift)


def _layer_norm(h, g, b, eps):
    mu = jnp.mean(h, axis=-1, keepdims=True)
    xc = h - mu
    var = jnp.mean(xc * xc, axis=-1, keepdims=True)
    return xc * lax.rsqrt(var + eps) * g + b


class _Layer:
    def __init__(self, arr, l):
        self.arr, self.l, self.shape = arr, l, tuple(arr.shape[1:])


def _layer_spec(p, single_buffer=False):
    n, l = len(p.shape), p.l
    if single_buffer:
        return pl.BlockSpec((None,) + p.shape, lambda *g: (l,) + (0,) * n, pipeline_mode=pl.Buffered(1))
    return pl.BlockSpec((None,) + p.shape, lambda *g: (l,) + (0,) * n)


def _stacked_like(p):
    return jax.ShapeDtypeStruct(p.arr.shape, p.arr.dtype)


def _linear_body(x_ref, w_ref, o_ref):
    o_ref[...] = jnp.dot(x_ref[...].astype(BF16), w_ref[...], preferred_element_type=F32)


def _linear(x, w, tm):
    t, k = x.shape
    n = w.shape[1]
    return pl.pallas_call(
        _linear_body,
        grid=(t // tm,),
        in_specs=[pl.BlockSpec((tm, k), lambda i: (i, 0)), _layer_spec(w, single_buffer=True)],
        out_specs=pl.BlockSpec((tm, n), lambda i: (i, 0)),
        out_shape=jax.ShapeDtypeStruct((t, n), F32),
        compiler_params=_cp("arbitrary"),
        name="in_proj",
    )(x, w.arr)


def _outproj(ys, x, w_ref, g_ref, b_ref):
    y = jnp.concatenate(ys, axis=1).astype(BF16)
    mixed = jnp.dot(y, w_ref[...], preferred_element_type=F32)
    return _layer_norm(ALPHA * x + mixed, g_ref[...], b_ref[...], 1e-5)


FF_CHUNK = 256
N_FF_CHUNKS = D_FF // FF_CHUNK
UP_AHEAD = 2
DOWN_ROWS = 256


def _ffn_prompt_body(ya_ref, yb_ref, yc_ref, yd_ref, x_ref, buf_ref, wo_ref, g1_ref, b1_ref,
                     wup_ref, cw_ref, cb_ref, wdn_ref, g_ref, b_ref,
                     o_ref, st_ref, carry_ref, h_ref, x1_ref, *, tm, n_tiles):
    j = pl.program_id(1)

    @pl.when(j == 0)
    def _():
        carry_ref[...] = buf_ref[0]

    x = _outproj([ya_ref[0], yb_ref[0], yc_ref[0], yd_ref[0]], x_ref[0], wo_ref, g1_ref, b1_ref)
    x1_ref[...] = x
    xb = x.astype(BF16)
    row = _iota((8, FF_CHUNK), 0)
    col = lambda c, half: slice(half * D_FF + c * FF_CHUNK, half * D_FF + (c + 1) * FF_CHUNK)
    up = lambda c: [jnp.dot(xb, wup_ref[:, col(c, half)], preferred_element_type=F32) for half in range(2)]
    ahead = [up(c) for c in range(UP_AHEAD)]
    for c in range(N_FF_CHUNKS):
        if c + UP_AHEAD < N_FF_CHUNKS:
            ahead.append(up(c + UP_AHEAD))
        u_now = ahead.pop(0)
        acts = []
        for half in range(2):
            sl = col(c, half)
            u = u_now[half]
            two_back, one_back = carry_ref[0:1, sl], carry_ref[1:2, sl]
            w0, w1, w2 = cw_ref[0:1, sl], cw_ref[1:2, sl], cw_ref[2:3, sl]
            r1 = pltpu.roll(u, 1, 0)
            r2 = pltpu.roll(u, 2, 0)
            r1_head = jnp.where(row == 0, one_back, r1[0:8])
            r2_head = jnp.where(row == 0, two_back, jnp.where(row == 1, one_back, r2[0:8]))
            head = cb_ref[:, sl] + w0 * r2_head + w1 * r1_head + w2 * u[0:8]
            body = cb_ref[:, sl] + w0 * r2 + w1 * r1 + w2 * u
            carry_ref[:, sl] = u[tm - 2:tm]
            acts.append(jnp.concatenate([head, body[8:]], axis=0))
        h_ref[:, c * FF_CHUNK:(c + 1) * FF_CHUNK] = (_silu(acts[0]) * acts[1]).astype(BF16)
    for r0 in range(0, tm, DOWN_ROWS):
        rows = slice(r0, r0 + DOWN_ROWS)
        f = jnp.dot(h_ref[rows, :], wdn_ref[...], preferred_element_type=F32)
        o_ref[0, rows, :] = _layer_norm(ALPHA * x1_ref[rows, :] + f, g_ref[...], b_ref[...], 1e-5)

    @pl.when(j == n_tiles - 1)
    def _():
        st_ref[0] = carry_ref[...]


_DENSE_PARAMS = ("w_out", "ln1_g", "ln1_b", "w_up", "ffn_cw", "ffn_cb", "w_down", "ln2_g", "ln2_b")


def _ffn_prompt(ys, x3, buf, p, tm):
    bsz, seq, d = x3.shape
    n_tiles = seq // tm
    params = [p[k] for k in _DENSE_PARAMS]
    tok = lambda w: pl.BlockSpec((1, tm, w), lambda i, j: (i, j, 0))
    st = pl.BlockSpec((1, 2, 2 * D_FF), lambda i, j: (i, 0, 0))
    return pl.pallas_call(
        functools.partial(_ffn_prompt_body, tm=tm, n_tiles=n_tiles),
        grid=(bsz, n_tiles),
        in_specs=[tok(W_GROUP)] * 4 + [tok(d), st] + [_layer_spec(q, single_buffer=True) for q in params],
        out_specs=[tok(d), st],
        out_shape=[jax.ShapeDtypeStruct((bsz, seq, d), F32), jax.ShapeDtypeStruct((bsz, 2, 2 * D_FF), F32)],
        scratch_shapes=[pltpu.VMEM((2, 2 * D_FF), F32), pltpu.VMEM((tm, D_FF), BF16), pltpu.VMEM((tm, d), F32)],
        compiler_params=_cp("arbitrary", "arbitrary"),
        name="ffn_prompt",
    )(*ys, x3, buf, *[q.arr for q in params])


def _ffn_sample_body(ya_ref, yb_ref, yc_ref, yd_ref, x_ref, buf_ref, wo_ref, g1_ref, b1_ref,
                     wup_ref, cw_ref, cb_ref, wdn_ref, g_ref, b_ref,
                     o_ref, st_ref, h_ref, *, nb, steps):
    t = nb * steps
    x = _outproj([ya_ref[...], yb_ref[...], yc_ref[...], yd_ref[...]], x_ref[...], wo_ref, g1_ref, b1_ref)
    xb = x.astype(BF16)
    for c in range(N_FF_CHUNKS):
        acts = []
        for half in range(2):
            lo = half * D_FF + c * FF_CHUNK
            sl = slice(lo, lo + FF_CHUNK)
            u = jnp.dot(xb, wup_ref[:, sl], preferred_element_type=F32)
            b0 = buf_ref[:, 0, sl]
            b1 = buf_ref[:, 1, sl]
            back1 = jnp.concatenate([b1, u[0:t - nb]], axis=0)
            back2 = jnp.concatenate([b0, b1, u[0:t - 2 * nb]], axis=0)
            conv = cb_ref[:, sl] + cw_ref[0:1, sl] * back2 + cw_ref[1:2, sl] * back1 + cw_ref[2:3, sl] * u
            st_ref[:, 0, sl] = u[t - 2 * nb:t - nb]
            st_ref[:, 1, sl] = u[t - nb:t]
            acts.append(conv)
        h_ref[:, c * FF_CHUNK:(c + 1) * FF_CHUNK] = (_silu(acts[0]) * acts[1]).astype(BF16)
    f = jnp.dot(h_ref[...], wdn_ref[...], preferred_element_type=F32)
    o_ref[...] = _layer_norm(ALPHA * x + f, g_ref[...], b_ref[...], 1e-5)


def _ffn_sample(ys, x, buf, p, nb, steps):
    t, d = x.shape
    params = [p[k] for k in _DENSE_PARAMS]
    full = lambda shape: pl.BlockSpec(shape, lambda i: (0,) * len(shape))
    return pl.pallas_call(
        functools.partial(_ffn_sample_body, nb=nb, steps=steps),
        grid=(1,),
        in_specs=[full((t, W_GROUP))] * 4 + [full((t, d)), _layer_spec(buf)] + [_layer_spec(q) for q in params],
        out_specs=[full((t, d)), _layer_spec(buf)],
        out_shape=[jax.ShapeDtypeStruct((t, d), F32), _stacked_like(buf)],
        input_output_aliases={5: 1},
        scratch_shapes=[pltpu.VMEM((t, D_FF), BF16)],
        compiler_params=_cp("arbitrary"),
        name="ffn_sample",
    )(*ys, x, buf.arr, *[q.arr for q in params])


def _exp_masks():
    t = _iota((CHUNK, 256), 0)
    s = _iota((CHUNK, 256), 1) & (CHUNK - 1)
    return t >= s, t > s, t == s


def _bd256():
    return _block_diag_mask(256, 256, 6, 6)


def _tril_blocks(n):
    r = _iota((n, n), 0)
    c = _iota((n, n), 1)
    return ((r >= c) & ((r >> 6) == (c >> 6))).astype(F32)


def _mask_bf16(mask):
    return mask.astype(F32).astype(BF16)


def _bdx(x, mask01):
    return _tile4(x.astype(BF16)) * mask01


def _put_block_diag(dst_ref, blocks, rows, cols):
    dst_ref[...] = jnp.zeros(dst_ref.shape, F32)
    for h in range(N_HEADS):
        dst_ref[h * rows:(h + 1) * rows, h * cols:(h + 1) * cols] = blocks[h]


def _get_block_diag(src_ref, out_ref, rows, cols):
    for h in range(N_HEADS):
        out_ref[0, h] = src_ref[h * rows:(h + 1) * rows, h * cols:(h + 1) * cols]


def _gla_prompt_body(z_ref, s0_ref, wa_ref, ba_ref, ng_ref, y_ref, so_ref, sbd_ref, *, n_chunks, n_blocks):
    j = pl.program_id(1)

    @pl.when(j == 0)
    def _():
        _put_block_diag(sbd_ref, s0_ref[0], DK_A, DV_A)

    lb = n_chunks * CHUNK
    causal, _, _ = _exp_masks()
    bd_state = _block_diag_mask(128, 256, 5, 6)
    bd_k = _mask_bf16(_block_diag_mask(256, 128, 6, 5))
    bd_v = _mask_bf16(_bd256())
    pavg2 = _twice(bd_v.astype(F32) * (1.0 / DV_A), 0)

    lg = _log_sigmoid(_mm(z_ref[0, :, 768:896], wa_ref[...]) + ba_ref[...]) * (1.0 / GLA_TAU)
    bc_all = _mm_cx(_twice(_tril_blocks(lb), 1), lg)

    cs = range(n_chunks)
    sl = [slice(c * CHUNK, (c + 1) * CHUNK) for c in cs]
    q = [z_ref[0, sl[c], 0:128] * DK_A ** -0.5 for c in cs]
    bc = [bc_all[sl[c]] for c in cs]
    bl = [bc[c][CHUNK - 1:CHUNK] for c in cs]
    rho = [bc[c][CHUNK // 2 - 1:CHUNK // 2] for c in cs]
    att = [_mm_nt(q[c] * jnp.exp(bc[c] - rho[c]), _bdx(z_ref[0, sl[c], 128:256] * jnp.exp(rho[c] - bc[c]), bd_k))
           for c in cs]
    incr = [jnp.where(bd_state, _mm_tn(z_ref[0, sl[c], 128:256] * jnp.exp(bl[c] - bc[c]), z_ref[0, sl[c], 256:512]),
                      0.0) for c in cs]
    o_intra = [_mm(jnp.where(causal, att[c], 0.0), _bdx(z_ref[0, sl[c], 256:512], bd_v)) for c in cs]
    q_dec = [q[c] * jnp.exp(bc[c]) for c in cs]
    decay = []
    for c in cs:
        decay_col = jnp.broadcast_to(jnp.exp(bl[c]), (128, 128)).T
        decay.append(jnp.concatenate([decay_col, decay_col], axis=1))
    starts = [sbd_ref[...]]
    for c in range(n_chunks):
        starts.append(decay[c] * starts[c] + incr[c])
    sbd_ref[...] = starts[n_chunks]
    o = jnp.concatenate([_mm(q_dec[c], starts[c]) + o_intra[c] for c in range(n_chunks)], axis=0)
    ms = _mm_xc(o * o, pavg2)
    y_ref[0] = (o * lax.rsqrt(ms + 1e-6) * ng_ref[...] * _silu(z_ref[0, :, 512:768])).astype(BF16)

    @pl.when(j == n_blocks - 1)
    def _():
        _get_block_diag(sbd_ref, so_ref, DK_A, DV_A)


def _gla_prompt(z3, s0, wa, ba, ng, lb):
    bsz, seq, _ = z3.shape
    n_blocks = seq // lb
    st = pl.BlockSpec((1, N_HEADS, DK_A, DV_A), lambda i, j: (i, 0, 0, 0))
    return pl.pallas_call(
        functools.partial(_gla_prompt_body, n_chunks=lb // CHUNK, n_blocks=n_blocks),
        grid=(bsz, n_blocks),
        in_specs=[pl.BlockSpec((1, lb, GLA_W), lambda i, j: (i, j, GLA_OFF // GLA_W)), st,
                  _layer_spec(wa), _layer_spec(ba), _layer_spec(ng)],
        out_specs=[pl.BlockSpec((1, lb, W_GROUP), lambda i, j: (i, j, 0)), st],
        out_shape=[jax.ShapeDtypeStruct((bsz, seq, W_GROUP), BF16),
                   jax.ShapeDtypeStruct((bsz, N_HEADS, DK_A, DV_A), F32)],
        scratch_shapes=[pltpu.VMEM((128, 256), F32)],
        compiler_params=_cp("arbitrary", "arbitrary"),
        name="gla_prompt",
    )(z3, s0, wa.arr, ba.arr, ng.arr)


def _pool_prompt_body(z_ref, buf_ref, w_ref, sc_ref, y_ref, bo_ref, hist_ref, *, lb, n_blocks, start_pos):
    j = pl.program_id(1)

    @pl.when(j == 0)
    def _():
        hist_ref[0:1, :] = jnp.zeros((1, W_GROUP), F32)
        hist_ref[1:16, :] = buf_ref[0]

    zp = z_ref[0]
    e = jnp.concatenate([hist_ref[...], zp], axis=0)
    s2 = e + pltpu.roll(e, 1, 0)
    s4 = s2 + pltpu.roll(s2, 2, 0)
    s8 = s4 + pltpu.roll(s4, 4, 0)
    s16 = s8 + pltpu.roll(s8, 8, 0)
    lane = _iota((lb, W_GROUP), 1)
    win = jnp.where(lane < 64, s2[16:], jnp.where(lane < 128, s4[16:], jnp.where(lane < 192, s8[16:], s16[16:])))
    by_lane = lambda ln, vals: jnp.where(ln < 64, vals[0], jnp.where(ln < 128, vals[1],
                                                                      jnp.where(ln < 192, vals[2], vals[3])))
    pos = start_pos + j * lb + _iota((16, W_GROUP), 0)
    cnt_head = jnp.minimum(pos + 1, by_lane(_iota((16, W_GROUP), 1), POOL_WINDOWS)).astype(F32)
    inv_width = by_lane(_iota((1, W_GROUP), 1), [1.0 / w for w in POOL_WINDOWS])
    mean = jnp.concatenate([win[0:16] / cnt_head, win[16:] * inv_width], axis=0)
    y_ref[0] = (_mm(mean - zp, w_ref[...]) * sc_ref[...]).astype(BF16)
    hist_ref[1:16, :] = e[lb + 1:lb + 16]

    @pl.when(j == n_blocks - 1)
    def _():
        bo_ref[0] = hist_ref[1:16, :]


def _pool_prompt(z3, buf, wbd, scale, lb, start_pos):
    bsz, seq, _ = z3.shape
    n_blocks = seq // lb
    st = pl.BlockSpec((1, POOL_BUF, W_GROUP), lambda i, j: (i, 0, 0))
    return pl.pallas_call(
        functools.partial(_pool_prompt_body, lb=lb, n_blocks=n_blocks, start_pos=start_pos),
        grid=(bsz, n_blocks),
        in_specs=[pl.BlockSpec((1, lb, W_GROUP), lambda i, j: (i, j, POOL_OFF // W_GROUP)), st,
                  _layer_spec(wbd), _layer_spec(scale)],
        out_specs=[pl.BlockSpec((1, lb, W_GROUP), lambda i, j: (i, j, 0)), st],
        out_shape=[jax.ShapeDtypeStruct((bsz, seq, W_GROUP), BF16),
                   jax.ShapeDtypeStruct((bsz, POOL_BUF, W_GROUP), F32)],
        scratch_shapes=[pltpu.VMEM((16, W_GROUP), F32)],
        compiler_params=_cp("arbitrary", "arbitrary"),
        name="pool_prompt",
    )(z3, buf, wbd.arr, scale.arr)


def _rwkv_prompt_body(z_ref, sh0_ref, s0_ref, mu_ref, w0_ref, w2_ref, a0_ref, a2_ref, g2_ref, kk_ref, ka_ref,
                      rk_ref, lg_ref, lb_ref, y_ref, so_ref, sho_ref,
                      nbd_ref, carry_ref, r_s, k_s, v_s, kk_s, b_s, lw_s, g_s, *, n_chunks, n_blocks, lb):
    j = pl.program_id(1)

    @pl.when(j == 0)
    def _():
        _put_block_diag(nbd_ref, s0_ref[0], N_C, N_C)
        carry_ref[...] = sh0_ref[0]

    bd = _bd256()
    bd_m = _mask_bf16(bd)
    ones2 = _twice(bd, 0)
    pavg2 = _twice(bd.astype(F32) * (1.0 / N_C), 0)

    f = z_ref[0]
    prev = jnp.concatenate([carry_ref[...], f[:-1]], axis=0)
    mixed = f + (prev - f) * mu_ref[...]
    carry_ref[...] = f[lb - 1:lb]
    r = mixed[:, 0:256]
    k = mixed[:, 256:512]
    v = mixed[:, 512:768]
    low = mixed[:, 768:896]
    lw_all = -_sigmoid(w0_ref[...] + _mm(jnp.tanh(low), w2_ref[...])) * math.exp(-0.5)
    lw_s[...] = lw_all
    a = _sigmoid(a0_ref[...] + _mm(low, a2_ref[...]))
    g_s[...] = _mm(_sigmoid(low), g2_ref[...])
    kk = k * kk_ref[...]
    kk = kk * lax.rsqrt(jnp.maximum(_mm_xc(kk * kk, ones2), 1e-24))
    r_s[...] = r
    v_s[...] = v
    kk_s[...] = kk
    k_s[...] = k * (1.0 + (a - 1.0) * ka_ref[...])
    b_s[...] = kk * a
    gc_all = _mm_cx(_twice(_tril_blocks(lb), 1), lw_all)

    lower, strict, eye_exp = _exp_masks()
    eye_f = eye_exp.astype(F32)

    cs = range(n_chunks)
    sl = [slice(c * CHUNK, (c + 1) * CHUNK) for c in cs]
    gc = [gc_all[s] for s in sl]
    kt = [kk_s[sl[c], :] * jnp.exp(gc[c] - lw_s[sl[c], :]) for c in cs]
    rt = [r_s[sl[c], :] * jnp.exp(gc[c]) for c in cs]
    a_kk, a_kb, a_rk, a_rb = [], [], [], []
    for c in cs:
        einv = jnp.exp(-gc[c])
        lhs = jnp.concatenate([kt[c], rt[c]], axis=0)
        rhs = jnp.concatenate([_bdx(k_s[sl[c], :] * einv, bd_m), _bdx(b_s[sl[c], :] * einv, bd_m)], axis=0)
        aa = _mm_nt(lhs, rhs)
        a_kk.append(jnp.where(strict, aa[0:CHUNK, 0:256], 0.0))
        a_kb.append(jnp.where(strict, aa[0:CHUNK, 256:512], 0.0))
        a_rk.append(jnp.where(lower, aa[CHUNK:, 0:256], 0.0))
        a_rb.append(jnp.where(lower, aa[CHUNK:, 256:512], 0.0))
    t = [eye_f - a_kb[c] for c in cs]
    p = [_mm(a_kb[c], _bdx(a_kb[c], bd_m)) for c in cs]
    for _ in range(4):
        m = [_mm(jnp.concatenate([p[c], t[c]], axis=0), _bdx(p[c], bd_m)) for c in cs]
        p = [m[c][0:CHUNK] for c in cs]
        t = [t[c] + m[c][CHUNK:] for c in cs]
    t = [t[c] + _mm(t[c], _bdx(p[c], bd_m)) for c in cs]
    av = [_mm(jnp.concatenate([a_kk[c], a_rk[c]], axis=0), _bdx(v_s[sl[c], :], bd_m)) for c in cs]
    tu = [_mm(t[c], jnp.concatenate([_bdx(av[c][0:CHUNK], bd_m), _bdx(kt[c], bd_m)], axis=1)) for c in cs]
    u0 = [tu[c][:, 0:256] for c in cs]
    tk = [tu[c][:, 256:512] for c in cs]
    ro = [_mm(a_rb[c], jnp.concatenate([_bdx(tk[c], bd_m), _bdx(u0[c], bd_m)], axis=1)) for c in cs]
    r_eff = [rt[c] - ro[c][:, 0:256] for c in cs]
    o0 = [av[c][CHUNK:] - ro[c][:, 256:512] for c in cs]
    gl = [gc[c][CHUNK - 1:CHUNK] for c in cs]
    dk = [jnp.exp(gl[c] - gc[c]) for c in cs]
    bdk = [b_s[sl[c], :] * dk[c] for c in cs]
    a_st = [jnp.where(bd, -_mm_tn(tk[c], bdk[c]), 0.0) for c in cs]
    b_st = [jnp.where(bd, _mm_tn(jnp.concatenate([v_s[sl[c], :], -u0[c]], axis=0),
                                 jnp.concatenate([k_s[sl[c], :] * dk[c], bdk[c]], axis=0)), 0.0)
            for c in cs]
    n = nbd_ref[...]
    o = []
    for c in cs:
        o.append(_mm_nt(r_eff[c], n) + o0[c])
        n = n * jnp.exp(gl[c]) + _mm(n, a_st[c]) + b_st[c]
    nbd_ref[...] = n
    o = jnp.concatenate(o, axis=0)
    mu = _mm_xc(o, pavg2)
    xc = o - mu
    var = _mm_xc(xc * xc, pavg2)
    on = xc * lax.rsqrt(var + RWKV_LN_EPS) * lg_ref[...] + lb_ref[...]
    bonus = _mm_xc(r_s[...] * k_s[...] * rk_ref[...], ones2) * v_s[...]
    y_ref[0] = ((on + bonus) * g_s[...]).astype(BF16)

    @pl.when(j == n_blocks - 1)
    def _():
        _get_block_diag(nbd_ref, so_ref, N_C, N_C)
        sho_ref[0] = carry_ref[...]


_RWKV_PARAMS = ("mu", "w0", "w2", "a0", "a2", "g2", "k_k", "k_a", "r_k", "ln_g", "ln_b")


def _rwkv_prompt(z3, sh0, s0, p, lb):
    bsz, seq, _ = z3.shape
    n_blocks = seq // lb
    st = pl.BlockSpec((1, N_HEADS, N_C, N_C), lambda i, j: (i, 0, 0, 0))
    sh = pl.BlockSpec((1, 1, RWKV_W), lambda i, j: (i, 0, 0))
    blk = pltpu.VMEM((lb, 256), F32)
    params = [p[k] for k in _RWKV_PARAMS]
    return pl.pallas_call(
        functools.partial(_rwkv_prompt_body, n_chunks=lb // CHUNK, n_blocks=n_blocks, lb=lb),
        grid=(bsz, n_blocks),
        in_specs=[pl.BlockSpec((1, lb, RWKV_W), lambda i, j: (i, j, RWKV_OFF // RWKV_W)), sh, st]
        + [_layer_spec(q) for q in params],
        out_specs=[pl.BlockSpec((1, lb, W_GROUP), lambda i, j: (i, j, 0)), st, sh],
        out_shape=[jax.ShapeDtypeStruct((bsz, seq, W_GROUP), BF16),
                   jax.ShapeDtypeStruct((bsz, N_HEADS, N_C, N_C), F32),
                   jax.ShapeDtypeStruct((bsz, 1, RWKV_W), F32)],
        scratch_shapes=[pltpu.VMEM((256, 256), F32), pltpu.VMEM((1, RWKV_W), F32),
                        blk, blk, blk, blk, blk, blk, blk],
        compiler_params=_cp("arbitrary", "arbitrary"),
        name="rwkv_prompt",
    )(z3, sh0, s0, *[q.arr for q in params])


def _mlstm_prompt_body(z_ref, zif_ref, cb0_ref, c0_ref, n0_ref, m0_ref, cw_ref, cbias_ref, bif_ref, ng_ref,
                       y_ref, co_ref, no_ref, mo_ref, cbo_ref,
                       cbd_ref, n_ref, m_ref, carry_ref, qk_s, *, n_chunks, n_blocks, lb):
    j = pl.program_id(1)

    @pl.when(j == 0)
    def _():
        _put_block_diag(cbd_ref, c0_ref[0], D_D, D_D)
        n_ref[...] = n0_ref[0]
        m_ref[...] = m0_ref[0]
        carry_ref[...] = cb0_ref[0]

    qk = z_ref[0, :, 0:512]
    conv = cbias_ref[...] + cw_ref[CONV_D - 1:CONV_D, :] * qk
    head = cbias_ref[...] + cw_ref[CONV_D - 1:CONV_D, :] * qk[0:8]
    row = _iota((8, 512), 0)
    for back in range(1, CONV_D):
        w = cw_ref[CONV_D - 1 - back:CONV_D - back, :]
        rolled = pltpu.roll(qk, back, 0)
        conv = conv + w * rolled
        fixed = rolled[0:8]
        for r in range(back):
            fixed = jnp.where(row == r, carry_ref[CONV_D - 1 - back + r:CONV_D - back + r, :], fixed)
        head = head + w * fixed
    qk_s[...] = _silu(jnp.concatenate([head, conv[8:]], axis=0))
    carry_ref[...] = qk[lb - (CONV_D - 1):lb]

    bd = _bd256()
    bd_m = _mask_bf16(bd)
    ones2 = _twice(bd, 0)
    pavg2 = _twice(bd.astype(F32) * (1.0 / D_D), 0)
    causal, _, eye_exp = _exp_masks()
    er = _iota((128, 512), 0)
    ec = _iota((128, 512), 1)
    expand2 = _twice(((ec < 256) & (er == (ec >> 6))) | ((ec >= 256) & (er == 4 + ((ec - 256) >> 6))), 0)
    is_f = (_iota((lb, 128), 1) >= 4) & (_iota((lb, 128), 1) < 8)
    lane = _iota((CHUNK, 256), 1)

    gates = zif_ref[0] + bif_ref[...]
    gates = jnp.where(is_f, _log_sigmoid(gates), gates)
    gates = jnp.where(is_f, _mm_cx(_twice(_tril_blocks(lb), 1), gates), gates)
    ge_all = _mm_xc(gates, expand2)

    cs = range(n_chunks)
    sl = [slice(c * CHUNK, (c + 1) * CHUNK) for c in cs]
    q = [qk_s[sl[c], 0:256] * D_D ** -0.5 for c in cs]
    k = [qk_s[sl[c], 256:512] for c in cs]
    iexp = [ge_all[sl[c], 0:256] for c in cs]
    fexp = [ge_all[sl[c], 256:512] for c in cs]
    qk_att = [_mm_nt(q[c], _bdx(k[c], bd_m)) for c in cs]
    dlog, m_t, m_prev = [], [], [m_ref[...]]
    for c in cs:
        d_row = jnp.sum(jnp.where(eye_exp, iexp[c] - fexp[c], 0.0), axis=0, keepdims=True)
        dl = jnp.where(causal, fexp[c] + d_row, -jnp.inf)
        mx = [jnp.max(dl[:, h * 64:(h + 1) * 64], axis=1, keepdims=True) for h in range(N_HEADS)]
        mx = jnp.where(lane < 64, mx[0], jnp.where(lane < 128, mx[1], jnp.where(lane < 192, mx[2], mx[3])))
        dlog.append(dl)
        m_t.append(jnp.maximum(fexp[c] + m_prev[c], mx))
        m_prev.append(m_t[c][CHUNK - 1:CHUNK])
    w0 = [jnp.exp(fexp[c] + m_prev[c] - m_t[c]) for c in cs]
    s = [jnp.exp(dlog[c] - m_t[c]) * qk_att[c] for c in cs]
    f_last = [fexp[c][CHUNK - 1:CHUNK] for c in cs]
    kw = [k[c] * jnp.exp(f_last[c] - fexp[c] + iexp[c] - m_prev[c + 1]) for c in cs]
    w0f = [jnp.exp(f_last[c] + m_prev[c] - m_prev[c + 1]) for c in cs]
    incr = [jnp.where(bd, _mm_tn(kw[c], z_ref[0, sl[c], 512:768]), 0.0) for c in cs]
    sv = [_mm(s[c], _bdx(z_ref[0, sl[c], 512:768], bd_m)) for c in cs]
    c_start, n_start = [cbd_ref[...]], [n_ref[...]]
    for c in cs:
        c_start.append(w0f[c] * c_start[c] + incr[c])
        n_start.append(w0f[c] * n_start[c] + jnp.sum(kw[c], axis=0, keepdims=True))
    cbd_ref[...] = c_start[n_chunks]
    n_ref[...] = n_start[n_chunks]
    m_ref[...] = m_prev[n_chunks]
    num = [w0[c] * _mm(q[c], c_start[c]) + sv[c] for c in cs]
    den = [_mm_xc(w0[c] * (q[c] * n_start[c]) + s[c], ones2) for c in cs]
    hh = jnp.concatenate([num[c] / jnp.maximum(jnp.abs(den[c]), jnp.exp(-m_t[c])) for c in cs], axis=0)
    ms = _mm_xc(hh * hh, pavg2)
    y_ref[0] = (_sigmoid(z_ref[0, :, 768:1024]) * (hh * lax.rsqrt(ms + 1e-6) * ng_ref[...])).astype(BF16)

    @pl.when(j == n_blocks - 1)
    def _():
        _get_block_diag(cbd_ref, co_ref, D_D, D_D)
        no_ref[0] = n_ref[...]
        mo_ref[0] = m_ref[...]
        cbo_ref[0] = carry_ref[...]


def _mlstm_prompt(z3, cb0, c0, n0, m0, cw, cbias, bif, ng, lb):
    bsz, seq, _ = z3.shape
    n_blocks = seq // lb
    cst = pl.BlockSpec((1, N_HEADS, D_D, D_D), lambda i, j: (i, 0, 0, 0))
    row = pl.BlockSpec((1, 1, 256), lambda i, j: (i, 0, 0))
    cbs = pl.BlockSpec((1, CONV_D - 1, 512), lambda i, j: (i, 0, 0))
    return pl.pallas_call(
        functools.partial(_mlstm_prompt_body, n_chunks=lb // CHUNK, n_blocks=n_blocks, lb=lb),
        grid=(bsz, n_blocks),
        in_specs=[pl.BlockSpec((1, lb, ML_W), lambda i, j: (i, j, ML_OFF // ML_W)),
                  pl.BlockSpec((1, lb, MLIF_W), lambda i, j: (i, j, MLIF_OFF // MLIF_W)),
                  cbs, cst, row, row, _layer_spec(cw), _layer_spec(cbias), _layer_spec(bif), _layer_spec(ng)],
        out_specs=[pl.BlockSpec((1, lb, W_GROUP), lambda i, j: (i, j, 0)), cst, row, row, cbs],
        out_shape=[jax.ShapeDtypeStruct((bsz, seq, W_GROUP), BF16),
                   jax.ShapeDtypeStruct((bsz, N_HEADS, D_D, D_D), F32),
                   jax.ShapeDtypeStruct((bsz, 1, 256), F32),
                   jax.ShapeDtypeStruct((bsz, 1, 256), F32),
                   jax.ShapeDtypeStruct((bsz, CONV_D - 1, 512), F32)],
        scratch_shapes=[pltpu.VMEM((256, 256), F32), pltpu.VMEM((1, 256), F32), pltpu.VMEM((1, 256), F32),
                        pltpu.VMEM((CONV_D - 1, 512), F32), pltpu.VMEM((lb, 512), F32)],
        compiler_params=_cp("arbitrary", "arbitrary"),
        name="mlstm_prompt",
    )(z3, z3, cb0, c0, n0, m0, cw.arr, cbias.arr, bif.arr, ng.arr)


T_COLS = 1024


def _to_rows(src_ref, dst_ref, n):
    for i in range(n // T_COLS):
        dst_ref[i * T_COLS:(i + 1) * T_COLS, :] = src_ref[:, i * T_COLS:(i + 1) * T_COLS].T


def _to_cols(src_ref, dst_ref, n):
    for i in range(n // T_COLS):
        dst_ref[:, i * T_COLS:(i + 1) * T_COLS] = src_ref[i * T_COLS:(i + 1) * T_COLS, :].T


def _per_head(x, fn):
    return jnp.concatenate([fn(x[h * 64:(h + 1) * 64], h) for h in range(N_HEADS)], axis=0)


def _expand_heads(x, nb):
    return jnp.concatenate([jnp.broadcast_to(x[h:h + 1], (64, nb)) for h in range(N_HEADS)], axis=0)


def _gla_sample_body(z_ref, s_ref, wa_ref, ba_ref, ng_ref, y_ref, so_ref,
                     st_s, dec_s, k_s, q_s, v_s, o_s, *, steps, nb):
    n_state = N_HEADS * DK_A * DV_A
    st_s[...] = s_ref[...]
    for t in range(steps):
        zt = z_ref[t].T
        q_s[t] = zt[0:128] * DK_A ** -0.5
        k_s[t] = zt[128:256]
        v_s[t] = zt[256:512]
        lg = _log_sigmoid(_mm_hi(wa_ref[...], zt[768:896]) + ba_ref[...]) * (1.0 / GLA_TAU)
        dec_s[t] = jnp.exp(lg)
    o_s[...] = jnp.zeros(o_s.shape, F32)

    def per_key(hk, carry):
        r0 = pl.multiple_of(hk * DV_A, DV_A)
        v0 = pl.multiple_of((hk >> 5) * DV_A, DV_A)
        s = st_s[pl.ds(r0, DV_A), :]
        for t in range(steps):
            s = dec_s[t, pl.ds(hk, 1), :] * s + k_s[t, pl.ds(hk, 1), :] * v_s[t, pl.ds(v0, DV_A), :]
            o_s[t, pl.ds(v0, DV_A), :] += q_s[t, pl.ds(hk, 1), :] * s
        st_s[pl.ds(r0, DV_A), :] = s
        return carry

    lax.fori_loop(0, N_HEADS * DK_A, per_key, 0, unroll=2)
    for t in range(steps):
        g = z_ref[t, :, 512:768].T
        on = _per_head(o_s[t], lambda oh, h: oh * lax.rsqrt(jnp.mean(oh * oh, axis=0, keepdims=True) + 1e-6))
        y_ref[t] = (on * ng_ref[...] * _silu(g)).T.astype(BF16)
    so_ref[...] = st_s[...]


def _gla_sample(zs, s, wa_t, ba_col, ng_col):
    steps, nb, _ = zs.shape
    n_state = N_HEADS * DK_A * DV_A
    full = lambda shape: pl.BlockSpec(shape, lambda i: (0,) * len(shape))
    return pl.pallas_call(
        functools.partial(_gla_sample_body, steps=steps, nb=nb),
        grid=(1,),
        in_specs=[pl.BlockSpec((steps, nb, GLA_W), lambda i: (0, 0, GLA_OFF // GLA_W)), _layer_spec(s),
                  _layer_spec(wa_t), _layer_spec(ba_col), _layer_spec(ng_col)],
        out_specs=[full((steps, nb, W_GROUP)), _layer_spec(s)],
        out_shape=[jax.ShapeDtypeStruct((steps, nb, W_GROUP), BF16), _stacked_like(s)],
        input_output_aliases={1: 1},
        scratch_shapes=[pltpu.VMEM((n_state, nb), F32), pltpu.VMEM((steps, 128, nb), F32),
                        pltpu.VMEM((steps, 128, nb), F32), pltpu.VMEM((steps, 128, nb), F32),
                        pltpu.VMEM((steps, 256, nb), F32), pltpu.VMEM((steps, 256, nb), F32)],
        compiler_params=_cp("arbitrary"),
        name="gla_sample",
    )(zs, s.arr, wa_t.arr, ba_col.arr, ng_col.arr)


def _pool_sample_body(z_ref, buf_ref, w_ref, sc_ref, y_ref, bo_ref, *, steps, nb, start_pos):
    ext = [buf_ref[:, j * W_GROUP:(j + 1) * W_GROUP] for j in range(POOL_BUF)] + [z_ref[t] for t in range(steps)]
    lane = _iota((nb, W_GROUP), 1)

    def by_group(vals):
        return jnp.where(lane < 64, vals[0], jnp.where(lane < 128, vals[1], jnp.where(lane < 192, vals[2], vals[3])))

    for t in range(steps):
        r = POOL_BUF + t
        acc = ext[r]
        sums = {}
        for back in range(1, max(POOL_WINDOWS)):
            acc = acc + ext[r - back]
            sums[back + 1] = acc
        win = by_group([sums[w] for w in POOL_WINDOWS])
        cnt = by_group([float(min(start_pos + t + 1, w)) for w in POOL_WINDOWS])
        y_ref[t] = (_mm(win / cnt - ext[r], w_ref[...]) * sc_ref[...]).astype(BF16)
    for j in range(POOL_BUF):
        bo_ref[:, j * W_GROUP:(j + 1) * W_GROUP] = ext[steps + j]


def _pool_sample(zs, buf, wbd, scale, start_pos):
    steps, nb, _ = zs.shape
    full = lambda shape: pl.BlockSpec(shape, lambda i: (0,) * len(shape))
    return pl.pallas_call(
        functools.partial(_pool_sample_body, steps=steps, nb=nb, start_pos=start_pos),
        grid=(1,),
        in_specs=[pl.BlockSpec((steps, nb, W_GROUP), lambda i: (0, 0, POOL_OFF // W_GROUP)),
                  _layer_spec(buf), _layer_spec(wbd), _layer_spec(scale)],
        out_specs=[full((steps, nb, W_GROUP)), _layer_spec(buf)],
        out_shape=[jax.ShapeDtypeStruct((steps, nb, W_GROUP), BF16), _stacked_like(buf)],
        input_output_aliases={1: 1},
        compiler_params=_cp("arbitrary"),
        name="pool_sample",
    )(zs, buf.arr, wbd.arr, scale.arr)


def _rwkv_sample_body(z_ref, sh_ref, s_ref, mu_ref, w0_ref, w2_ref, a0_ref, a2_ref, g2_ref, kk_ref, ka_ref,
                      rk_ref, lg_ref, lb_ref, y_ref, so_ref, sho_ref,
                      st_s, r_s, k_s, v_s, kk_s, b_s, w_s, g_s, o_s, *, steps, nb):
    n_state = N_HEADS * N_C * N_C
    st_s[...] = s_ref[...]
    prev = sh_ref[...].T
    for t in range(steps):
        f = z_ref[t].T
        mixed = f + (prev - f) * mu_ref[...]
        prev = f
        k = mixed[256:512]
        low = mixed[768:896]
        w_s[t] = jnp.exp(-_sigmoid(w0_ref[...] + _mm_hi(w2_ref[...], jnp.tanh(low))) * math.exp(-0.5))
        a = _sigmoid(a0_ref[...] + _mm_hi(a2_ref[...], low))
        g_s[t] = _mm_hi(g2_ref[...], _sigmoid(low))
        kk = _per_head(k * kk_ref[...], lambda kh, h: kh * lax.rsqrt(
            jnp.maximum(jnp.sum(kh * kh, axis=0, keepdims=True), 1e-24)))
        r_s[t] = mixed[0:256]
        v_s[t] = mixed[512:768]
        kk_s[t] = kk
        k_s[t] = k * (1.0 + (a - 1.0) * ka_ref[...])
        b_s[t] = kk * a

    def per_value(hv, carry):
        r0 = pl.multiple_of(hv * N_C, N_C)
        hs = pl.ds(pl.multiple_of((hv >> 6) * N_C, N_C), N_C)
        s = st_s[pl.ds(r0, N_C), :]
        for t in range(steps):
            sa = -jnp.sum(s * kk_s[t, hs, :], axis=0, keepdims=True)
            s = s * w_s[t, hs, :] + sa * b_s[t, hs, :] + v_s[t, pl.ds(hv, 1), :] * k_s[t, hs, :]
            o_s[t, pl.ds(hv, 1), :] = jnp.sum(s * r_s[t, hs, :], axis=0, keepdims=True)
        st_s[pl.ds(r0, N_C), :] = s
        return carry

    lax.fori_loop(0, N_HEADS * N_C, per_value, 0, unroll=4)
    for t in range(steps):
        def norm(oh, h):
            mu = jnp.mean(oh, axis=0, keepdims=True)
            xc = oh - mu
            return xc * lax.rsqrt(jnp.mean(xc * xc, axis=0, keepdims=True) + RWKV_LN_EPS)
        on = _per_head(o_s[t], norm) * lg_ref[...] + lb_ref[...]
        rk = r_s[t] * k_s[t] * rk_ref[...]
        v = v_s[t]
        bonus = _per_head(rk, lambda x, h: jnp.sum(x, axis=0, keepdims=True) * v[h * 64:(h + 1) * 64])
        y_ref[t] = ((on + bonus) * g_s[t]).T.astype(BF16)
    so_ref[...] = st_s[...]
    sho_ref[...] = z_ref[steps - 1]


def _rwkv_sample(zs, sh, s, p):
    steps, nb, _ = zs.shape
    n_state = N_HEADS * N_C * N_C
    full = lambda shape: pl.BlockSpec(shape, lambda i: (0,) * len(shape))
    blk = pltpu.VMEM((steps, 256, nb), F32)
    params = [p[k + "_c"] for k in _RWKV_PARAMS]
    return pl.pallas_call(
        functools.partial(_rwkv_sample_body, steps=steps, nb=nb),
        grid=(1,),
        in_specs=[pl.BlockSpec((steps, nb, RWKV_W), lambda i: (0, 0, RWKV_OFF // RWKV_W)), _layer_spec(sh),
                  _layer_spec(s)] + [_layer_spec(q) for q in params],
        out_specs=[full((steps, nb, W_GROUP)), _layer_spec(s), _layer_spec(sh)],
        out_shape=[jax.ShapeDtypeStruct((steps, nb, W_GROUP), BF16), _stacked_like(s), _stacked_like(sh)],
        input_output_aliases={2: 1, 1: 2},
        scratch_shapes=[pltpu.VMEM((n_state, nb), F32), blk, blk, blk, blk, blk, blk, blk, blk],
        compiler_params=_cp("arbitrary"),
        name="rwkv_sample",
    )(zs, sh.arr, s.arr, *[q.arr for q in params])


def _mlstm_sample_body(z_ref, zif_ref, cb_ref, c_ref, n_ref, m_ref, cw_ref, cbias_ref, bif_ref, ng_ref,
                       y_ref, co_ref, no_ref, mo_ref, cbo_ref,
                       ct_s, q_s, k_s, v_s, wf_s, den_s, em_s, o_s, *, steps, nb):
    n_state = N_HEADS * D_D * D_D
    ct_s[...] = c_ref[...]
    n = n_ref[...].T
    m = m_ref[...].T[0:N_HEADS]
    hist = [cb_ref[:, j * 512:(j + 1) * 512].T for j in range(CONV_D - 1)]
    for t in range(steps):
        hist = hist + [z_ref[t, :, 0:512].T]
        conv = cbias_ref[...]
        for jj in range(CONV_D):
            conv = conv + cw_ref[:, jj:jj + 1] * hist[jj]
        hist = hist[1:]
        act = _silu(conv)
        q = act[0:256] * D_D ** -0.5
        k = act[256:512]
        gates = zif_ref[t].T + bif_ref[...]
        ig = gates[0:N_HEADS]
        lf = _log_sigmoid(gates[N_HEADS:2 * N_HEADS])
        m_new = jnp.maximum(lf + m, ig)
        wf = _expand_heads(jnp.exp(lf + m - m_new), nb)
        kw = _expand_heads(jnp.exp(ig - m_new), nb) * k
        m = m_new
        n = wf * n + kw
        den = _per_head(q * n, lambda x, h: jnp.broadcast_to(jnp.sum(x, axis=0, keepdims=True), (64, nb)))
        q_s[t] = q
        k_s[t] = kw
        v_s[t] = z_ref[t, :, 512:768].T
        wf_s[t] = wf
        den_s[t] = den
        em_s[t] = _expand_heads(jnp.exp(-m_new), nb)
    o_s[...] = jnp.zeros(o_s.shape, F32)

    def per_key(hd, carry):
        r0 = pl.multiple_of(hd * D_D, D_D)
        e0 = pl.multiple_of((hd >> 6) * D_D, D_D)
        c = ct_s[pl.ds(r0, D_D), :]
        for t in range(steps):
            c = wf_s[t, pl.ds(hd, 1), :] * c + k_s[t, pl.ds(hd, 1), :] * v_s[t, pl.ds(e0, D_D), :]
            o_s[t, pl.ds(e0, D_D), :] += q_s[t, pl.ds(hd, 1), :] * c
        ct_s[pl.ds(r0, D_D), :] = c
        return carry

    lax.fori_loop(0, N_HEADS * D_D, per_key, 0, unroll=2)
    for t in range(steps):
        hh = o_s[t] / jnp.maximum(jnp.abs(den_s[t]), em_s[t])
        hn = _per_head(hh, lambda x, h: x * lax.rsqrt(jnp.mean(x * x, axis=0, keepdims=True) + 1e-6))
        y_ref[t] = (_sigmoid(z_ref[t, :, 768:1024].T) * (hn * ng_ref[...])).T.astype(BF16)
    co_ref[...] = ct_s[...]
    no_ref[...] = n.T
    mo_ref[...] = jnp.concatenate([m, jnp.zeros((8 - N_HEADS, nb), F32)], axis=0)
    for j in range(CONV_D - 1):
        cbo_ref[:, j * 512:(j + 1) * 512] = hist[j].T


def _mlstm_sample(zs, cb, c, n, m_pad, cw_t, cbias_col, bif_col, ng_col):
    steps, nb, _ = zs.shape
    n_state = N_HEADS * D_D * D_D
    full = lambda shape: pl.BlockSpec(shape, lambda i: (0,) * len(shape))
    blk = pltpu.VMEM((steps, 256, nb), F32)
    return pl.pallas_call(
        functools.partial(_mlstm_sample_body, steps=steps, nb=nb),
        grid=(1,),
        in_specs=[pl.BlockSpec((steps, nb, ML_W), lambda i: (0, 0, ML_OFF // ML_W)),
                  pl.BlockSpec((steps, nb, MLIF_W), lambda i: (0, 0, MLIF_OFF // MLIF_W)),
                  _layer_spec(cb), _layer_spec(c), _layer_spec(n), _layer_spec(m_pad),
                  _layer_spec(cw_t), _layer_spec(cbias_col), _layer_spec(bif_col), _layer_spec(ng_col)],
        out_specs=[full((steps, nb, W_GROUP)), _layer_spec(c), _layer_spec(n), full((8, nb)), _layer_spec(cb)],
        out_shape=[jax.ShapeDtypeStruct((steps, nb, W_GROUP), BF16), _stacked_like(c), _stacked_like(n),
                   jax.ShapeDtypeStruct((8, nb), F32), _stacked_like(cb)],
        input_output_aliases={3: 1, 4: 2, 2: 4},
        scratch_shapes=[pltpu.VMEM((n_state, nb), F32), blk, blk, blk, blk, blk, blk, blk],
        compiler_params=_cp("arbitrary"),
        name="mlstm_sample",
    )(zs, zs, cb.arr, c.arr, n.arr, m_pad.arr, cw_t.arr, cbias_col.arr, bif_col.arr, ng_col.arr)


def _pad_cols(x, n):
    return jnp.pad(x, [(0, 0)] * (x.ndim - 1) + [(0, n - x.shape[-1])])


def _rows_at(x, row0, n_rows):
    return jnp.pad(x, ((0, 0), (row0, n_rows - row0 - x.shape[1]), (0, 0)))


def _stacked_params(w):
    gla, pool, rwkv, ml = jnp.split(w["w_in"], [784, 784 + 256, 784 + 256 + 832], axis=2)
    w_in_p = jnp.concatenate([_pad_cols(gla, GLA_W), _pad_cols(rwkv, RWKV_W), pool, ml[:, :, :1024],
                              _pad_cols(ml[:, :, 1024:], MLIF_W)], axis=2).astype(BF16)
    col = lambda v: v.reshape(DEPTH, -1, 1)
    row = lambda v: v.reshape(DEPTH, 1, -1)
    tr = lambda m: m.transpose(0, 2, 1)
    wa = _rows_at(w["gla_w_a2"], 0, 128)
    w2 = _rows_at(w["rwkv_w2"], 0, 128)
    a2 = _rows_at(w["rwkv_a2"], R_W, 128)
    g2 = _rows_at(w["rwkv_g2"], R_W + R_AA, 128)
    mu = _pad_cols(w["rwkv_mu"], RWKV_W)
    bif = _pad_cols(jnp.concatenate([w["mlstm_b_i"], w["mlstm_b_f"]], axis=1), 128)
    wbd = jnp.zeros((DEPTH, 256, 256), F32)
    for gi in range(4):
        wbd = wbd.at[:, gi * 64:(gi + 1) * 64, gi * 64:(gi + 1) * 64].set(w["pool_w"][:, gi])
    vecs = dict(w0=w["rwkv_w0"], a0=w["rwkv_a0"], k_k=w["rwkv_k_k"], k_a=w["rwkv_k_a"], r_k=w["rwkv_r_k"],
                ln_g=w["rwkv_ln_g"], ln_b=w["rwkv_ln_b"], mu=mu)
    rw = {k: row(v) for k, v in vecs.items()}
    rw.update({k + "_c": col(v) for k, v in vecs.items()})
    rw.update(w2=w2, a2=a2, g2=g2, w2_c=tr(w2), a2_c=tr(a2), g2_c=tr(g2))
    return dict(
        w_in=w_in_p, w_out=w["w_out"].astype(BF16), ln1_g=row(w["ln1_g"]), ln1_b=row(w["ln1_b"]),
        w_up=w["ffn_w_up"].astype(BF16), ffn_cw=w["ffn_conv_w"], ffn_cb=row(w["ffn_conv_b"]),
        w_down=w["ffn_w_down"].astype(BF16), ln2_g=row(w["ln2_g"]), ln2_b=row(w["ln2_b"]),
        gla_wa=wa, gla_wa_t=tr(wa), gla_ba=row(w["gla_b_a"]), gla_ba_c=col(w["gla_b_a"]),
        gla_ng=row(w["gla_norm_g"]), gla_ng_c=col(w["gla_norm_g"]),
        pool_w=wbd, pool_scale=row(w["pool_scale"]), rwkv=rw,
        ml_cw=w["mlstm_conv_w"], ml_cw_t=tr(w["mlstm_conv_w"]), ml_cb=row(w["mlstm_conv_b"]),
        ml_cb_c=col(w["mlstm_conv_b"]), ml_bif=row(bif), ml_bif_c=col(bif),
        ml_ng=row(w["mlstm_norm_g"]), ml_ng_c=col(w["mlstm_norm_g"]))


def _layer_view(stacked, l):
    return {k: (_layer_view(v, l) if isinstance(v, dict) else _Layer(v, l)) for k, v in stacked.items()}


MATRIX_STATES = (0, 2, 4)
RWKV_BLOCK = 512
GLA_BLOCK = 512
MLSTM_BLOCK = 512
POOL_BLOCK = 1024
DENSE_TILE = 1024
FFN_TILE = 512


def _prompt_layer(x, p):
    bsz, seq, d = x.shape
    z = _linear(x.reshape(bsz * seq, d), p["w_in"], DENSE_TILE).reshape(bsz, seq, ZC)
    zeros = lambda *s: jnp.zeros((bsz,) + s, F32)
    y_a, s_gla = _gla_prompt(z, zeros(N_HEADS, DK_A, DV_A), p["gla_wa"], p["gla_ba"], p["gla_ng"], GLA_BLOCK)
    y_b, s_pool = _pool_prompt(z, zeros(POOL_BUF, W_GROUP), p["pool_w"], p["pool_scale"], POOL_BLOCK, 0)
    y_c, s_rwkv, s_shift = _rwkv_prompt(z, zeros(1, RWKV_W), zeros(N_HEADS, N_C, N_C), p["rwkv"], RWKV_BLOCK)
    y_d, s_c, s_n, s_m, s_conv = _mlstm_prompt(z, zeros(CONV_D - 1, 512), zeros(N_HEADS, D_D, D_D), zeros(1, 256),
                                               zeros(1, 256), p["ml_cw"], p["ml_cb"], p["ml_bif"], p["ml_ng"],
                                               MLSTM_BLOCK)
    x2, s_ffn = _ffn_prompt([y_a, y_b, y_c, y_d], x, zeros(FFN_CONV - 1, 2 * D_FF), p, FFN_TILE)
    states = (s_gla, s_pool, s_rwkv, s_shift[:, :, :RWKV_COLS], s_c, s_n.reshape(bsz, N_HEADS, D_D),
              s_m[:, 0, ::D_D], s_conv, s_ffn)
    return x2, states


def _sample_layer(x, st, p, steps, nb):
    s_gla, s_pool, s_rwkv, s_shift, s_c, s_n, s_m, s_conv, s_ffn = st
    z = _linear(x, p["w_in"], steps * nb).reshape(steps, nb, ZC)
    y_a, n_gla = _gla_sample(z, s_gla, p["gla_wa_t"], p["gla_ba_c"], p["gla_ng_c"])
    y_b, n_pool = _pool_sample(z, s_pool, p["pool_w"], p["pool_scale"], PAST_LEN)
    y_c, n_rwkv, n_shift = _rwkv_sample(z, s_shift, s_rwkv, p["rwkv"])
    y_d, n_c, n_n, n_m, n_conv = _mlstm_sample(z, s_conv, s_c, s_n, s_m, p["ml_cw_t"], p["ml_cb_c"], p["ml_bif_c"],
                                               p["ml_ng_c"])
    flat = lambda y: y.reshape(steps * nb, W_GROUP)
    x2, n_ffn = _ffn_sample([flat(y_a), flat(y_b), flat(y_c), flat(y_d)], x, s_ffn, p, nb, steps)
    return x2, (n_gla, n_pool, n_rwkv, n_shift, n_c, n_n, n_m, n_conv, n_ffn)


def kernel(x_prompt, x_sample, state_gla, state_pool, state_rwkv, state_rwkv_shift, state_mlstm_c, state_mlstm_n, state_mlstm_m, state_mlstm_conv, state_ffn_conv, w_in, gla_w_a2, gla_b_a, gla_norm_g, pool_w, pool_scale, rwkv_mu, rwkv_w0, rwkv_w2, rwkv_a0, rwkv_a2, rwkv_g2, rwkv_k_k, rwkv_k_a, rwkv_r_k, rwkv_ln_g, rwkv_ln_b, mlstm_conv_w, mlstm_conv_b, mlstm_b_i, mlstm_b_f, mlstm_norm_g, w_out, ln1_g, ln1_b, ffn_w_up, ffn_conv_w, ffn_conv_b, ffn_w_down, ln2_g, ln2_b):
    w = dict(w_in=w_in, gla_w_a2=gla_w_a2, gla_b_a=gla_b_a, gla_norm_g=gla_norm_g, pool_w=pool_w,
             pool_scale=pool_scale, rwkv_mu=rwkv_mu, rwkv_w0=rwkv_w0, rwkv_w2=rwkv_w2, rwkv_a0=rwkv_a0,
             rwkv_a2=rwkv_a2, rwkv_g2=rwkv_g2, rwkv_k_k=rwkv_k_k, rwkv_k_a=rwkv_k_a, rwkv_r_k=rwkv_r_k,
             rwkv_ln_g=rwkv_ln_g, rwkv_ln_b=rwkv_ln_b, mlstm_conv_w=mlstm_conv_w, mlstm_conv_b=mlstm_conv_b,
             mlstm_b_i=mlstm_b_i, mlstm_b_f=mlstm_b_f, mlstm_norm_g=mlstm_norm_g, w_out=w_out, ln1_g=ln1_g,
             ln1_b=ln1_b, ffn_w_up=ffn_w_up, ffn_conv_w=ffn_conv_w, ffn_conv_b=ffn_conv_b, ffn_w_down=ffn_w_down,
             ln2_g=ln2_g, ln2_b=ln2_b)
    sample_states = (state_gla, state_pool, state_rwkv, state_rwkv_shift, state_mlstm_c, state_mlstm_n,
                     state_mlstm_m, state_mlstm_conv, state_ffn_conv)
    nb, steps, d = x_sample.shape
    flat_in = [s.reshape(DEPTH, nb, -1) for s in sample_states]
    flat_in[8] = state_ffn_conv
    for i in MATRIX_STATES:
        flat_in[i] = flat_in[i].transpose(0, 2, 1)
    flat_in[3] = _pad_cols(flat_in[3], RWKV_W)
    flat_in[6] = _pad_cols(flat_in[6], 128)
    stacked = _stacked_params(w)
    yp = x_prompt
    ys = x_sample.transpose(1, 0, 2).reshape(steps * nb, d)
    acc_p = [[] for _ in sample_states]
    m_new = []
    flat = list(flat_in)
    for l in range(DEPTH):
        p = _layer_view(stacked, l)
        yp, st_p = _prompt_layer(yp, p)
        ys, st_s = _sample_layer(ys, tuple(_Layer(s, l) for s in flat), p, steps, nb)
        for i in range(len(sample_states)):
            acc_p[i].append(st_p[i])
            if i == 6:
                m_new.append(st_s[i])
            else:
                flat[i] = st_s[i]
    ys = ys.reshape(steps, nb, d).transpose(1, 0, 2)
    out_s = list(flat)
    for i in MATRIX_STATES:
        out_s[i] = out_s[i].transpose(0, 2, 1)
    out_s[3] = out_s[3][:, :, :RWKV_COLS]
    out_s[6] = jnp.stack(m_new)[:, 0:N_HEADS, :].transpose(0, 2, 1)
    out = [yp, ys]
    for sp, ss, ref in zip(acc_p, out_s, sample_states):
        out.append(jnp.stack(sp))
        out.append(ss.reshape(ref.shape))
    return tuple(out)
```
